```python
import jax, jax.numpy as jnp
from jax import lax
import numpy as np

D_MODEL = 1024
BATCH = 4
SEQ = 4096
DEPTH = 2

GRID_W = 64
CTX_LEN = 256
D_INNER = 2 * D_MODEL
BRANCH_W = D_INNER // 4
RG_W = BRANCH_W
RG_HEADS = 8
RG_HD = RG_W // RG_HEADS
RG_CONV = 4
RG_C = 8.0
SC_W = BRANCH_W
SC_CONV = 3
FN_W = BRANCH_W
FN_GROUPS = 4
FN_GD = FN_W // FN_GROUPS
SSD_W = BRANCH_W
SSD_HD = 64
SSD_HEADS = SSD_W // SSD_HD
SSD_GROUPS = 2
SSD_STATE = 64
SSD_CONV = 4
SSD_CHUNK = 128
SSD_XBC = SSD_W + 2 * SSD_GROUPS * SSD_STATE
SSD_DT = 2 * SSD_HEADS
CONV4_PAD = (2, 1)
NORM_EPS = 1e-6
COL_WIDTHS = (RG_W, SSD_XBC, SSD_DT, RG_W, SSD_W, SC_W, SC_W, SC_W, SC_W, FN_W, FN_W)
SCAN_COLS = RG_W + SSD_XBC + SSD_DT
IN_COLS = sum(COL_WIDTHS)

kernel_name = "hybrid_rglru_conv_fourier_ssd_block"


def split_cols(y, widths):
    idx = [int(i) for i in np.cumsum(widths)[:-1]]
    return jnp.split(y, idx, axis=-1)


def rmsnorm(x, w):
    xf = x.astype(jnp.float32)
    y = xf * lax.rsqrt(jnp.mean(xf * xf, axis=-1, keepdims=True) + NORM_EPS)
    return (y * w.astype(jnp.float32)).astype(x.dtype)


def dwconv(x, w, b, pad):
    y = lax.conv_general_dilated(x, w[:, None, :].astype(x.dtype), window_strides=(1,), padding=[pad],
                                 dimension_numbers=('NWC', 'WIO', 'NWC'), feature_group_count=x.shape[-1])
    return y if b is None else y + b.astype(y.dtype)


def adaln(cvec, ada_w, ada_b):
    m = jax.nn.silu(cvec) @ ada_w + ada_b
    m = m.reshape(-1, 1, 3 * D_MODEL)
    return jnp.split(m, 3, axis=-1)


def linear_scan(a, v, h0, reverse):
    def comb(l, r):
        return (l[0] * r[0], r[0] * l[1] + r[1])
    acum, h = lax.associative_scan(comb, (a, v), axis=1, reverse=reverse)
    return h + acum * h0[:, None]


def rglru(u_raw, lp, h0, with_output):
    u = dwconv(u_raw, lp['rg_conv_w'], lp['rg_conv_b'], CONV4_PAD).astype(jnp.float32)
    bsz, t, _ = u.shape
    ub = u.reshape(bsz, t, RG_HEADS, RG_HD)
    ys, finals = [], []
    for d, reverse in enumerate((False, True)):
        r = jax.nn.sigmoid(jnp.einsum('bthi,hij->bthj', ub, lp['rg_gate_a_w'][d].astype(jnp.float32)).reshape(bsz, t, RG_W)
                           + lp['rg_gate_a_b'][d].astype(jnp.float32))
        i = jax.nn.sigmoid(jnp.einsum('bthi,hij->bthj', ub, lp['rg_gate_x_w'][d].astype(jnp.float32)).reshape(bsz, t, RG_W)
                           + lp['rg_gate_x_b'][d].astype(jnp.float32))
        log_a = -RG_C * r * jax.nn.softplus(-lp['rg_lambda'][d].astype(jnp.float32))
        v = jnp.sqrt(-jnp.expm1(2.0 * log_a)) * (i * u)
        h = linear_scan(jnp.exp(log_a), v, h0[d], reverse)
        ys.append(h)
        finals.append(h[:, 0] if reverse else h[:, -1])
    y = ys[0] + ys[1] if with_output else None
    return y, jnp.stack(finals)


def to_chunks(z):
    return z.reshape(z.shape[0], z.shape[1] // SSD_CHUNK, SSD_CHUNK, *z.shape[2:])


def ssd_inputs(xbc_raw, dt_raw, lp):
    xbc = jax.nn.silu(dwconv(xbc_raw, lp['ssd_conv_w'], lp['ssd_conv_b'], CONV4_PAD)).astype(jnp.float32)
    xs, bm, cm = split_cols(xbc, (SSD_W, SSD_GROUPS * SSD_STATE, SSD_GROUPS * SSD_STATE))
    bsz, t, _ = xs.shape
    rep = SSD_HEADS // SSD_GROUPS
    xs = xs.reshape(bsz, t, SSD_HEADS, SSD_HD)
    bm = jnp.repeat(bm.reshape(bsz, t, SSD_GROUPS, SSD_STATE), rep, axis=2)
    cm = jnp.repeat(cm.reshape(bsz, t, SSD_GROUPS, SSD_STATE), rep, axis=2)
    dt = jax.nn.softplus(dt_raw.astype(jnp.float32).reshape(bsz, t, 2, SSD_HEADS)
                         + lp['ssd_dt_bias'].astype(jnp.float32))
    a = -jnp.exp(lp['ssd_a_log'].astype(jnp.float32))
    return xs, bm, cm, dt, a


def chunk_states(xdt, bmc, cs, h0):
    decay_to_end = jnp.exp(cs[:, :, -1:] - cs)
    states = jnp.einsum('bcqhn,bcqh,bcqhp->bchpn', bmc, decay_to_end, xdt)
    chunk_decay = jnp.exp(cs[:, :, -1])

    def step(h, inp):
        dec, st = inp
        return dec[..., None, None] * h + st, h

    h_final, h_enter = lax.scan(step, h0, (jnp.moveaxis(chunk_decay, 1, 0), jnp.moveaxis(states, 1, 0)))
    return jnp.moveaxis(h_enter, 0, 1), h_final


def ssd_scan(xs, dt, a, bm, cm, h0, reverse, with_output):
    if reverse:
        xs, dt, bm, cm = xs[:, ::-1], dt[:, ::-1], bm[:, ::-1], cm[:, ::-1]
    bsz, t = xs.shape[0], xs.shape[1]
    xdt = to_chunks(xs * dt[..., None])
    cs = jnp.cumsum(to_chunks(dt * a), axis=2)
    bmc = to_chunks(bm)
    h_enter, h_final = chunk_states(xdt, bmc, cs, h0)
    if not with_output:
        return None, h_final
    cmc = to_chunks(cm)
    lower = jnp.tril(jnp.ones((SSD_CHUNK, SSD_CHUNK), dtype=bool))[None, None, :, :, None]
    seg = cs[:, :, :, None, :] - cs[:, :, None, :, :]
    lmat = jnp.where(lower, jnp.exp(jnp.where(lower, seg, 0.0)), 0.0)
    scores = jnp.einsum('bcihn,bcjhn->bcijh', cmc, bmc) * lmat
    y = (jnp.einsum('bcijh,bcjhp->bcihp', scores, xdt)
         + jnp.einsum('bcihn,bchpn,bcih->bcihp', cmc, h_enter, jnp.exp(cs)))
    y = y.reshape(bsz, t, SSD_HEADS, SSD_HD)
    if reverse:
        y = y[:, ::-1]
    return y, h_final


def ssd_branch(xbc_raw, dt_raw, z, lp, h0, with_output):
    xs, bm, cm, dt, a = ssd_inputs(xbc_raw, dt_raw, lp)
    yf, hf = ssd_scan(xs, dt[:, :, 0], a[0], bm, cm, h0[0], False, with_output)
    yb, hb = ssd_scan(xs, dt[:, :, 1], a[1], bm, cm, h0[1], True, with_output)
    states = jnp.stack((hf, hb))
    if not with_output:
        return None, states
    bsz, t = xs.shape[0], xs.shape[1]
    y = yf + yb + lp['ssd_d'].astype(jnp.float32)[:, None] * xs
    y = y.reshape(bsz, t, SSD_W) * jax.nn.silu(z.astype(jnp.float32))
    return rmsnorm(y, lp['ssd_norm_w']), states


def shortconv_branch(bg, cg, xs, lp, rows):
    v = cg * xs
    if rows is None:
        vc = dwconv(v, lp['sc_conv_w'], None, (1, 1))
    else:
        bsz, t, ch = v.shape
        vc = dwconv(v.reshape(bsz * rows, GRID_W, ch), lp['sc_conv_w'], None, (1, 1)).reshape(bsz, t, ch)
    return bg * vc


def fourier_branch(xs):
    bsz, t, _ = xs.shape
    v = xs.astype(jnp.float32).reshape(bsz, t, FN_GROUPS, FN_GD)
    return jnp.fft.fft2(v, axes=(1, 3), norm='ortho').real.reshape(bsz, t, FN_W)


def mix(h, lp, rows, rg_h0, ssd_h0):
    proj = h @ lp['w_in']
    rg_x, ssd_xbc, ssd_dt, rg_g, ssd_z, sc_b, sc_c, sc_x, sc_g, fn_x, fn_g = split_cols(proj, COL_WIDTHS)
    y_rg, rg_st = rglru(rg_x, lp, rg_h0, True)
    y_ssd, ssd_st = ssd_branch(ssd_xbc, ssd_dt, ssd_z, lp, ssd_h0, True)
    y_sc = shortconv_branch(sc_b, sc_c, sc_x, lp, rows)
    y_fn = fourier_branch(fn_x)
    ycat = jnp.concatenate([y_rg * jax.nn.silu(rg_g), y_sc * jax.nn.silu(sc_g),
                            y_fn * jax.nn.silu(fn_g), y_ssd], axis=-1)
    return ycat.astype(h.dtype) @ lp['w_out'], rg_st, ssd_st


def context_states(h, lp):
    proj = h @ lp['w_in'][:, :SCAN_COLS]
    rg_x, ssd_xbc, ssd_dt = split_cols(proj, (RG_W, SSD_XBC, SSD_DT))
    bsz = h.shape[0]
    _, rg_st = rglru(rg_x, lp, jnp.zeros((2, bsz, RG_W), jnp.float32), False)
    _, ssd_st = ssd_branch(ssd_xbc, ssd_dt, None, lp, jnp.zeros((2, bsz, SSD_HEADS, SSD_HD, SSD_STATE), jnp.float32), False)
    return rg_st, ssd_st


def setup_inputs(seed: int = 0) -> dict:
    key = jax.random.key(seed)
    ks = jax.random.split(key, 24)
    f32 = jnp.float32
    nrm = lambda k, shape, s: jax.random.normal(k, shape, f32) * s
    a8 = jax.random.uniform(ks[14], (DEPTH, 2, RG_W), f32, minval=0.9, maxval=0.999)
    a_base = a8 ** (1.0 / RG_C)
    dt0 = jnp.exp(jax.random.uniform(ks[18], (DEPTH, 2, SSD_HEADS), f32, minval=jnp.log(1e-3), maxval=jnp.log(1e-1)))
    return {
        'x': nrm(ks[0], (BATCH, SEQ, D_MODEL), 1.0),
        'c': nrm(ks[1], (BATCH, D_MODEL), 1.0),
        'ctx': nrm(ks[2], (BATCH, CTX_LEN, D_MODEL), 1.0),
        'c_ctx': nrm(ks[3], (D_MODEL,), 1.0),
        'ada_w': nrm(ks[4], (DEPTH, D_MODEL, 3 * D_MODEL), 0.5 * D_MODEL ** -0.5),
        'ada_b': nrm(ks[5], (DEPTH, 3 * D_MODEL), 0.02),
        'norm_w': 1.0 + nrm(ks[6], (DEPTH, D_MODEL), 0.02),
        'w_in': nrm(ks[7], (DEPTH, D_MODEL, IN_COLS), D_MODEL ** -0.5),
        'w_out': nrm(ks[8], (DEPTH, D_INNER, D_MODEL), D_INNER ** -0.5),
        'rg_conv_w': nrm(ks[9], (DEPTH, RG_CONV, RG_W), RG_CONV ** -0.5),
        'rg_conv_b': nrm(ks[10], (DEPTH, RG_W), 0.02),
        'rg_gate_a_w': nrm(ks[11], (DEPTH, 2, RG_HEADS, RG_HD, RG_HD), RG_HD ** -0.5),
        'rg_gate_a_b': nrm(ks[12], (DEPTH, 2, RG_W), 0.02),
        'rg_gate_x_w': nrm(ks[13], (DEPTH, 2, RG_HEADS, RG_HD, RG_HD), RG_HD ** -0.5),
        'rg_gate_x_b': nrm(ks[15], (DEPTH, 2, RG_W), 0.02),
        'rg_lambda': jnp.log(a_base) - jnp.log1p(-a_base),
        'sc_conv_w': nrm(ks[16], (DEPTH, SC_CONV, SC_W), SC_CONV ** -0.5),
        'ssd_conv_w': nrm(ks[17], (DEPTH, SSD_CONV, SSD_XBC), SSD_CONV ** -0.5),
        'ssd_conv_b': nrm(ks[19], (DEPTH, SSD_XBC), 0.02),
        'ssd_dt_bias': dt0 + jnp.log(-jnp.expm1(-dt0)),
        'ssd_a_log': jnp.log(jax.random.uniform(ks[20], (DEPTH, 2, SSD_HEADS), f32, minval=1.0, maxval=16.0)),
        'ssd_d': 1.0 + nrm(ks[21], (DEPTH, SSD_HEADS), 0.1),
        'ssd_norm_w': 1.0 + nrm(ks[22], (DEPTH, SSD_W), 0.02),
        'final_norm_w': 1.0 + nrm(ks[23], (D_MODEL,), 0.02),
    }


def reference(x, c, ctx, c_ctx, ada_w, ada_b, norm_w, w_in, w_out, rg_conv_w, rg_conv_b, rg_gate_a_w, rg_gate_a_b,
              rg_gate_x_w, rg_gate_x_b, rg_lambda, sc_conv_w, ssd_conv_w, ssd_conv_b, ssd_dt_bias, ssd_a_log, ssd_d,
              ssd_norm_w, final_norm_w):
    bsz = x.shape[0]
    rows = x.shape[1] // GRID_W
    for l in range(DEPTH):
        lp = dict(w_in=w_in[l], w_out=w_out[l], rg_conv_w=rg_conv_w[l], rg_conv_b=rg_conv_b[l],
                  rg_gate_a_w=rg_gate_a_w[l], rg_gate_a_b=rg_gate_a_b[l], rg_gate_x_w=rg_gate_x_w[l],
                  rg_gate_x_b=rg_gate_x_b[l], rg_lambda=rg_lambda[l], sc_conv_w=sc_conv_w[l],
                  ssd_conv_w=ssd_conv_w[l], ssd_conv_b=ssd_conv_b[l], ssd_dt_bias=ssd_dt_bias[l],
                  ssd_a_log=ssd_a_log[l], ssd_d=ssd_d[l], ssd_norm_w=ssd_norm_w[l])
        shift_c, scale_c, gate_c = adaln(c_ctx, ada_w[l], ada_b[l])
        h_ctx = rmsnorm(ctx, norm_w[l]) * (1.0 + scale_c) + shift_c
        if l < DEPTH - 1:
            out_ctx, rg_st, ssd_st = mix(h_ctx, lp, None, jnp.zeros((2, bsz, RG_W), jnp.float32),
                                         jnp.zeros((2, bsz, SSD_HEADS, SSD_HD, SSD_STATE), jnp.float32))
            new_ctx = ctx + gate_c * out_ctx
        else:
            rg_st, ssd_st = context_states(h_ctx, lp)
            new_ctx = ctx
        shift, scale, gate = adaln(c, ada_w[l], ada_b[l])
        h = rmsnorm(x, norm_w[l]) * (1.0 + scale) + shift
        out, _, _ = mix(h, lp, rows, rg_st, ssd_st)
        x = x + gate * out
        ctx = new_ctx
    return rmsnorm(x, final_norm_w)
```

```python
import functools

import numpy as np
import jax
import jax.numpy as jnp
from jax import lax
from jax.experimental import pallas as pl
from jax.experimental.pallas import tpu as pltpu

F32 = jnp.float32
BF16 = jnp.bfloat16
HIGHEST = lax.Precision.HIGHEST

D_MODEL = 1024
D_INNER = 2048
BRANCH_W = 512
GRID_W = 64
RG_HEADS = 8
RG_HD = 64
RG_C = 8.0
SSD_HEADS = 8
SSD_HD = 64
SSD_GROUPS = 2
SSD_STATE = 64
SSD_XBC = BRANCH_W + 2 * SSD_GROUPS * SSD_STATE
SSD_CHUNK = 128
FN_GD = 128
NORM_EPS = 1e-6
COL_WIDTHS = (512, SSD_XBC, 16, 512, 512, 512, 512, 512, 512, 512, 512)

LANES = 128
SUBLANES = 8
BF16_ROWS = 16
VMEM_LIMIT_BYTES = 60000 * 1024

C_RG, C_XBC, C_Z, C_DT, C_FN, C_SC, C_END = 0, 1024, 1792, 2304, 2432, 3456, 5504


def _silu(v):
    return v * jax.nn.sigmoid(v)


def _softplus(v):
    return jnp.maximum(v, 0.0) + jnp.log1p(jnp.exp(-jnp.abs(v)))


def _dot(a, b):
    return jnp.dot(a, b, preferred_element_type=F32)


def _params(*semantics):
    return pltpu.CompilerParams(dimension_semantics=semantics, vmem_limit_bytes=VMEM_LIMIT_BYTES)


def _ada_kernel(c_ref, w_ref, b_ref, o_ref):
    o_ref[...] = jnp.dot(_silu(c_ref[...]), w_ref[...], precision=HIGHEST,
                         preferred_element_type=F32) + b_ref[...]


def _ada(c8, w, b):
    tn = 512
    return pl.pallas_call(
        _ada_kernel,
        grid=(3 * D_MODEL // tn,),
        in_specs=[pl.BlockSpec((SUBLANES, D_MODEL), lambda j: (0, 0)),
                  pl.BlockSpec((D_MODEL, tn), lambda j: (0, j)),
                  pl.BlockSpec((1, tn), lambda j: (0, j))],
        out_specs=pl.BlockSpec((SUBLANES, tn), lambda j: (0, j)),
        out_shape=jax.ShapeDtypeStruct((SUBLANES, 3 * D_MODEL), F32),
        compiler_params=_params("arbitrary"),
        name="ada",
    )(c8, w, b)


def _inproj_kernel(x_ref, nw_ref, sc_ref, sh_ref, w_ref, cw_ref,
                   rg_ref, xbc_ref, z_ref, dt_ref, fn_ref, ysc_ref, *, rowlen):
    x = x_ref[0]
    tm = x.shape[0]
    ms = jnp.mean(x * x, axis=-1, keepdims=True)
    h = (x * lax.rsqrt(ms + NORM_EPS)) * nw_ref[...]
    h = h * (1.0 + sc_ref[0]) + sh_ref[0]
    hb = h.astype(BF16)

    def proj(lo, hi):
        return _dot(hb, w_ref[:, lo:hi])

    rg_ref[0] = proj(C_RG, C_XBC).astype(BF16)
    xbc_ref[0] = proj(C_XBC, C_Z).astype(BF16)
    z_ref[0] = proj(C_Z, C_DT).astype(BF16)
    dt_ref[0] = proj(C_DT, C_FN)
    fn_ref[0] = proj(C_FN, C_SC).astype(BF16)
    bg = proj(C_SC, C_SC + 512)
    v = proj(C_SC + 512, C_SC + 1024) * proj(C_SC + 1024, C_SC + 1536)
    g = proj(C_SC + 1536, C_END)
    pos = lax.broadcasted_iota(jnp.int32, (tm, BRANCH_W), 0) & (rowlen - 1)
    vm1 = jnp.where(pos == 0, 0.0, pltpu.roll(v, 1, 0))
    vp1 = jnp.where(pos == rowlen - 1, 0.0, pltpu.roll(v, tm - 1, 0))
    vc = cw_ref[0:1, :] * vm1 + cw_ref[1:2, :] * v + cw_ref[2:3, :] * vp1
    ysc_ref[0] = (bg * vc * _silu(g)).astype(BF16)


def _inproj(x, nw, scale, shift, w, cw, rowlen, tm):
    bsz, t, _ = x.shape
    assert t % tm == 0 and tm % rowlen == 0 and rowlen & (rowlen - 1) == 0
    tok = lambda width: pl.BlockSpec((1, tm, width), lambda b, i: (b, i, 0))
    vec = pl.BlockSpec((1, 1, D_MODEL), lambda b, i: (b, 0, 0))
    out = lambda width, dt: jax.ShapeDtypeStruct((bsz, t, width), dt)
    return pl.pallas_call(
        functools.partial(_inproj_kernel, rowlen=rowlen),
        grid=(bsz, t // tm),
        in_specs=[tok(D_MODEL),
                  pl.BlockSpec((1, D_MODEL), lambda b, i: (0, 0)),
                  vec, vec,
                  pl.BlockSpec((D_MODEL, C_END), lambda b, i: (0, 0)),
                  pl.BlockSpec((SUBLANES, BRANCH_W), lambda b, i: (0, 0))],
        out_specs=[tok(1024), tok(SSD_XBC), tok(512), tok(LANES), tok(1024), tok(512)],
        out_shape=[out(1024, BF16), out(SSD_XBC, BF16), out(512, BF16), out(LANES, F32),
                   out(1024, BF16), out(512, BF16)],
        compiler_params=_params("arbitrary", "arbitrary"),
        name="inproj",
    )(x, nw, scale, shift, w, cw)


def _outproj_kernel(yrg_ref, ysc_ref, yfn_ref, yssd_ref, w_ref, x_ref, g_ref, fw_ref, o_ref, *, final):
    acc = _dot(yrg_ref[0], w_ref[0:512, :])
    acc += _dot(ysc_ref[0], w_ref[512:1024, :])
    acc += _dot(yfn_ref[0], w_ref[1024:1536, :])
    acc += _dot(yssd_ref[0], w_ref[1536:2048, :])
    xn = x_ref[0] + g_ref[0] * acc
    if final:
        ms = jnp.mean(xn * xn, axis=-1, keepdims=True)
        xn = (xn * lax.rsqrt(ms + NORM_EPS)) * fw_ref[...]
    o_ref[0] = xn


def _outproj(y_rg, y_sc, y_fn, y_ssd, w, x, gate, fw, final, tm):
    bsz, t, _ = x.shape
    ytok = pl.BlockSpec((1, tm, BRANCH_W), lambda b, i: (b, i, 0))
    xtok = pl.BlockSpec((1, tm, D_MODEL), lambda b, i: (b, i, 0))
    return pl.pallas_call(
        functools.partial(_outproj_kernel, final=final),
        grid=(bsz, t // tm),
        in_specs=[ytok, ytok, ytok, ytok,
                  pl.BlockSpec((D_INNER, D_MODEL), lambda b, i: (0, 0)),
                  xtok,
                  pl.BlockSpec((1, 1, D_MODEL), lambda b, i: (b, 0, 0)),
                  pl.BlockSpec((1, D_MODEL), lambda b, i: (0, 0))],
        out_specs=xtok,
        out_shape=jax.ShapeDtypeStruct((bsz, t, D_MODEL), F32),
        compiler_params=_params("arbitrary", "arbitrary"),
        name="outproj",
    )(y_rg, y_sc, y_fn, y_ssd, w, x, gate, fw)


def _conv4_tile(ref, col_lo, col_hi, r0, rows, total_rows, first, last, cw, cb):
    halo = BF16_ROWS
    cur = ref[0, pl.ds(r0, rows), col_lo:col_hi].astype(F32)
    rp = pl.multiple_of(jnp.maximum(r0 - halo, 0), halo)
    rn = pl.multiple_of(jnp.minimum(r0 + rows, total_rows - halo), halo)
    prev = ref[0, pl.ds(rp, halo), col_lo:col_hi].astype(F32) * jnp.where(first, 0.0, 1.0)
    nxt = ref[0, pl.ds(rn, halo), col_lo:col_hi].astype(F32) * jnp.where(last, 0.0, 1.0)
    ext = jnp.concatenate([prev, cur, nxt], axis=0)
    n = rows + 2 * halo
    xm2 = pltpu.roll(ext, 2, 0)[halo:halo + rows]
    xm1 = pltpu.roll(ext, 1, 0)[halo:halo + rows]
    xp1 = pltpu.roll(ext, n - 1, 0)[halo:halo + rows]
    return cw[0:1, :] * xm2 + cw[1:2, :] * xm1 + cw[2:3, :] * cur + cw[3:4, :] * xp1 + cb


def _scan_tile(a, v, h_in, sub, reverse):
    rows = a.shape[0]
    for s in (1, 2, 4):
        if reverse:
            keep = sub < SUBLANES - s
            a_sh = jnp.where(keep, pltpu.roll(a, rows - s, 0), 1.0)
            v_sh = jnp.where(keep, pltpu.roll(v, rows - s, 0), 0.0)
        else:
            keep = sub >= s
            a_sh = jnp.where(keep, pltpu.roll(a, s, 0), 1.0)
            v_sh = jnp.where(keep, pltpu.roll(v, s, 0), 0.0)
        v = v + a * v_sh
        a = a * a_sh
    ngroups = rows // SUBLANES
    out = [None] * ngroups
    h = h_in
    order = range(ngroups - 1, -1, -1) if reverse else range(ngroups)
    for g in order:
        lo = g * SUBLANES
        hg = v[lo:lo + SUBLANES] + a[lo:lo + SUBLANES] * h
        out[g] = hg
        h = hg[0:1] if reverse else hg[SUBLANES - 1:SUBLANES]
    return jnp.concatenate(out, axis=0), h


def _rg_kernel(rg_ref, cw_ref, cb_ref, wg_ref, bg_ref, lam_ref, h0_ref, y_ref, ht_ref, hb_ref, *, t, tt):
    nt = t // tt
    cw = cw_ref[...]
    cb = cb_ref[...]
    sub = lax.broadcasted_iota(jnp.int32, (tt, BRANCH_W), 0) & (SUBLANES - 1)

    def gates(s, d):
        r0 = pl.multiple_of(s * tt, tt)
        u = _conv4_tile(rg_ref, 0, BRANCH_W, r0, tt, t, s == 0, s == nt - 1, cw, cb)
        ub = u.astype(BF16)
        rs, xs = [], []
        for half in (0, 1):
            pre = _dot(ub[:, half * 256:(half + 1) * 256], wg_ref[d, half]) + bg_ref[d, half]
            rs.append(jax.nn.sigmoid(pre[:, :256]))
            xs.append(jax.nn.sigmoid(pre[:, 256:]))
        r = jnp.concatenate(rs, axis=1)
        i = jnp.concatenate(xs, axis=1)
        log_a = r * (-RG_C * _softplus(-lam_ref[d:d + 1, :]))
        a = jnp.exp(log_a)
        return a, jnp.sqrt(1.0 - a * a) * (i * u)

    def rev_body(i, h):
        s = nt - 1 - i
        a, v = gates(s, 1)
        hh, hn = _scan_tile(a, v, h, sub, True)
        hb_ref[pl.ds(pl.multiple_of(s * tt, tt), tt), :] = hh
        return hn

    h_rev = lax.fori_loop(0, nt, rev_body, h0_ref[0, 1:2, :])

    def fwd_body(s, h):
        a, v = gates(s, 0)
        hh, hn = _scan_tile(a, v, h, sub, False)
        r0 = pl.multiple_of(s * tt, tt)
        g = rg_ref[0, pl.ds(r0, tt), BRANCH_W:2 * BRANCH_W].astype(F32)
        y_ref[0, pl.ds(r0, tt), :] = ((hh + hb_ref[pl.ds(r0, tt), :]) * _silu(g)).astype(BF16)
        return hn

    h_fwd = lax.fori_loop(0, nt, fwd_body, h0_ref[0, 0:1, :])
    ht_ref[0] = jnp.concatenate([h_fwd, h_rev, jnp.zeros((SUBLANES - 2, BRANCH_W), F32)], axis=0)


def _rg(rg, lw, h0, tt):
    bsz, t, _ = rg.shape
    assert t % tt == 0
    full = lambda shape: pl.BlockSpec(shape, lambda b: (0,) * len(shape))
    return pl.pallas_call(
        functools.partial(_rg_kernel, t=t, tt=tt),
        grid=(bsz,),
        in_specs=[pl.BlockSpec((1, t, 1024), lambda b: (b, 0, 0)),
                  full((SUBLANES, BRANCH_W)), full((1, BRANCH_W)),
                  full((2, 2, 256, 512)), full((2, 2, 1, 512)), full((SUBLANES, BRANCH_W)),
                  pl.BlockSpec((1, SUBLANES, BRANCH_W), lambda b: (b, 0, 0))],
        out_specs=[pl.BlockSpec((1, t, BRANCH_W), lambda b: (b, 0, 0)),
                   pl.BlockSpec((1, SUBLANES, BRANCH_W), lambda b: (b, 0, 0))],
        out_shape=[jax.ShapeDtypeStruct((bsz, t, BRANCH_W), BF16),
                   jax.ShapeDtypeStruct((bsz, SUBLANES, BRANCH_W), F32)],
        scratch_shapes=[pltpu.VMEM((t, BRANCH_W), F32)],
        compiler_params=_params("arbitrary"),
        name="rglru",
    )(rg, lw["rg_cw"], lw["rg_cb"], lw["rg_wg"], lw["rg_bg"], lw["rg_lam"], h0)


def _ssd_kernel(xbc_ref, z_ref, dt_ref, cw_ref, cb_ref, dtb_ref, alog_ref, dsk_ref, nw_ref, h0_ref,
                y_ref, ht_ref, yb_ref, st_ref, *, t):
    q = SSD_CHUNK
    nc = t // q
    npairs = SSD_HEADS // 2
    cw = cw_ref[...]
    cb = cb_ref[...]
    lane1 = lax.broadcasted_iota(jnp.int32, (1, LANES), 1)
    a_row = jnp.where(lane1 < 2 * SSD_HEADS, -jnp.exp(alog_ref[...]), 0.0)
    ri = lax.broadcasted_iota(jnp.int32, (q, q), 0)
    ci = lax.broadcasted_iota(jnp.int32, (q, q), 1)
    head0 = lax.broadcasted_iota(jnp.int32, (q, LANES), 1) < SSD_HD
    el = lax.broadcasted_iota(jnp.int32, (LANES, BRANCH_W), 0)
    eh = lax.broadcasted_iota(jnp.int32, (LANES, BRANCH_W), 1) // SSD_HD

    def chunk(c, d):
        r0 = pl.multiple_of(c * q, q)
        conv = _conv4_tile(xbc_ref, 0, SSD_XBC, r0, q, t, c == 0, c == nc - 1, cw, cb)
        xc = _silu(conv)
        xs = xc[:, :BRANCH_W]
        bmat = xc[:, BRANCH_W:BRANCH_W + LANES]
        cmat = xc[:, BRANCH_W + LANES:]
        dtv = _softplus(dt_ref[0, pl.ds(r0, q), :] + dtb_ref[...])
        tri = (ri >= ci) if d == 0 else (ri <= ci)
        cs = jnp.dot(tri.astype(F32), dtv * a_row, precision=HIGHEST, preferred_element_type=F32)
        cst = cs.T
        expand = (el == eh + d * SSD_HEADS).astype(F32)
        cs_x = jnp.dot(cs, expand, precision=HIGHEST, preferred_element_type=F32)
        dt_x = jnp.dot(dtv, expand, precision=HIGHEST, preferred_element_type=F32)
        end = q - 1 if d == 0 else 0
        cs_end = cs_x[end:end + 1, :]
        xdt = xs * dt_x
        ecs = jnp.exp(cs_x)
        xdt_dec = (xdt * jnp.exp(cs_end - cs_x)).astype(BF16)
        chunk_decay = jnp.exp(cs_end)
        bm_b = bmat.astype(BF16)
        cm_b = cmat.astype(BF16)
        bt_b = bmat.T.astype(BF16)
        ys = []
        for g in range(SSD_GROUPS):
            cg = cm_b[:, g * SSD_STATE:(g + 1) * SSD_STATE]
            bg = bm_b[:, g * SSD_STATE:(g + 1) * SSD_STATE]
            cbm = lax.dot_general(cg, bg, (((1,), (1,)), ((), ())), preferred_element_type=F32)
            btg = bt_b[g * SSD_STATE:(g + 1) * SSD_STATE, :]
            for pp in range(npairs // SSD_GROUPS):
                pair = g * (npairs // SSD_GROUPS) + pp
                lo = pair * LANES
                xp = xdt[:, lo:lo + LANES]
                yd = None
                for hh in (0, 1):
                    l = d * SSD_HEADS + 2 * pair + hh
                    seg = cs[:, l:l + 1] - cst[l:l + 1, :]
                    lmat = jnp.where(tri, jnp.exp(jnp.where(tri, seg, 0.0)), 0.0)
                    sc = (cbm * lmat).astype(BF16)
                    xm = jnp.where(head0 if hh == 0 else jnp.logical_not(head0), xp, 0.0).astype(BF16)
                    part = _dot(sc, xm)
                    yd = part if yd is None else yd + part
                state = st_ref[pair]
                yo = _dot(cg, state.astype(BF16)) * ecs[:, lo:lo + LANES]
                st_ref[pair] = state * chunk_decay[:, lo:lo + LANES] + _dot(btg, xdt_dec[:, lo:lo + LANES])
                ys.append(yd + yo)
        return jnp.concatenate(ys, axis=1), xs

    def run(d):
        for p in range(npairs):
            st_ref[p] = h0_ref[0, d, p]

        def body(i, carry):
            c = i if d == 0 else nc - 1 - i
            y, xs = chunk(c, d)
            r0 = pl.multiple_of(c * q, q)
            if d == 1:
                yb_ref[pl.ds(r0, q), :] = y
            else:
                yt = y + yb_ref[pl.ds(r0, q), :] + dsk_ref[...] * xs
                yt = yt * _silu(z_ref[0, pl.ds(r0, q), :].astype(F32))
                ms = jnp.mean(yt * yt, axis=-1, keepdims=True)
                y_ref[0, pl.ds(r0, q), :] = ((yt * lax.rsqrt(ms + NORM_EPS)) * nw_ref[...]).astype(BF16)
            return carry

        lax.fori_loop(0, nc, body, 0)
        for p in range(npairs):
            ht_ref[0, d, p] = st_ref[p]

    run(1)
    run(0)


def _ssd(xbc, z, dt, lw, h0):
    bsz, t, _ = xbc.shape
    assert t % SSD_CHUNK == 0
    npairs = SSD_HEADS // 2
    full = lambda shape: pl.BlockSpec(shape, lambda b: (0,) * len(shape))
    seq = lambda width: pl.BlockSpec((1, t, width), lambda b: (b, 0, 0))
    st_spec = pl.BlockSpec((1, 2, npairs, SSD_STATE, LANES), lambda b: (b, 0, 0, 0, 0))
    return pl.pallas_call(
        functools.partial(_ssd_kernel, t=t),
        grid=(bsz,),
        in_specs=[seq(SSD_XBC), seq(BRANCH_W), seq(LANES),
                  full((SUBLANES, SSD_XBC)), full((1, SSD_XBC)), full((1, LANES)), full((1, LANES)),
                  full((1, BRANCH_W)), full((1, BRANCH_W)), st_spec],
        out_specs=[seq(BRANCH_W), st_spec],
        out_shape=[jax.ShapeDtypeStruct((bsz, t, BRANCH_W), BF16),
                   jax.ShapeDtypeStruct((bsz, 2, npairs, SSD_STATE, LANES), F32)],
        scratch_shapes=[pltpu.VMEM((t, BRANCH_W), F32), pltpu.VMEM((npairs, SSD_STATE, LANES), F32)],
        compiler_params=_params("arbitrary"),
        name="ssd",
    )(xbc, z, dt, lw["ssd_cw"], lw["ssd_cb"], lw["ssd_dtb"], lw["ssd_alog"], lw["ssd_dskip"],
      lw["ssd_nw"], h0)


@functools.lru_cache(maxsize=None)
def _fourier_tables(n, blk):
    half = n // 2
    nh = half // blk
    assert nh * blk == half and nh <= SUBLANES
    p = np.arange(FN_GD)
    ang_c = 2.0 * np.pi * ((p[:, None] * p[None, :]) % FN_GD) / FN_GD
    eye = np.eye(BRANCH_W // FN_GD)
    cc = np.kron(eye, np.cos(ang_c))
    sc = np.kron(eye, np.sin(ang_c))
    k = np.arange(half)
    ang_t = 2.0 * np.pi * ((k[:, None] * k[None, :]) % n) / n
    ct = np.cos(ang_t)
    stn = -np.sin(ang_t)
    jp = np.zeros((blk, blk))
    r = np.arange(1, blk)
    jp[r, blk - r] = 1.0
    tt = np.arange(n)
    ks = blk * (np.arange(nh) + 1)
    ang_k = 2.0 * np.pi * ((ks[:, None] * tt[None, :]) % n) / n
    tc = np.zeros((2 * SUBLANES, n))
    ts = np.zeros((2 * SUBLANES, n))
    tc[:nh] = np.cos(ang_k)
    ts[:nh] = np.sin(ang_k)
    tc[SUBLANES, half] = 1.0
    as_bf16 = lambda a: jnp.asarray(a, dtype=F32).astype(BF16)
    return dict(cc=as_bf16(cc), sc=as_bf16(sc), ct=as_bf16(ct), stn=as_bf16(stn), jp=as_bf16(jp),
                tc=as_bf16(tc), ts=as_bf16(ts))


def _fn_fold_kernel(xj_ref, xm_ref, xr_ref, jp_ref, cc_ref, sc_ref, eo_ref):
    j = pl.program_id(1)
    xj = xj_ref[0]
    blk = xj.shape[0]
    mir = _dot(jp_ref[...], xm_ref[0])
    row0 = xr_ref[0, 0:1, :].astype(F32) * jnp.where(j == 0, 0.0, 1.0)
    rows = lax.broadcasted_iota(jnp.int32, (blk, BRANCH_W), 0)
    mir = jnp.where(rows == 0, row0, mir).astype(BF16)
    e = _dot(xj, cc_ref[...]) + _dot(mir, cc_ref[...])
    o = _dot(xj, sc_ref[...]) - _dot(mir, sc_ref[...])
    eo_ref[0] = jnp.concatenate([e, o], axis=1).astype(BF16)


def _fn_aux_kernel(x_ref, tc_ref, ts_ref, cc_ref, sc_ref, o_ref):
    x = x_ref[0]
    xc = _dot(tc_ref[...], x).astype(BF16)
    xs = _dot(ts_ref[...], x).astype(BF16)
    o_ref[0] = _dot(xc, cc_ref[...]) + _dot(xs, sc_ref[...])


def _fn_main_kernel(ct_ref, stn_ref, eo_ref, aux_ref, g_ref, jp_ref, y_ref, r_ref, *, scale):
    m = pl.program_id(0)
    half = pl.program_id(2)
    blk = r_ref.shape[0]
    gate = _silu(g_ref[0].astype(F32)) * scale
    rows = lax.broadcasted_iota(jnp.int32, (blk, BRANCH_W), 0)

    @pl.when(half == 0)
    def _():
        p = _dot(ct_ref[...], eo_ref[0, :, 0:BRANCH_W])
        qn = _dot(stn_ref[...], eo_ref[0, :, BRANCH_W:2 * BRANCH_W])
        sgn = (1 - 2 * ((m * blk + rows) & 1)).astype(F32)
        p = p + sgn * aux_ref[0, SUBLANES:SUBLANES + 1, :]
        r_ref[...] = p - qn
        y_ref[0] = ((p + qn) * gate).astype(BF16)

    @pl.when(half == 1)
    def _():
        flipped = _dot(jp_ref[...], r_ref[...].astype(BF16))
        sel = lax.broadcasted_iota(jnp.int32, (2 * SUBLANES, BRANCH_W), 0) == m
        row0 = jnp.sum(jnp.where(sel, aux_ref[0], 0.0), axis=0, keepdims=True)
        y_ref[0] = (jnp.where(rows == 0, row0, flipped) * gate).astype(BF16)


def _fourier(fn, blk):
    bsz, n, _ = fn.shape
    tb = _fourier_tables(n, blk)
    half = n // 2
    nh = half // blk
    nblk = 2 * nh
    rows16 = n // BF16_ROWS
    full2 = lambda shape: pl.BlockSpec(shape, lambda b, j: (0,) * len(shape))
    eo = pl.pallas_call(
        _fn_fold_kernel,
        grid=(bsz, nh),
        in_specs=[pl.BlockSpec((1, blk, BRANCH_W), lambda b, j: (b, j, 0)),
                  pl.BlockSpec((1, blk, BRANCH_W), lambda b, j: (b, nblk - 1 - j, 0)),
                  pl.BlockSpec((1, BF16_ROWS, BRANCH_W),
                               lambda b, j: (b, jnp.minimum((blk // BF16_ROWS) * (nblk - j), rows16 - 1), 0)),
                  full2((blk, blk)), full2((BRANCH_W, BRANCH_W)), full2((BRANCH_W, BRANCH_W))],
        out_specs=pl.BlockSpec((1, blk, 2 * BRANCH_W), lambda b, j: (b, j, 0)),
        out_shape=jax.ShapeDtypeStruct((bsz, half, 2 * BRANCH_W), BF16),
        compiler_params=_params("arbitrary", "arbitrary"),
        name="fn_fold",
    )(fn, fn, fn, tb["jp"], tb["cc"], tb["sc"])
    full1 = lambda shape: pl.BlockSpec(shape, lambda b: (0,) * len(shape))
    aux = pl.pallas_call(
        _fn_aux_kernel,
        grid=(bsz,),
        in_specs=[pl.BlockSpec((1, n, BRANCH_W), lambda b: (b, 0, 0)),
                  full1((2 * SUBLANES, n)), full1((2 * SUBLANES, n)),
                  full1((BRANCH_W, BRANCH_W)), full1((BRANCH_W, BRANCH_W))],
        out_specs=pl.BlockSpec((1, 2 * SUBLANES, BRANCH_W), lambda b: (b, 0, 0)),
        out_shape=jax.ShapeDtypeStruct((bsz, 2 * SUBLANES, BRANCH_W), F32),
        compiler_params=_params("arbitrary"),
        name="fn_aux",
    )(fn, tb["tc"], tb["ts"], tb["cc"], tb["sc"])
    out_blk = lambda m, b, h: (b, m + h * (nblk - 1 - 2 * m), 0)
    return pl.pallas_call(
        functools.partial(_fn_main_kernel, scale=float(1.0 / np.sqrt(n * FN_GD))),
        grid=(nh, bsz, 2),
        in_specs=[pl.BlockSpec((blk, half), lambda m, b, h: (m, 0)),
                  pl.BlockSpec((blk, half), lambda m, b, h: (m, 0)),
                  pl.BlockSpec((1, half, 2 * BRANCH_W), lambda m, b, h: (b, 0, 0)),
                  pl.BlockSpec((1, 2 * SUBLANES, BRANCH_W), lambda m, b, h: (b, 0, 0)),
                  pl.BlockSpec((1, blk, BRANCH_W), lambda m, b, h: out_blk(m, b, h)[:2] + (1,)),
                  pl.BlockSpec((blk, blk), lambda m, b, h: (0, 0))],
        out_specs=pl.BlockSpec((1, blk, BRANCH_W), out_blk),
        out_shape=jax.ShapeDtypeStruct((bsz, n, BRANCH_W), BF16),
        scratch_shapes=[pltpu.VMEM((blk, BRANCH_W), F32)],
        compiler_params=_params("arbitrary", "arbitrary", "arbitrary"),
        name="fn_main",
    )(tb["ct"], tb["stn"], eo, aux, fn, tb["jp"])


def _blockdiag(w4):
    n = w4.shape[0]
    return jnp.einsum("hij,hg->higj", w4, jnp.eye(n, dtype=w4.dtype)).reshape(n * RG_HD, n * RG_HD)


def _pad_rows(a, rows):
    return jnp.pad(a, ((0, rows - a.shape[0]), (0, 0)))


def _layer_weights(l, w_in, w_out, rg_conv_w, rg_conv_b, rg_gate_a_w, rg_gate_a_b, rg_gate_x_w, rg_gate_x_b,
                   rg_lambda, sc_conv_w, ssd_conv_w, ssd_conv_b, ssd_dt_bias, ssd_a_log, ssd_d, ssd_norm_w):
    idx = [int(i) for i in np.cumsum(COL_WIDTHS)[:-1]]
    rg_x, xbc, dt, rg_g, z, sc_b, sc_c, sc_x, sc_g, fn_x, fn_g = jnp.split(w_in[l], idx, axis=1)
    dt = jnp.pad(dt, ((0, 0), (0, LANES - dt.shape[1])))
    w = jnp.concatenate([rg_x, rg_g, xbc, z, dt, fn_x, fn_g, sc_b, sc_c, sc_x, sc_g], axis=1).astype(BF16)
    wg, bg = [], []
    for d in range(2):
        wd, bd = [], []
        for half in range(2):
            hs = slice(4 * half, 4 * half + 4)
            cs = slice(256 * half, 256 * half + 256)
            wd.append(jnp.concatenate([_blockdiag(rg_gate_a_w[l, d, hs]), _blockdiag(rg_gate_x_w[l, d, hs])], axis=1))
            bd.append(jnp.concatenate([rg_gate_a_b[l, d, cs], rg_gate_x_b[l, d, cs]])[None, :])
        wg.append(jnp.stack(wd))
        bg.append(jnp.stack(bd))
    pad_lanes = lambda v: jnp.pad(v.reshape(1, -1), ((0, 0), (0, LANES - v.size)))
    return dict(
        w_in=w, w_out=w_out[l].astype(BF16),
        sc_cw=_pad_rows(sc_conv_w[l], SUBLANES),
        rg_cw=_pad_rows(rg_conv_w[l], SUBLANES), rg_cb=rg_conv_b[l][None, :],
        rg_wg=jnp.stack(wg).astype(BF16), rg_bg=jnp.stack(bg), rg_lam=_pad_rows(rg_lambda[l], SUBLANES),
        ssd_cw=_pad_rows(ssd_conv_w[l], SUBLANES), ssd_cb=ssd_conv_b[l][None, :],
        ssd_dtb=pad_lanes(ssd_dt_bias[l]), ssd_alog=pad_lanes(ssd_a_log[l]),
        ssd_dskip=jnp.repeat(ssd_d[l], SSD_HD)[None, :], ssd_nw=ssd_norm_w[l][None, :],
    )


def _mix(x, nw, shift, scale, gate, lw, rowlen, tm, tt, blk, rg_h0, ssd_h0, with_output, final, fw):
    rg, xbc, z, dt, fn, y_sc = _inproj(x, nw, scale, shift, lw["w_in"], lw["sc_cw"], rowlen, tm)
    y_rg, rg_st = _rg(rg, lw, rg_h0, tt)
    y_ssd, ssd_st = _ssd(xbc, z, dt, lw, ssd_h0)
    if not with_output:
        return None, rg_st, ssd_st
    y_fn = _fourier(fn, blk)
    return _outproj(y_rg, y_sc, y_fn, y_ssd, lw["w_out"], x, gate, fw, final, tm), rg_st, ssd_st


def kernel(x, c, ctx, c_ctx, ada_w, ada_b, norm_w, w_in, w_out, rg_conv_w, rg_conv_b, rg_gate_a_w, rg_gate_a_b,
           rg_gate_x_w, rg_gate_x_b, rg_lambda, sc_conv_w, ssd_conv_w, ssd_conv_b, ssd_dt_bias, ssd_a_log, ssd_d,
           ssd_norm_w, final_norm_w):
    bsz, seq, _ = x.shape
    ctx_len = ctx.shape[1]
    depth = w_in.shape[0]
    assert bsz + 1 <= SUBLANES
    c8 = jnp.concatenate([c, c_ctx[None, :], jnp.zeros((SUBLANES - bsz - 1, D_MODEL), F32)], axis=0)
    fw = final_norm_w[None, :]
    zeros_rg = jnp.zeros((bsz, SUBLANES, BRANCH_W), F32)
    zeros_ssd = jnp.zeros((bsz, 2, SSD_HEADS // 2, SSD_STATE, LANES), F32)
    tm_lat = min(512, seq)
    tt_lat = min(256, seq)
    tt_ctx = min(256, ctx_len)
    blk_lat = min(512, seq // 2)
    blk_ctx = min(512, ctx_len // 2)
    for l in range(depth):
        lw = _layer_weights(l, w_in, w_out, rg_conv_w, rg_conv_b, rg_gate_a_w, rg_gate_a_b, rg_gate_x_w,
                            rg_gate_x_b, rg_lambda, sc_conv_w, ssd_conv_w, ssd_conv_b, ssd_dt_bias, ssd_a_log,
                            ssd_d, ssd_norm_w)
        mod = _ada(c8, ada_w[l], ada_b[l][None, :])
        shift, scale, gate = (mod[:, i * D_MODEL:(i + 1) * D_MODEL] for i in range(3))
        lat = lambda m: m[:bsz, None, :]
        con = lambda m: jnp.broadcast_to(m[bsz:bsz + 1, None, :], (bsz, 1, D_MODEL))
        nw = norm_w[l][None, :]
        last = l == depth - 1
        new_ctx, rg_st, ssd_st = _mix(ctx, nw, con(shift), con(scale), con(gate), lw, ctx_len, ctx_len, tt_ctx,
                                      blk_ctx, zeros_rg, zeros_ssd, not last, False, fw)
        x, _, _ = _mix(x, nw, lat(shift), lat(scale), lat(gate), lw, GRID_W, tm_lat, tt_lat, blk_lat,
                       rg_st, ssd_st, True, last, fw)
        if not last:
            ctx = new_ctx
    return x
```

```python
import functools

import numpy as np
import jax
import jax.numpy as jnp
from jax import lax
from jax.experimental import pallas as pl
from jax.experimental.pallas import tpu as pltpu

F32 = jnp.float32
BF16 = jnp.bfloat16
HIGHEST = lax.Precision.HIGHEST

D_MODEL = 1024
D_INNER = 2048
BRANCH_W = 512
GRID_W = 64
RG_HEADS = 8
RG_HD = 64
RG_C = 8.0
SSD_HEADS = 8
SSD_HD = 64
SSD_GROUPS = 2
SSD_STATE = 64
SSD_XBC = BRANCH_W + 2 * SSD_GROUPS * SSD_STATE
SSD_CHUNK = 128
FN_GD = 128
NORM_EPS = 1e-6
COL_WIDTHS = (512, SSD_XBC, 16, 512, 512, 512, 512, 512, 512, 512, 512)

LANES = 128
SUBLANES = 8
BF16_ROWS = 16
VMEM_LIMIT_BYTES = 60000 * 1024

C_RG, C_XBC, C_Z, C_DT, C_FN, C_SC, C_END = 0, 1024, 1792, 2304, 2432, 3456, 5504


def _sigmoid(v):
    return 0.5 + 0.5 * jnp.tanh(0.5 * v)


def _silu(v):
    h = 0.5 * v
    return h + h * jnp.tanh(h)


def _softplus(v):
    return jnp.maximum(v, 0.0) + jnp.log1p(jnp.exp(-jnp.abs(v)))


def _dot(a, b):
    return jnp.dot(a, b, preferred_element_type=F32)


def _split3(v):
    hi = v.astype(BF16).astype(F32)
    rest = v - hi
    mid = rest.astype(BF16).astype(F32)
    return hi, mid, rest - mid


SPLIT_LANES = 16


def _pack3(v):
    hi, mid, lo = _split3(v)
    period = lax.broadcasted_iota(jnp.int32, v.shape, 1) // SPLIT_LANES
    return jnp.where(period == 0, hi, jnp.where(period == 1, mid, jnp.where(period == 2, lo, 0.0))).astype(BF16)


def _params(*semantics):
    return pltpu.CompilerParams(dimension_semantics=semantics, vmem_limit_bytes=VMEM_LIMIT_BYTES)


def _ada_kernel(c_ref, w_ref, b_ref, o_ref):
    o_ref[...] = jnp.dot(_silu(c_ref[...]), w_ref[...], precision=HIGHEST,
                         preferred_element_type=F32) + b_ref[...]


def _ada(c8, w, b):
    tn = 512
    return pl.pallas_call(
        _ada_kernel,
        grid=(3 * D_MODEL // tn,),
        in_specs=[pl.BlockSpec((SUBLANES, D_MODEL), lambda j: (0, 0)),
                  pl.BlockSpec((D_MODEL, tn), lambda j: (0, j)),
                  pl.BlockSpec((1, tn), lambda j: (0, j))],
        out_specs=pl.BlockSpec((SUBLANES, tn), lambda j: (0, j)),
        out_shape=jax.ShapeDtypeStruct((SUBLANES, 3 * D_MODEL), F32),
        compiler_params=_params("arbitrary"),
        name="ada",
    )(c8, w, b)


def _inproj_kernel(x_ref, xp_ref, xn_ref, nw_ref, sc_ref, sh_ref, w_ref, cw_ref, rcw_ref, rcb_ref, scw_ref, scb_ref,
                   rg_ref, xbc_ref, z_ref, dt_ref, fn_ref, ysc_ref, *, rowlen):
    i = pl.program_id(1)
    tm = x_ref.shape[1]
    halo = xp_ref.shape[1]
    x = jnp.concatenate([xp_ref[0], x_ref[0], xn_ref[0]], axis=0)
    ms = jnp.mean(x * x, axis=-1, keepdims=True)
    h = (x * lax.rsqrt(ms + NORM_EPS)) * nw_ref[...]
    h = h * (1.0 + sc_ref[0]) + sh_ref[0]
    hx = h.astype(BF16)
    hb = hx[halo:halo + tm]
    keep_prev = jnp.where(i == 0, 0.0, 1.0)
    keep_next = jnp.where(i == pl.num_programs(1) - 1, 0.0, 1.0)

    def proj(lo, hi):
        return _dot(hb, w_ref[:, lo:hi])

    def conv4(lo, hi, cw, cb):
        p = _dot(hx, w_ref[:, lo:hi])
        p = jnp.concatenate([p[:halo] * keep_prev, p[halo:halo + tm], p[halo + tm:] * keep_next], axis=0)
        n = tm + 2 * halo
        mid = slice(halo, halo + tm)
        return (cw[0:1, :] * pltpu.roll(p, 2, 0)[mid] + cw[1:2, :] * pltpu.roll(p, 1, 0)[mid]
                + cw[2:3, :] * p[mid] + cw[3:4, :] * pltpu.roll(p, n - 1, 0)[mid] + cb)

    rg_ref[0, :, 0:BRANCH_W] = conv4(C_RG, C_RG + BRANCH_W, rcw_ref[...], rcb_ref[...]).astype(BF16)
    rg_ref[0, :, BRANCH_W:] = proj(C_RG + BRANCH_W, C_XBC).astype(BF16)
    xbc_ref[0] = _silu(conv4(C_XBC, C_Z, scw_ref[...], scb_ref[...])).astype(BF16)
    z_ref[0] = proj(C_Z, C_DT).astype(BF16)
    dt_ref[0] = proj(C_DT, C_FN)
    fn_ref[0] = proj(C_FN, C_SC).astype(BF16)
    bg = proj(C_SC, C_SC + 512)
    v = proj(C_SC + 512, C_SC + 1024) * proj(C_SC + 1024, C_SC + 1536)
    g = proj(C_SC + 1536, C_END)
    pos = lax.broadcasted_iota(jnp.int32, (tm, BRANCH_W), 0) & (rowlen - 1)
    vm1 = jnp.where(pos == 0, 0.0, pltpu.roll(v, 1, 0))
    vp1 = jnp.where(pos == rowlen - 1, 0.0, pltpu.roll(v, tm - 1, 0))
    vc = cw_ref[0:1, :] * vm1 + cw_ref[1:2, :] * v + cw_ref[2:3, :] * vp1
    ysc_ref[0] = (bg * vc * _silu(g)).astype(BF16)


def _inproj(x, nw, scale, shift, lw, rowlen, tm):
    bsz, t, _ = x.shape
    halo = BF16_ROWS
    assert t % tm == 0 and tm % rowlen == 0 and rowlen & (rowlen - 1) == 0 and tm % halo == 0
    per_tile = tm // halo
    last_halo = t // halo - 1
    tok = lambda width: pl.BlockSpec((1, tm, width), lambda b, i: (b, i, 0))
    vec = pl.BlockSpec((1, 1, D_MODEL), lambda b, i: (b, 0, 0))
    full = lambda shape: pl.BlockSpec(shape, lambda b, i: (0,) * len(shape))
    out = lambda width, dt: jax.ShapeDtypeStruct((bsz, t, width), dt)
    return pl.pallas_call(
        functools.partial(_inproj_kernel, rowlen=rowlen),
        grid=(bsz, t // tm),
        in_specs=[tok(D_MODEL),
                  pl.BlockSpec((1, halo, D_MODEL), lambda b, i: (b, jnp.maximum(i * per_tile - 1, 0), 0)),
                  pl.BlockSpec((1, halo, D_MODEL), lambda b, i: (b, jnp.minimum((i + 1) * per_tile, last_halo), 0)),
                  full((1, D_MODEL)), vec, vec, full((D_MODEL, C_END)),
                  full((SUBLANES, BRANCH_W)), full((SUBLANES, BRANCH_W)), full((1, BRANCH_W)),
                  full((SUBLANES, SSD_XBC)), full((1, SSD_XBC))],
        out_specs=[tok(1024), tok(SSD_XBC), tok(512), tok(LANES), tok(1024), tok(512)],
        out_shape=[out(1024, BF16), out(SSD_XBC, BF16), out(512, BF16), out(LANES, F32),
                   out(1024, BF16), out(512, BF16)],
        compiler_params=_params("arbitrary", "arbitrary"),
        name="inproj",
    )(x, x, x, nw, scale, shift, lw["w_in"], lw["sc_cw"], lw["rg_cw"], lw["rg_cb"], lw["ssd_cw"], lw["ssd_cb"])


def _outproj_kernel(yrg_ref, ysc_ref, yfn_ref, yssd_ref, w_ref, x_ref, g_ref, fw_ref, o_ref, *, final):
    acc = _dot(yrg_ref[0], w_ref[0:512, :])
    acc += _dot(ysc_ref[0], w_ref[512:1024, :])
    acc += _dot(yfn_ref[0], w_ref[1024:1536, :])
    acc += _dot(yssd_ref[0], w_ref[1536:2048, :])
    xn = x_ref[0] + g_ref[0] * acc
    if final:
        ms = jnp.mean(xn * xn, axis=-1, keepdims=True)
        xn = (xn * lax.rsqrt(ms + NORM_EPS)) * fw_ref[...]
    o_ref[0] = xn


def _outproj(y_rg, y_sc, y_fn, y_ssd, w, x, gate, fw, final, tm):
    bsz, t, _ = x.shape
    ytok = pl.BlockSpec((1, tm, BRANCH_W), lambda b, i: (b, i, 0))
    xtok = pl.BlockSpec((1, tm, D_MODEL), lambda b, i: (b, i, 0))
    return pl.pallas_call(
        functools.partial(_outproj_kernel, final=final),
        grid=(bsz, t // tm),
        in_specs=[ytok, ytok, ytok, ytok,
                  pl.BlockSpec((D_INNER, D_MODEL), lambda b, i: (0, 0)),
                  xtok,
                  pl.BlockSpec((1, 1, D_MODEL), lambda b, i: (b, 0, 0)),
                  pl.BlockSpec((1, D_MODEL), lambda b, i: (0, 0))],
        out_specs=xtok,
        out_shape=jax.ShapeDtypeStruct((bsz, t, D_MODEL), F32),
        compiler_params=_params("arbitrary", "arbitrary"),
        name="outproj",
    )(y_rg, y_sc, y_fn, y_ssd, w, x, gate, fw)


def _scan_tile(a, v, h_in, sub, reverse):
    rows = a.shape[0]
    for s in (1, 2, 4):
        if reverse:
            keep = sub < SUBLANES - s
            a_sh = jnp.where(keep, pltpu.roll(a, rows - s, 0), 1.0)
            v_sh = jnp.where(keep, pltpu.roll(v, rows - s, 0), 0.0)
        else:
            keep = sub >= s
            a_sh = jnp.where(keep, pltpu.roll(a, s, 0), 1.0)
            v_sh = jnp.where(keep, pltpu.roll(v, s, 0), 0.0)
        v = v + a * v_sh
        a = a * a_sh
    ngroups = rows // SUBLANES
    out = [None] * ngroups
    h = h_in
    order = range(ngroups - 1, -1, -1) if reverse else range(ngroups)
    for g in order:
        lo = g * SUBLANES
        hg = v[lo:lo + SUBLANES] + a[lo:lo + SUBLANES] * h
        out[g] = hg
        h = hg[0:1] if reverse else hg[SUBLANES - 1:SUBLANES]
    return jnp.concatenate(out, axis=0), h


def _rg_kernel(rg_ref, wg_ref, bg_ref, lam_ref, h0_ref, y_ref, ht_ref, hb_ref, *, t, tt):
    nt = t // tt
    sub = lax.broadcasted_iota(jnp.int32, (tt, BRANCH_W), 0) & (SUBLANES - 1)

    def gates(s, d):
        r0 = pl.multiple_of(s * tt, tt)
        ub = rg_ref[0, pl.ds(r0, tt), 0:BRANCH_W]
        u = ub.astype(F32)
        rs, xs = [], []
        for half in (0, 1):
            pre = _dot(ub[:, half * 256:(half + 1) * 256], wg_ref[d, half]) + bg_ref[d, half]
            rs.append(_sigmoid(pre[:, :256]))
            xs.append(_sigmoid(pre[:, 256:]))
        r = jnp.concatenate(rs, axis=1)
        i = jnp.concatenate(xs, axis=1)
        log_a = r * (-RG_C * _softplus(-lam_ref[d:d + 1, :]))
        a = jnp.exp(log_a)
        return a, jnp.sqrt(1.0 - a * a) * (i * u)

    def rev_body(i, h):
        s = nt - 1 - i
        a, v = gates(s, 1)
        hh, hn = _scan_tile(a, v, h, sub, True)
        hb_ref[pl.ds(pl.multiple_of(s * tt, tt), tt), :] = hh
        return hn

    h_rev = lax.fori_loop(0, nt, rev_body, h0_ref[0, 1:2, :])

    def fwd_body(s, h):
        a, v = gates(s, 0)
        hh, hn = _scan_tile(a, v, h, sub, False)
        r0 = pl.multiple_of(s * tt, tt)
        g = rg_ref[0, pl.ds(r0, tt), BRANCH_W:2 * BRANCH_W].astype(F32)
        y_ref[0, pl.ds(r0, tt), :] = ((hh + hb_ref[pl.ds(r0, tt), :]) * _silu(g)).astype(BF16)
        return hn

    h_fwd = lax.fori_loop(0, nt, fwd_body, h0_ref[0, 0:1, :])
    ht_ref[0] = jnp.concatenate([h_fwd, h_rev, jnp.zeros((SUBLANES - 2, BRANCH_W), F32)], axis=0)


def _rg(rg, lw, h0, tt):
    bsz, t, _ = rg.shape
    assert t % tt == 0
    full = lambda shape: pl.BlockSpec(shape, lambda b: (0,) * len(shape))
    return pl.pallas_call(
        functools.partial(_rg_kernel, t=t, tt=tt),
        grid=(bsz,),
        in_specs=[pl.BlockSpec((1, t, 1024), lambda b: (b, 0, 0)),
                  full((2, 2, 256, 512)), full((2, 2, 1, 512)), full((SUBLANES, BRANCH_W)),
                  pl.BlockSpec((1, SUBLANES, BRANCH_W), lambda b: (b, 0, 0))],
        out_specs=[pl.BlockSpec((1, t, BRANCH_W), lambda b: (b, 0, 0)),
                   pl.BlockSpec((1, SUBLANES, BRANCH_W), lambda b: (b, 0, 0))],
        out_shape=[jax.ShapeDtypeStruct((bsz, t, BRANCH_W), BF16),
                   jax.ShapeDtypeStruct((bsz, SUBLANES, BRANCH_W), F32)],
        scratch_shapes=[pltpu.VMEM((t, BRANCH_W), F32)],
        compiler_params=_params("arbitrary"),
        name="rglru",
    )(rg, lw["rg_wg"], lw["rg_bg"], lw["rg_lam"], h0)


def _ssd_kernel(xbc_ref, z_ref, dt_ref, dtb_ref, alog_ref, dsk_ref, nw_ref, ecol_ref, ehead_ref,
                hmask_ref, h0_ref, y_ref, ht_ref, yb_ref, cs_ref, loc_ref, st_ref, *, t):
    q = SSD_CHUNK
    nc = t // q
    npairs = SSD_HEADS // 2
    pairs_per_group = npairs // SSD_GROUPS
    lane1 = lax.broadcasted_iota(jnp.int32, (1, LANES), 1)
    a_row = -jnp.exp(alog_ref[...])
    fwd_lane = (lax.broadcasted_iota(jnp.int32, (q, LANES), 1) & (SPLIT_LANES - 1)) < SSD_HEADS
    ri = lax.broadcasted_iota(jnp.int32, (q, q), 0)
    ci = lax.broadcasted_iota(jnp.int32, (q, q), 1)
    lower = ri >= ci
    lower_b = jnp.where(lower, 1.0, 0.0).astype(BF16)
    head0_s = lax.broadcasted_iota(jnp.int32, (SSD_STATE, LANES), 1) < SSD_HD

    def pair_vec(row, l0, l1):
        return jnp.where(lane1 < SSD_HD, row[:, l0:l0 + 1], row[:, l1:l1 + 1])

    def pass1(i, carry):
        c = nc - 1 - i
        r0 = pl.multiple_of(c * q, q)
        xs_b = xbc_ref[0, pl.ds(r0, q), 0:BRANCH_W]
        bm_b = xbc_ref[0, pl.ds(r0, q), BRANCH_W:BRANCH_W + LANES]
        cm_b = xbc_ref[0, pl.ds(r0, q), BRANCH_W + LANES:SSD_XBC]
        dtv = _softplus(dt_ref[0, pl.ds(r0, q), :] + dtb_ref[...])
        da = dtv * a_row
        pre = sum(_dot(lower_b, part.astype(BF16)) for part in _split3(da))
        tot = pre[q - 1:q, :]
        cs = jnp.where(fwd_lane, pre, tot - pre + da)
        wgt = dtv * jnp.exp(tot - cs)
        etot = jnp.exp(tot)
        cs_p = _pack3(cs)
        cs_col = _dot(cs_p, ecol_ref[...])
        ecs_b = jnp.exp(_dot(cs_p, ehead_ref[:, BRANCH_W:]))
        cst = cs.T
        dtt = dtv.T
        wgtt = wgt.T
        bt = bm_b.astype(F32).T
        xs_h0 = xs_b * hmask_ref[0:1, :]
        xs_h1 = xs_b * hmask_ref[1:2, :]
        ys = []
        for g in range(SSD_GROUPS):
            gs = slice(g * SSD_STATE, (g + 1) * SSD_STATE)
            cbm = lax.dot_general(cm_b[:, gs], bm_b[:, gs], (((1,), (1,)), ((), ())), preferred_element_type=F32)
            btg = bt[gs, :]
            for pp in range(pairs_per_group):
                pair = g * pairs_per_group + pp
                ps = slice(pair * LANES, (pair + 1) * LANES)
                state = st_ref[pair]
                lf0 = 2 * pair
                lb0 = SSD_HEADS + lf0
                ms = []
                for lf in (lf0, lf0 + 1):
                    lb = SSD_HEADS + lf
                    dtf = dtt[lf:lf + 1, :]
                    dtb = dtt[lb:lb + 1, :]
                    arg = jnp.where(lower, cs_col[:, lf * LANES:(lf + 1) * LANES] - cst[lf:lf + 1, :],
                                    cs_col[:, lb * LANES:(lb + 1) * LANES] - cst[lb:lb + 1, :])
                    w = jnp.where(ri > ci, dtf, jnp.where(ri < ci, dtb, dtf + dtb))
                    ms.append((cbm * jnp.exp(arg) * w).astype(BF16))
                yd = _dot(jnp.concatenate(ms, axis=1), jnp.concatenate([xs_h0[:, ps], xs_h1[:, ps]], axis=0))
                yo = _dot(cm_b[:, gs], state.astype(BF16)) * ecs_b[:, ps]
                ys.append(yd + yo)
                stack = jnp.concatenate([(btg * wgtt[l:l + 1, :]).astype(BF16) for l in (lf0, lf0 + 1, lb0, lb0 + 1)],
                                        axis=0)
                res = _dot(stack, xs_b[:, ps])
                n = SSD_STATE
                loc_ref[c, pair] = jnp.where(head0_s, res[0:n], res[n:2 * n])
                st_ref[pair] = state * pair_vec(etot, lb0, lb0 + 1) + jnp.where(head0_s, res[2 * n:3 * n], res[3 * n:])
        yb_ref[pl.ds(r0, q), :] = jnp.concatenate(ys, axis=1) + dsk_ref[...] * xs_b.astype(F32)
        cs_ref[pl.ds(r0, q), :] = cs
        return carry

    def pass2(c, carry):
        r0 = pl.multiple_of(c * q, q)
        cs = cs_ref[pl.ds(r0, q), :]
        cm_b = xbc_ref[0, pl.ds(r0, q), BRANCH_W + LANES:SSD_XBC]
        ecs_f = jnp.exp(_dot(_pack3(cs), ehead_ref[:, :BRANCH_W]))
        etot = jnp.exp(cs[q - 1:q, :])
        ys = []
        for pair in range(npairs):
            g = pair // pairs_per_group
            state = st_ref[pair]
            lf0 = 2 * pair
            ys.append(_dot(cm_b[:, g * SSD_STATE:(g + 1) * SSD_STATE], state.astype(BF16))
                      * ecs_f[:, pair * LANES:(pair + 1) * LANES])
            st_ref[pair] = state * pair_vec(etot, lf0, lf0 + 1) + loc_ref[c, pair]
        yt = yb_ref[pl.ds(r0, q), :] + jnp.concatenate(ys, axis=1)
        yt = yt * _silu(z_ref[0, pl.ds(r0, q), :].astype(F32))
        ms = jnp.mean(yt * yt, axis=-1, keepdims=True)
        y_ref[0, pl.ds(r0, q), :] = ((yt * lax.rsqrt(ms + NORM_EPS)) * nw_ref[...]).astype(BF16)
        return carry

    for p in range(npairs):
        st_ref[p] = h0_ref[0, 1, p]
    lax.fori_loop(0, nc, pass1, 0)
    for p in range(npairs):
        ht_ref[0, 1, p] = st_ref[p]
        st_ref[p] = h0_ref[0, 0, p]
    lax.fori_loop(0, nc, pass2, 0)
    for p in range(npairs):
        ht_ref[0, 0, p] = st_ref[p]


@functools.lru_cache(maxsize=None)
def _ssd_tables():
    lanes = np.arange(LANES)[:, None]
    src = np.where(lanes < 3 * SPLIT_LANES, lanes % SPLIT_LANES, -1)
    ecol = (src == np.arange(2 * SSD_HEADS * LANES)[None, :] // LANES)
    ehead = (src == np.arange(2 * BRANCH_W)[None, :] // SSD_HD)
    even_head = (np.arange(BRANCH_W) // SSD_HD) % 2 == 0
    hmask = np.zeros((BF16_ROWS, BRANCH_W))
    hmask[0] = even_head
    hmask[1] = ~even_head
    as_bf16 = lambda a: jnp.asarray(a, dtype=F32).astype(BF16)
    return as_bf16(ecol), as_bf16(ehead), as_bf16(hmask)


def _ssd(xbc, z, dt, lw, h0):
    bsz, t, _ = xbc.shape
    assert t % SSD_CHUNK == 0
    npairs = SSD_HEADS // 2
    ecol, ehead, hmask = _ssd_tables()
    full = lambda shape: pl.BlockSpec(shape, lambda b: (0,) * len(shape))
    seq = lambda width: pl.BlockSpec((1, t, width), lambda b: (b, 0, 0))
    st_spec = pl.BlockSpec((1, 2, npairs, SSD_STATE, LANES), lambda b: (b, 0, 0, 0, 0))
    return pl.pallas_call(
        functools.partial(_ssd_kernel, t=t),
        grid=(bsz,),
        in_specs=[seq(SSD_XBC), seq(BRANCH_W), seq(LANES), full((1, LANES)), full((1, LANES)),
                  full((1, BRANCH_W)), full((1, BRANCH_W)), full(ecol.shape), full(ehead.shape), full(hmask.shape),
                  st_spec],
        out_specs=[seq(BRANCH_W), st_spec],
        out_shape=[jax.ShapeDtypeStruct((bsz, t, BRANCH_W), BF16),
                   jax.ShapeDtypeStruct((bsz, 2, npairs, SSD_STATE, LANES), F32)],
        scratch_shapes=[pltpu.VMEM((t, BRANCH_W), F32), pltpu.VMEM((t, LANES), F32),
                        pltpu.VMEM((t // SSD_CHUNK, npairs, SSD_STATE, LANES), F32),
                        pltpu.VMEM((npairs, SSD_STATE, LANES), F32)],
        compiler_params=_params("arbitrary"),
        name="ssd",
    )(xbc, z, dt, lw["ssd_dtb"], lw["ssd_alog"], lw["ssd_dskip"], lw["ssd_nw"], ecol, ehead, hmask, h0)


@functools.lru_cache(maxsize=None)
def _fourier_tables(n, blk):
    half = n // 2
    nh = half // blk
    assert nh * blk == half and nh <= SUBLANES
    p = np.arange(FN_GD)
    ang_c = 2.0 * np.pi * ((p[:, None] * p[None, :]) % FN_GD) / FN_GD
    eye = np.eye(BRANCH_W // FN_GD)
    cc = np.kron(eye, np.cos(ang_c))
    sc = np.kron(eye, np.sin(ang_c))
    k = np.arange(half)
    ang_t = 2.0 * np.pi * ((k[:, None] * k[None, :]) % n) / n
    ct = np.cos(ang_t)
    stn = -np.sin(ang_t)
    jp = np.zeros((blk, blk))
    r = np.arange(1, blk)
    jp[r, blk - r] = 1.0
    tt = np.arange(n)
    ks = blk * (np.arange(nh) + 1)
    ang_k = 2.0 * np.pi * ((ks[:, None] * tt[None, :]) % n) / n
    tc = np.zeros((2 * SUBLANES, n))
    ts = np.zeros((2 * SUBLANES, n))
    tc[:nh] = np.cos(ang_k)
    ts[:nh] = np.sin(ang_k)
    tc[SUBLANES, half] = 1.0
    as_bf16 = lambda a: jnp.asarray(a, dtype=F32).astype(BF16)
    return dict(cc=as_bf16(cc), sc=as_bf16(sc), ct=as_bf16(ct), stn=as_bf16(stn), jp=as_bf16(jp),
                tc=as_bf16(tc), ts=as_bf16(ts))


def _fn_fold_kernel(xj_ref, xm_ref, xr_ref, jp_ref, cc_ref, sc_ref, eo_ref):
    j = pl.program_id(1)
    xj = xj_ref[0]
    blk = xj.shape[0]
    mir = _dot(jp_ref[...], xm_ref[0])
    row0 = xr_ref[0, 0:1, :].astype(F32) * jnp.where(j == 0, 0.0, 1.0)
    rows = lax.broadcasted_iota(jnp.int32, (blk, BRANCH_W), 0)
    mir = jnp.where(rows == 0, row0, mir).astype(BF16)
    e = _dot(xj, cc_ref[...]) + _dot(mir, cc_ref[...])
    o = _dot(xj, sc_ref[...]) - _dot(mir, sc_ref[...])
    eo_ref[0] = jnp.concatenate([e, o], axis=1).astype(BF16)


def _fn_aux_kernel(x_ref, tc_ref, ts_ref, cc_ref, sc_ref, o_ref):
    x = x_ref[0]
    xc = _dot(tc_ref[...], x).astype(BF16)
    xs = _dot(ts_ref[...], x).astype(BF16)
    o_ref[0] = _dot(xc, cc_ref[...]) + _dot(xs, sc_ref[...])


def _fn_main_kernel(ct_ref, stn_ref, eo_ref, aux_ref, g_ref, jp_ref, y_ref, r_ref, *, scale):
    m = pl.program_id(0)
    half = pl.program_id(2)
    blk = r_ref.shape[0]
    gate = _silu(g_ref[0].astype(F32)) * scale
    rows = lax.broadcasted_iota(jnp.int32, (blk, BRANCH_W), 0)

    @pl.when(half == 0)
    def _():
        p = _dot(ct_ref[...], eo_ref[0, :, 0:BRANCH_W])
        qn = _dot(stn_ref[...], eo_ref[0, :, BRANCH_W:2 * BRANCH_W])
        sgn = (1 - 2 * ((m * blk + rows) & 1)).astype(F32)
        p = p + sgn * aux_ref[0, SUBLANES:SUBLANES + 1, :]
        r_ref[...] = p - qn
        y_ref[0] = ((p + qn) * gate).astype(BF16)

    @pl.when(half == 1)
    def _():
        flipped = _dot(jp_ref[...], r_ref[...].astype(BF16))
        sel = lax.broadcasted_iota(jnp.int32, (2 * SUBLANES, BRANCH_W), 0) == m
        row0 = jnp.sum(jnp.where(sel, aux_ref[0], 0.0), axis=0, keepdims=True)
        y_ref[0] = (jnp.where(rows == 0, row0, flipped) * gate).astype(BF16)


def _fourier(fn, blk):
    bsz, n, _ = fn.shape
    tb = _fourier_tables(n, blk)
    half = n // 2
    nh = half // blk
    nblk = 2 * nh
    rows16 = n // BF16_ROWS
    full2 = lambda shape: pl.BlockSpec(shape, lambda b, j: (0,) * len(shape))
    eo = pl.pallas_call(
        _fn_fold_kernel,
        grid=(bsz, nh),
        in_specs=[pl.BlockSpec((1, blk, BRANCH_W), lambda b, j: (b, j, 0)),
                  pl.BlockSpec((1, blk, BRANCH_W), lambda b, j: (b, nblk - 1 - j, 0)),
                  pl.BlockSpec((1, BF16_ROWS, BRANCH_W),
                               lambda b, j: (b, jnp.minimum((blk // BF16_ROWS) * (nblk - j), rows16 - 1), 0)),
                  full2((blk, blk)), full2((BRANCH_W, BRANCH_W)), full2((BRANCH_W, BRANCH_W))],
        out_specs=pl.BlockSpec((1, blk, 2 * BRANCH_W), lambda b, j: (b, j, 0)),
        out_shape=jax.ShapeDtypeStruct((bsz, half, 2 * BRANCH_W), BF16),
        compiler_params=_params("arbitrary", "arbitrary"),
        name="fn_fold",
    )(fn, fn, fn, tb["jp"], tb["cc"], tb["sc"])
    full1 = lambda shape: pl.BlockSpec(shape, lambda b: (0,) * len(shape))
    aux = pl.pallas_call(
        _fn_aux_kernel,
        grid=(bsz,),
        in_specs=[pl.BlockSpec((1, n, BRANCH_W), lambda b: (b, 0, 0)),
                  full1((2 * SUBLANES, n)), full1((2 * SUBLANES, n)),
                  full1((BRANCH_W, BRANCH_W)), full1((BRANCH_W, BRANCH_W))],
        out_specs=pl.BlockSpec((1, 2 * SUBLANES, BRANCH_W), lambda b: (b, 0, 0)),
        out_shape=jax.ShapeDtypeStruct((bsz, 2 * SUBLANES, BRANCH_W), F32),
        compiler_params=_params("arbitrary"),
        name="fn_aux",
    )(fn, tb["tc"], tb["ts"], tb["cc"], tb["sc"])
    out_blk = lambda m, b, h: (b, m + h * (nblk - 1 - 2 * m), 0)
    return pl.pallas_call(
        functools.partial(_fn_main_kernel, scale=float(1.0 / np.sqrt(n * FN_GD))),
        grid=(nh, bsz, 2),
        in_specs=[pl.BlockSpec((blk, half), lambda m, b, h: (m, 0)),
                  pl.BlockSpec((blk, half), lambda m, b, h: (m, 0)),
                  pl.BlockSpec((1, half, 2 * BRANCH_W), lambda m, b, h: (b, 0, 0)),
                  pl.BlockSpec((1, 2 * SUBLANES, BRANCH_W), lambda m, b, h: (b, 0, 0)),
                  pl.BlockSpec((1, blk, BRANCH_W), lambda m, b, h: out_blk(m, b, h)[:2] + (1,)),
                  pl.BlockSpec((blk, blk), lambda m, b, h: (0, 0))],
        out_specs=pl.BlockSpec((1, blk, BRANCH_W), out_blk),
        out_shape=jax.ShapeDtypeStruct((bsz, n, BRANCH_W), BF16),
        scratch_shapes=[pltpu.VMEM((blk, BRANCH_W), F32)],
        compiler_params=_params("arbitrary", "arbitrary", "arbitrary"),
        name="fn_main",
    )(tb["ct"], tb["stn"], eo, aux, fn, tb["jp"])


def _blockdiag(w4):
    n = w4.shape[0]
    return jnp.einsum("hij,hg->higj", w4, jnp.eye(n, dtype=w4.dtype)).reshape(n * RG_HD, n * RG_HD)


def _pad_rows(a, rows):
    return jnp.pad(a, ((0, rows - a.shape[0]), (0, 0)))


def _layer_weights(l, w_in, w_out, rg_conv_w, rg_conv_b, rg_gate_a_w, rg_gate_a_b, rg_gate_x_w, rg_gate_x_b,
                   rg_lambda, sc_conv_w, ssd_conv_w, ssd_conv_b, ssd_dt_bias, ssd_a_log, ssd_d, ssd_norm_w):
    idx = [int(i) for i in np.cumsum(COL_WIDTHS)[:-1]]
    rg_x, xbc, dt, rg_g, z, sc_b, sc_c, sc_x, sc_g, fn_x, fn_g = jnp.split(w_in[l], idx, axis=1)
    assert dt.shape[1] == SPLIT_LANES
    dt = jnp.tile(dt, (1, LANES // SPLIT_LANES))
    w = jnp.concatenate([rg_x, rg_g, xbc, z, dt, fn_x, fn_g, sc_b, sc_c, sc_x, sc_g], axis=1).astype(BF16)
    wg, bg = [], []
    for d in range(2):
        wd, bd = [], []
        for half in range(2):
            hs = slice(4 * half, 4 * half + 4)
            cs = slice(256 * half, 256 * half + 256)
            wd.append(jnp.concatenate([_blockdiag(rg_gate_a_w[l, d, hs]), _blockdiag(rg_gate_x_w[l, d, hs])], axis=1))
            bd.append(jnp.concatenate([rg_gate_a_b[l, d, cs], rg_gate_x_b[l, d, cs]])[None, :])
        wg.append(jnp.stack(wd))
        bg.append(jnp.stack(bd))
    pad_lanes = lambda v: jnp.tile(v.reshape(1, -1), (1, LANES // v.size))
    return dict(
        w_in=w, w_out=w_out[l].astype(BF16),
        sc_cw=_pad_rows(sc_conv_w[l], SUBLANES),
        rg_cw=_pad_rows(rg_conv_w[l], SUBLANES), rg_cb=rg_conv_b[l][None, :],
        rg_wg=jnp.stack(wg).astype(BF16), rg_bg=jnp.stack(bg), rg_lam=_pad_rows(rg_lambda[l], SUBLANES),
        ssd_cw=_pad_rows(ssd_conv_w[l], SUBLANES), ssd_cb=ssd_conv_b[l][None, :],
        ssd_dtb=pad_lanes(ssd_dt_bias[l]), ssd_alog=pad_lanes(ssd_a_log[l]),
        ssd_dskip=jnp.repeat(ssd_d[l], SSD_HD)[None, :], ssd_nw=ssd_norm_w[l][None, :],
    )


def _mix(x, nw, shift, scale, gate, lw, rowlen, tm, tt, blk, rg_h0, ssd_h0, with_output, final, fw):
    rg, xbc, z, dt, fn, y_sc = _inproj(x, nw, scale, shift, lw, rowlen, tm)
    y_rg, rg_st = _rg(rg, lw, rg_h0, tt)
    y_ssd, ssd_st = _ssd(xbc, z, dt, lw, ssd_h0)
    if not with_output:
        return None, rg_st, ssd_st
    y_fn = _fourier(fn, blk)
    return _outproj(y_rg, y_sc, y_fn, y_ssd, lw["w_out"], x, gate, fw, final, tm), rg_st, ssd_st


def kernel(x, c, ctx, c_ctx, ada_w, ada_b, norm_w, w_in, w_out, rg_conv_w, rg_conv_b, rg_gate_a_w, rg_gate_a_b,
           rg_gate_x_w, rg_gate_x_b, rg_lambda, sc_conv_w, ssd_conv_w, ssd_conv_b, ssd_dt_bias, ssd_a_log, ssd_d,
           ssd_norm_w, final_norm_w):
    bsz, seq, _ = x.shape
    ctx_len = ctx.shape[1]
    depth = w_in.shape[0]
    assert bsz + 1 <= SUBLANES
    c8 = jnp.concatenate([c, c_ctx[None, :], jnp.zeros((SUBLANES - bsz - 1, D_MODEL), F32)], axis=0)
    fw = final_norm_w[None, :]
    zeros_rg = jnp.zeros((bsz, SUBLANES, BRANCH_W), F32)
    zeros_ssd = jnp.zeros((bsz, 2, SSD_HEADS // 2, SSD_STATE, LANES), F32)
    tm_lat = min(512, seq)
    tt_lat = min(256, seq)
    tt_ctx = min(256, ctx_len)
    blk_lat = min(512, seq // 2)
    blk_ctx = min(512, ctx_len // 2)
    for l in range(depth):
        lw = _layer_weights(l, w_in, w_out, rg_conv_w, rg_conv_b, rg_gate_a_w, rg_gate_a_b, rg_gate_x_w,
                            rg_gate_x_b, rg_lambda, sc_conv_w, ssd_conv_w, ssd_conv_b, ssd_dt_bias, ssd_a_log,
                            ssd_d, ssd_norm_w)
        mod = _ada(c8, ada_w[l], ada_b[l][None, :])
        shift, scale, gate = (mod[:, i * D_MODEL:(i + 1) * D_MODEL] for i in range(3))
        lat = lambda m: m[:bsz, None, :]
        con = lambda m: jnp.broadcast_to(m[bsz:bsz + 1, None, :], (bsz, 1, D_MODEL))
        nw = norm_w[l][None, :]
        last = l == depth - 1
        new_ctx, rg_st, ssd_st = _mix(ctx, nw, con(shift), con(scale), con(gate), lw, ctx_len, ctx_len, tt_ctx,
                                      blk_ctx, zeros_rg, zeros_ssd, not last, False, fw)
        x, _, _ = _mix(x, nw, lat(shift), lat(scale), lat(gate), lw, GRID_W, tm_lat, tt_lat, blk_lat,
                       rg_st, ssd_st, True, last, fw)
        if not last:
            ctx = new_ctx
    return x
```

```python
import functools

import numpy as np
import jax
import jax.numpy as jnp
from jax import lax
from jax.experimental import pallas as pl
from jax.experimental.pallas import tpu as pltpu

F32 = jnp.float32
BF16 = jnp.bfloat16
HIGHEST = lax.Precision.HIGHEST

D_MODEL = 1024
D_INNER = 2048
BRANCH_W = 512
GRID_W = 64
RG_HD = 64
RG_C = 8.0
RG_HALF = 256
SSD_HEADS = 8
SSD_HD = 64
SSD_GROUPS = 2
SSD_STATE = 64
SSD_XBC = BRANCH_W + 2 * SSD_GROUPS * SSD_STATE
SSD_CHUNK = 128
SSD_DT = 2 * SSD_HEADS
FN_GD = 128
NORM_EPS = 1e-6

LANES = 128
SUBLANES = 8
BF16_ROWS = 16
VMEM_LIMIT_BYTES = 60000 * 1024

C_RGX, C_XBC, C_RGG, C_Z, C_SC, C_FN, C_DT, C_END = 0, 512, 1280, 1792, 2304, 4352, 5376, 5504
REF_DT_LO, REF_DT_HI = 1280, 1296


def _sigmoid(v):
    return 0.5 + 0.5 * jnp.tanh(0.5 * v)


def _silu(v):
    h = 0.5 * v
    return h + h * jnp.tanh(h)


def _softplus(v):
    return jnp.maximum(v, 0.0) + jnp.log1p(jnp.exp(-jnp.abs(v)))


def _dot(a, b):
    return jnp.dot(a, b, preferred_element_type=F32)


def _split3(v):
    hi = v.astype(BF16).astype(F32)
    rest = v - hi
    mid = rest.astype(BF16).astype(F32)
    return hi, mid, rest - mid


def _pack3(v):
    hi, mid, lo = _split3(v)
    period = lax.broadcasted_iota(jnp.int32, v.shape, 1) // SSD_DT
    return jnp.where(period == 0, hi, jnp.where(period == 1, mid, jnp.where(period == 2, lo, 0.0))).astype(BF16)


def _params(*semantics):
    return pltpu.CompilerParams(dimension_semantics=semantics, vmem_limit_bytes=VMEM_LIMIT_BYTES)


def _layer_spec(l, shape, nargs):
    return pl.BlockSpec((None,) + tuple(shape), lambda *_: (l,) + (0,) * len(shape))


def _ada_kernel(c_ref, w_ref, b_ref, o_ref):
    o_ref[...] = jnp.dot(_silu(c_ref[...]), w_ref[...], precision=HIGHEST,
                         preferred_element_type=F32) + b_ref[...]


def _ada(c8, w, b):
    depth = w.shape[0]
    tn = 512
    return pl.pallas_call(
        _ada_kernel,
        grid=(depth, 3 * D_MODEL // tn),
        in_specs=[pl.BlockSpec((SUBLANES, D_MODEL), lambda l, j: (0, 0)),
                  pl.BlockSpec((None, D_MODEL, tn), lambda l, j: (l, 0, j)),
                  pl.BlockSpec((None, 1, tn), lambda l, j: (l, 0, j))],
        out_specs=pl.BlockSpec((None, SUBLANES, tn), lambda l, j: (l, 0, j)),
        out_shape=jax.ShapeDtypeStruct((depth, SUBLANES, 3 * D_MODEL), F32),
        compiler_params=_params("arbitrary", "arbitrary"),
        name="ada",
    )(c8, w, b)


def _inproj_kernel(x_ref, xp_ref, xn_ref, nw_ref, sh_ref, sc_ref, w_ref, cw_ref, rcw_ref, rcb_ref, scw_ref, scb_ref,
                   rg_ref, xbc_ref, z_ref, dt_ref, fn_ref, ysc_ref, *, rowlen, mod_row):
    i = pl.program_id(1)
    tm = x_ref.shape[1]
    halo = xp_ref.shape[1]
    row = pl.program_id(0) if mod_row is None else mod_row
    x = jnp.concatenate([xp_ref[0], x_ref[0], xn_ref[0]], axis=0)
    ms = jnp.mean(x * x, axis=-1, keepdims=True)
    h = (x * lax.rsqrt(ms + NORM_EPS)) * nw_ref[...]
    h = h * (1.0 + sc_ref[pl.ds(row, 1), :]) + sh_ref[pl.ds(row, 1), :]
    hx = h.astype(BF16)
    hb = hx[halo:halo + tm]
    keep_prev = jnp.where(i == 0, 0.0, 1.0)
    keep_next = jnp.where(i == pl.num_programs(1) - 1, 0.0, 1.0)

    def proj(lo, hi):
        return _dot(hb, w_ref[:, lo:hi])

    def conv4(lo, hi, cw, cb):
        p = _dot(hx, w_ref[:, lo:hi])
        p = jnp.concatenate([p[:halo] * keep_prev, p[halo:halo + tm], p[halo + tm:] * keep_next], axis=0)
        n = tm + 2 * halo
        mid = slice(halo, halo + tm)
        return (cw[0:1, :] * pltpu.roll(p, 2, 0)[mid] + cw[1:2, :] * pltpu.roll(p, 1, 0)[mid]
                + cw[2:3, :] * p[mid] + cw[3:4, :] * pltpu.roll(p, n - 1, 0)[mid] + cb)

    rg_ref[0, :, 0:BRANCH_W] = conv4(C_RGX, C_XBC, rcw_ref[...], rcb_ref[...]).astype(BF16)
    rg_ref[0, :, BRANCH_W:] = proj(C_RGG, C_Z).astype(BF16)
    xbc_ref[0] = _silu(conv4(C_XBC, C_RGG, scw_ref[...], scb_ref[...])).astype(BF16)
    z_ref[0] = proj(C_Z, C_SC).astype(BF16)
    dt_ref[0] = proj(C_DT, C_END)
    fn_ref[0] = proj(C_FN, C_DT).astype(BF16)
    bg = proj(C_SC, C_SC + 512)
    v = proj(C_SC + 512, C_SC + 1024) * proj(C_SC + 1024, C_SC + 1536)
    g = proj(C_SC + 1536, C_FN)
    pos = lax.broadcasted_iota(jnp.int32, (tm, BRANCH_W), 0) & (rowlen - 1)
    vm1 = jnp.where(pos == 0, 0.0, pltpu.roll(v, 1, 0))
    vp1 = jnp.where(pos == rowlen - 1, 0.0, pltpu.roll(v, tm - 1, 0))
    vc = cw_ref[0:1, :] * vm1 + cw_ref[1:2, :] * v + cw_ref[2:3, :] * vp1
    ysc_ref[0] = (bg * vc * _silu(g)).astype(BF16)


def _inproj(x, mod, pw, l, rowlen, tm, mod_row, interleave):
    bsz, t, _ = x.shape
    halo = BF16_ROWS
    nt = t // tm
    assert t % tm == 0 and tm % rowlen == 0 and rowlen & (rowlen - 1) == 0 and tm % halo == 0
    per_tile = tm // halo
    last_halo = t // halo - 1
    tok = lambda width: pl.BlockSpec((1, tm, width), lambda b, i: (b, i, 0))
    lay = lambda *shape: _layer_spec(l, shape, 2)
    out = lambda width, dt: jax.ShapeDtypeStruct((bsz, t, width), dt)
    if interleave:
        assert nt == SUBLANES
        rg_spec = pl.BlockSpec((1, tm, 1024), lambda b, i: (b, 0, i))
        rg_shape = jax.ShapeDtypeStruct((bsz, tm, nt * 1024), BF16)
    else:
        rg_spec, rg_shape = tok(1024), out(1024, BF16)
    outs = pl.pallas_call(
        functools.partial(_inproj_kernel, rowlen=rowlen, mod_row=mod_row),
        grid=(bsz, nt),
        in_specs=[tok(D_MODEL),
                  pl.BlockSpec((1, halo, D_MODEL), lambda b, i: (b, jnp.maximum(i * per_tile - 1, 0), 0)),
                  pl.BlockSpec((1, halo, D_MODEL), lambda b, i: (b, jnp.minimum((i + 1) * per_tile, last_halo), 0)),
                  lay(1, D_MODEL),
                  pl.BlockSpec((None, SUBLANES, D_MODEL), lambda b, i: (l, 0, 0)),
                  pl.BlockSpec((None, SUBLANES, D_MODEL), lambda b, i: (l, 0, 1)),
                  lay(D_MODEL, C_END), lay(3, BRANCH_W), lay(4, BRANCH_W), lay(1, BRANCH_W),
                  lay(4, SSD_XBC), lay(1, SSD_XBC)],
        out_specs=[rg_spec, tok(SSD_XBC), tok(512), tok(LANES), tok(1024), tok(512)],
        out_shape=[rg_shape, out(SSD_XBC, BF16), out(512, BF16), out(LANES, F32), out(1024, BF16), out(512, BF16)],
        compiler_params=_params("arbitrary", "arbitrary"),
        name="inproj",
    )(x, x, x, pw["norm_w"], mod, mod, pw["w_in"], pw["sc_cw"], pw["rg_cw"], pw["rg_cb"], pw["ssd_cw"], pw["ssd_cb"])
    return [outs[0].reshape(bsz, t, 1024)] + list(outs[1:])


def _outproj_kernel(yrg_ref, ysc_ref, yfn_ref, yssd_ref, w_ref, x_ref, g_ref, fw_ref, o_ref, *, final, mod_row):
    row = pl.program_id(0) if mod_row is None else mod_row
    acc = _dot(yrg_ref[0], w_ref[0:512, :])
    acc += _dot(ysc_ref[0], w_ref[512:1024, :])
    acc += _dot(yfn_ref[0], w_ref[1024:1536, :])
    acc += _dot(yssd_ref[0], w_ref[1536:2048, :])
    xn = x_ref[0] + g_ref[pl.ds(row, 1), :] * acc
    if final:
        ms = jnp.mean(xn * xn, axis=-1, keepdims=True)
        xn = (xn * lax.rsqrt(ms + NORM_EPS)) * fw_ref[...]
    o_ref[0] = xn


def _outproj(y_rg, y_sc, y_fn, y_ssd, x, mod, pw, l, final, tm, mod_row, interleave):
    bsz, t, _ = x.shape
    nt = t // tm
    ytok = pl.BlockSpec((1, tm, BRANCH_W), lambda b, i: (b, i, 0))
    xtok = pl.BlockSpec((1, tm, D_MODEL), lambda b, i: (b, i, 0))
    if interleave:
        assert nt == SUBLANES
        y_rg = y_rg.reshape(bsz, tm, nt * BRANCH_W)
        rg_spec = pl.BlockSpec((1, tm, BRANCH_W), lambda b, i: (b, 0, i))
    else:
        rg_spec = ytok
    return pl.pallas_call(
        functools.partial(_outproj_kernel, final=final, mod_row=mod_row),
        grid=(bsz, nt),
        in_specs=[rg_spec, ytok, ytok, ytok,
                  _layer_spec(l, (D_INNER, D_MODEL), 2),
                  xtok,
                  pl.BlockSpec((None, SUBLANES, D_MODEL), lambda b, i: (l, 0, 2)),
                  pl.BlockSpec((1, D_MODEL), lambda b, i: (0, 0))],
        out_specs=xtok,
        out_shape=jax.ShapeDtypeStruct((bsz, t, D_MODEL), F32),
        compiler_params=_params("arbitrary", "arbitrary"),
        name="outproj",
    )(y_rg, y_sc, y_fn, y_ssd, pw["w_out"], x, mod, pw["final_w"])


MIN_NORMAL_F32 = float(np.finfo(np.float32).tiny)


def _rg_coefficients(ub, wg_half, bg_half, lam):
    uh = 0.5 * ub.astype(F32)
    th = jnp.tanh(_dot(ub, wg_half) + bg_half)
    k = (-0.5 * RG_C * np.log2(np.e)) * _softplus(-lam)
    a = jnp.exp2(k + k * th[:, :RG_HALF])
    x = 1.0 - a * a
    root = x * lax.rsqrt(jnp.maximum(x, MIN_NORMAL_F32))
    return a, root * (uh + uh * th[:, RG_HALF:])


def _scan_tile(a, v, h_in, sub, reverse):
    rows = a.shape[0]
    for s in (1, 2, 4):
        if reverse:
            keep = sub < SUBLANES - s
            a_sh = jnp.where(keep, pltpu.roll(a, rows - s, 0), 1.0)
            v_sh = jnp.where(keep, pltpu.roll(v, rows - s, 0), 0.0)
        else:
            keep = sub >= s
            a_sh = jnp.where(keep, pltpu.roll(a, s, 0), 1.0)
            v_sh = jnp.where(keep, pltpu.roll(v, s, 0), 0.0)
        v = v + a * v_sh
        a = a * a_sh
    ngroups = rows // SUBLANES
    out = [None] * ngroups
    h = h_in
    order = range(ngroups - 1, -1, -1) if reverse else range(ngroups)
    for g in order:
        lo = g * SUBLANES
        hg = v[lo:lo + SUBLANES] + a[lo:lo + SUBLANES] * h
        out[g] = hg
        h = hg[0:1] if reverse else hg[SUBLANES - 1:SUBLANES]
    return jnp.concatenate(out, axis=0), h


def _rg_kernel(u_ref, g_ref, wg_ref, bg_ref, lam_ref, h0_ref, y_ref, ht_ref, hb_ref, *, t, tt):
    nt = t // tt
    sub = lax.broadcasted_iota(jnp.int32, (tt, RG_HALF), 0) & (SUBLANES - 1)

    def coeffs(s, d):
        r0 = pl.multiple_of(s * tt, tt)
        return _rg_coefficients(u_ref[0, pl.ds(r0, tt), :], wg_ref[d], bg_ref[d], lam_ref[d:d + 1, :])

    def rev_body(i, h):
        s = nt - 1 - i
        a, v = coeffs(s, 1)
        hh, hn = _scan_tile(a, v, h, sub, True)
        hb_ref[pl.ds(pl.multiple_of(s * tt, tt), tt), :] = hh
        return hn

    h_rev = lax.fori_loop(0, nt, rev_body, h0_ref[0, 1:2, :])

    def fwd_body(s, h):
        a, v = coeffs(s, 0)
        hh, hn = _scan_tile(a, v, h, sub, False)
        r0 = pl.multiple_of(s * tt, tt)
        g = g_ref[0, pl.ds(r0, tt), :].astype(F32)
        y_ref[0, pl.ds(r0, tt), :] = ((hh + hb_ref[pl.ds(r0, tt), :]) * _silu(g)).astype(BF16)
        return hn

    h_fwd = lax.fori_loop(0, nt, fwd_body, h0_ref[0, 0:1, :])
    ht_ref[0] = jnp.concatenate([h_fwd, h_rev, jnp.zeros((SUBLANES - 2, RG_HALF), F32)], axis=0)


def _rg_interleaved_kernel(u_ref, g_ref, wg_ref, bg_ref, lam_ref, h0_ref, y_ref, s_ref, af_ref, ab_ref, *, t, tt):
    nt = t // tt
    ng = tt // SUBLANES
    shape = (SUBLANES, RG_HALF)

    def sweep(d):
        def body(i, carry):
            h, acc = carry
            s = i if d == 0 else nt - 1 - i
            r0 = pl.multiple_of(s * tt, tt)
            a, v = _rg_coefficients(u_ref[0, pl.ds(r0, tt), :], wg_ref[d], bg_ref[d], lam_ref[d:d + 1, :])
            hs, ps = [None] * ng, [None] * ng
            for g in (range(ng) if d == 0 else range(ng - 1, -1, -1)):
                ag = a[g * SUBLANES:(g + 1) * SUBLANES]
                h = ag * h + v[g * SUBLANES:(g + 1) * SUBLANES]
                acc = ag * acc
                hs[g], ps[g] = h, acc
            local = jnp.concatenate(hs, axis=0)
            if d == 1:
                s_ref[pl.ds(r0, tt), :] = local
                ab_ref[pl.ds(r0, tt), :] = jnp.concatenate(ps, axis=0)
            else:
                s_ref[pl.ds(r0, tt), :] += local
                af_ref[pl.ds(r0, tt), :] = jnp.concatenate(ps, axis=0)
            return h, acc

        return lax.fori_loop(0, nt, body, (jnp.zeros(shape, F32), jnp.ones(shape, F32)))

    hb, pb = sweep(1)
    hf, pf = sweep(0)
    cf = [h0_ref[0, 0:1, :]]
    for s in range(SUBLANES - 1):
        cf.append(hf[s:s + 1] + pf[s:s + 1] * cf[s])
    cb = [None] * SUBLANES
    cb[SUBLANES - 1] = h0_ref[0, 1:2, :]
    for s in range(SUBLANES - 1, 0, -1):
        cb[s - 1] = hb[s:s + 1] + pb[s:s + 1] * cb[s]
    cf = jnp.concatenate(cf, axis=0)[None]
    cb = jnp.concatenate(cb, axis=0)[None]

    def fix(s, carry):
        r0 = pl.multiple_of(s * tt, tt)
        corr = (af_ref[pl.ds(r0, tt), :].reshape(ng, SUBLANES, RG_HALF) * cf
                + ab_ref[pl.ds(r0, tt), :].reshape(ng, SUBLANES, RG_HALF) * cb).reshape(tt, RG_HALF)
        g = g_ref[0, pl.ds(r0, tt), :].astype(F32)
        y_ref[0, pl.ds(r0, tt), :] = ((s_ref[pl.ds(r0, tt), :] + corr) * _silu(g)).astype(BF16)
        return carry

    lax.fori_loop(0, nt, fix, 0)


def _rg(rg, pw, l, h0, tt, interleaved):
    bsz, t, _ = rg.shape
    assert t % tt == 0
    nhalf = BRANCH_W // RG_HALF
    seq = lambda col0: pl.BlockSpec((1, t, RG_HALF), lambda b, h: (b, 0, col0 + h))
    in_specs = [seq(0), seq(nhalf),
                pl.BlockSpec((None, 2, None, RG_HALF, 2 * RG_HALF), lambda b, h: (l, 0, h, 0, 0)),
                pl.BlockSpec((None, 2, None, 1, 2 * RG_HALF), lambda b, h: (l, 0, h, 0, 0)),
                pl.BlockSpec((None, 2, RG_HALF), lambda b, h: (l, 0, h)),
                pl.BlockSpec((1, SUBLANES, RG_HALF), lambda b, h: (b, 0, h))]
    args = (rg, rg, pw["rg_wg"], pw["rg_bg"], pw["rg_lam"], h0)
    y_spec = pl.BlockSpec((1, t, RG_HALF), lambda b, h: (b, 0, h))
    y_shape = jax.ShapeDtypeStruct((bsz, t, BRANCH_W), BF16)
    seq_scratch = pltpu.VMEM((t, RG_HALF), F32)
    if interleaved:
        y = pl.pallas_call(
            functools.partial(_rg_interleaved_kernel, t=t, tt=tt),
            grid=(bsz, nhalf), in_specs=in_specs, out_specs=y_spec, out_shape=y_shape,
            scratch_shapes=[seq_scratch, seq_scratch, seq_scratch],
            compiler_params=_params("arbitrary", "arbitrary"), name="rglru_interleaved",
        )(*args)
        return y, None
    return pl.pallas_call(
        functools.partial(_rg_kernel, t=t, tt=tt),
        grid=(bsz, nhalf), in_specs=in_specs,
        out_specs=[y_spec, pl.BlockSpec((1, SUBLANES, RG_HALF), lambda b, h: (b, 0, h))],
        out_shape=[y_shape, jax.ShapeDtypeStruct((bsz, SUBLANES, BRANCH_W), F32)],
        scratch_shapes=[seq_scratch],
        compiler_params=_params("arbitrary", "arbitrary"), name="rglru",
    )(*args)


def _ssd_kernel(xbc_ref, z_ref, dt_ref, dtb_ref, alog_ref, dsk_ref, nw_ref, ecol_ref, ehead_ref,
                hmask_ref, h0_ref, y_ref, ht_ref, yb_ref, cs_ref, loc_ref, st_ref, *, t):
    q = SSD_CHUNK
    nc = t // q
    npairs = SSD_HEADS // 2
    pairs_per_group = npairs // SSD_GROUPS
    lane1 = lax.broadcasted_iota(jnp.int32, (1, LANES), 1)
    a_row = -jnp.exp(alog_ref[...])
    fwd_lane = (lax.broadcasted_iota(jnp.int32, (q, LANES), 1) & (SSD_DT - 1)) < SSD_HEADS
    ri = lax.broadcasted_iota(jnp.int32, (q, q), 0)
    ci = lax.broadcasted_iota(jnp.int32, (q, q), 1)
    lower = ri >= ci
    lower_b = jnp.where(lower, 1.0, 0.0).astype(BF16)
    head0_s = lax.broadcasted_iota(jnp.int32, (SSD_STATE, LANES), 1) < SSD_HD

    def pair_vec(row, l0, l1):
        return jnp.where(lane1 < SSD_HD, row[:, l0:l0 + 1], row[:, l1:l1 + 1])

    def pass1(i, carry):
        c = nc - 1 - i
        r0 = pl.multiple_of(c * q, q)
        xs_b = xbc_ref[0, pl.ds(r0, q), 0:BRANCH_W]
        bm_b = xbc_ref[0, pl.ds(r0, q), BRANCH_W:BRANCH_W + LANES]
        cm_b = xbc_ref[0, pl.ds(r0, q), BRANCH_W + LANES:SSD_XBC]
        dtv = _softplus(dt_ref[0, pl.ds(r0, q), :] + dtb_ref[...])
        da = dtv * a_row
        pre = sum(_dot(lower_b, part.astype(BF16)) for part in _split3(da))
        tot = pre[q - 1:q, :]
        cs = jnp.where(fwd_lane, pre, tot - pre + da)
        wgt = dtv * jnp.exp(tot - cs)
        etot = jnp.exp(tot)
        cs_p = _pack3(cs)
        cs_col = _dot(cs_p, ecol_ref[...])
        ecs_b = jnp.exp(_dot(cs_p, ehead_ref[:, BRANCH_W:]))
        cst = cs.T
        dtt = dtv.T
        wgtt = wgt.T
        bt = bm_b.astype(F32).T
        xs_h0 = xs_b * hmask_ref[0:1, :]
        xs_h1 = xs_b * hmask_ref[1:2, :]
        ys = []
        for g in range(SSD_GROUPS):
            gs = slice(g * SSD_STATE, (g + 1) * SSD_STATE)
            cbm = lax.dot_general(cm_b[:, gs], bm_b[:, gs], (((1,), (1,)), ((), ())), preferred_element_type=F32)
            btg = bt[gs, :]
            for pp in range(pairs_per_group):
                pair = g * pairs_per_group + pp
                ps = slice(pair * LANES, (pair + 1) * LANES)
                state = st_ref[pair]
                lf0 = 2 * pair
                lb0 = SSD_HEADS + lf0
                ms = []
                for lf in (lf0, lf0 + 1):
                    lb = SSD_HEADS + lf
                    dtf = dtt[lf:lf + 1, :]
                    dtb = dtt[lb:lb + 1, :]
                    arg = jnp.where(lower, cs_col[:, lf * LANES:(lf + 1) * LANES] - cst[lf:lf + 1, :],
                                    cs_col[:, lb * LANES:(lb + 1) * LANES] - cst[lb:lb + 1, :])
                    w = jnp.where(ri > ci, dtf, jnp.where(ri < ci, dtb, dtf + dtb))
                    ms.append((cbm * jnp.exp(arg) * w).astype(BF16))
                yd = _dot(jnp.concatenate(ms, axis=1), jnp.concatenate([xs_h0[:, ps], xs_h1[:, ps]], axis=0))
                yo = _dot(cm_b[:, gs], state.astype(BF16)) * ecs_b[:, ps]
                ys.append(yd + yo)
                stack = jnp.concatenate([(btg * wgtt[l:l + 1, :]).astype(BF16) for l in (lf0, lf0 + 1, lb0, lb0 + 1)],
                                        axis=0)
                res = _dot(stack, xs_b[:, ps])
                n = SSD_STATE
                loc_ref[c, pair] = jnp.where(head0_s, res[0:n], res[n:2 * n])
                st_ref[pair] = state * pair_vec(etot, lb0, lb0 + 1) + jnp.where(head0_s, res[2 * n:3 * n], res[3 * n:])
        yb_ref[pl.ds(r0, q), :] = jnp.concatenate(ys, axis=1) + dsk_ref[...] * xs_b.astype(F32)
        cs_ref[pl.ds(r0, q), :] = cs
        return carry

    def pass2(c, carry):
        r0 = pl.multiple_of(c * q, q)
        cs = cs_ref[pl.ds(r0, q), :]
        cm_b = xbc_ref[0, pl.ds(r0, q), BRANCH_W + LANES:SSD_XBC]
        ecs_f = jnp.exp(_dot(_pack3(cs), ehead_ref[:, :BRANCH_W]))
        etot = jnp.exp(cs[q - 1:q, :])
        ys = []
        for pair in range(npairs):
            g = pair // pairs_per_group
            state = st_ref[pair]
            lf0 = 2 * pair
            ys.append(_dot(cm_b[:, g * SSD_STATE:(g + 1) * SSD_STATE], state.astype(BF16))
                      * ecs_f[:, pair * LANES:(pair + 1) * LANES])
            st_ref[pair] = state * pair_vec(etot, lf0, lf0 + 1) + loc_ref[c, pair]
        yt = yb_ref[pl.ds(r0, q), :] + jnp.concatenate(ys, axis=1)
        yt = yt * _silu(z_ref[0, pl.ds(r0, q), :].astype(F32))
        ms = jnp.mean(yt * yt, axis=-1, keepdims=True)
        y_ref[0, pl.ds(r0, q), :] = ((yt * lax.rsqrt(ms + NORM_EPS)) * nw_ref[...]).astype(BF16)
        return carry

    for p in range(npairs):
        st_ref[p] = h0_ref[0, 1, p]
    lax.fori_loop(0, nc, pass1, 0)
    for p in range(npairs):
        ht_ref[0, 1, p] = st_ref[p]
        st_ref[p] = h0_ref[0, 0, p]
    lax.fori_loop(0, nc, pass2, 0)
    for p in range(npairs):
        ht_ref[0, 0, p] = st_ref[p]


@functools.lru_cache(maxsize=None)
def _ssd_tables():
    lanes = np.arange(LANES)[:, None]
    src = np.where(lanes < 3 * SSD_DT, lanes % SSD_DT, -1)
    ecol = (src == np.arange(SSD_DT * LANES)[None, :] // LANES)
    ehead = (src == np.arange(2 * BRANCH_W)[None, :] // SSD_HD)
    even_head = (np.arange(BRANCH_W) // SSD_HD) % 2 == 0
    hmask = np.zeros((BF16_ROWS, BRANCH_W))
    hmask[0] = even_head
    hmask[1] = ~even_head
    as_bf16 = lambda a: jnp.asarray(a, dtype=F32).astype(BF16)
    return as_bf16(ecol), as_bf16(ehead), as_bf16(hmask)


def _ssd(xbc, z, dt, pw, l, h0):
    bsz, t, _ = xbc.shape
    assert t % SSD_CHUNK == 0
    npairs = SSD_HEADS // 2
    ecol, ehead, hmask = _ssd_tables()
    full = lambda shape: pl.BlockSpec(shape, lambda b: (0,) * len(shape))
    lay = lambda *shape: _layer_spec(l, shape, 1)
    seq = lambda width: pl.BlockSpec((1, t, width), lambda b: (b, 0, 0))
    st_spec = pl.BlockSpec((1, 2, npairs, SSD_STATE, LANES), lambda b: (b, 0, 0, 0, 0))
    return pl.pallas_call(
        functools.partial(_ssd_kernel, t=t),
        grid=(bsz,),
        in_specs=[seq(SSD_XBC), seq(BRANCH_W), seq(LANES), lay(1, LANES), lay(1, LANES),
                  lay(1, BRANCH_W), lay(1, BRANCH_W), full(ecol.shape), full(ehead.shape), full(hmask.shape),
                  st_spec],
        out_specs=[seq(BRANCH_W), st_spec],
        out_shape=[jax.ShapeDtypeStruct((bsz, t, BRANCH_W), BF16),
                   jax.ShapeDtypeStruct((bsz, 2, npairs, SSD_STATE, LANES), F32)],
        scratch_shapes=[pltpu.VMEM((t, BRANCH_W), F32), pltpu.VMEM((t, LANES), F32),
                        pltpu.VMEM((t // SSD_CHUNK, npairs, SSD_STATE, LANES), F32),
                        pltpu.VMEM((npairs, SSD_STATE, LANES), F32)],
        compiler_params=_params("arbitrary"),
        name="ssd",
    )(xbc, z, dt, pw["ssd_dtb"], pw["ssd_alog"], pw["ssd_dskip"], pw["ssd_nw"], ecol, ehead, hmask, h0)


@functools.lru_cache(maxsize=None)
def _fourier_tables(n, blk):
    half = n // 2
    nh = half // blk
    assert nh * blk == half and nh <= SUBLANES
    p = np.arange(FN_GD)
    ang_c = 2.0 * np.pi * ((p[:, None] * p[None, :]) % FN_GD) / FN_GD
    eye = np.eye(BRANCH_W // FN_GD)
    cc = np.kron(eye, np.cos(ang_c))
    sc = np.kron(eye, np.sin(ang_c))
    k = np.arange(half)
    ang_t = 2.0 * np.pi * ((k[:, None] * k[None, :]) % n) / n
    ct = np.cos(ang_t)
    stn = -np.sin(ang_t)
    jp = np.zeros((blk, blk))
    r = np.arange(1, blk)
    jp[r, blk - r] = 1.0
    tt = np.arange(n)
    ks = blk * (np.arange(nh) + 1)
    ang_k = 2.0 * np.pi * ((ks[:, None] * tt[None, :]) % n) / n
    tc = np.zeros((2 * SUBLANES, n))
    ts = np.zeros((2 * SUBLANES, n))
    tc[:nh] = np.cos(ang_k)
    ts[:nh] = np.sin(ang_k)
    tc[SUBLANES, half] = 1.0
    as_bf16 = lambda a: jnp.asarray(a, dtype=F32).astype(BF16)
    return dict(cc=as_bf16(cc), sc=as_bf16(sc), ct=as_bf16(ct), stn=as_bf16(stn), jp=as_bf16(jp),
                tc=as_bf16(tc), ts=as_bf16(ts))


def _fn_fold_kernel(xj_ref, xm_ref, xr_ref, jp_ref, cc_ref, sc_ref, eo_ref):
    j = pl.program_id(1)
    xj = xj_ref[0]
    blk = xj.shape[0]
    mir = _dot(jp_ref[...], xm_ref[0])
    row0 = xr_ref[0, 0:1, :].astype(F32) * jnp.where(j == 0, 0.0, 1.0)
    rows = lax.broadcasted_iota(jnp.int32, (blk, BRANCH_W), 0)
    mir = jnp.where(rows == 0, row0, mir).astype(BF16)
    e = _dot(xj, cc_ref[...]) + _dot(mir, cc_ref[...])
    o = _dot(xj, sc_ref[...]) - _dot(mir, sc_ref[...])
    eo_ref[0] = jnp.concatenate([e, o], axis=1).astype(BF16)


def _fn_aux_kernel(x_ref, tc_ref, ts_ref, cc_ref, sc_ref, o_ref):
    x = x_ref[0]
    xc = _dot(tc_ref[...], x).astype(BF16)
    xs = _dot(ts_ref[...], x).astype(BF16)
    o_ref[0] = _dot(xc, cc_ref[...]) + _dot(xs, sc_ref[...])


def _fn_main_kernel(ct_ref, stn_ref, eo_ref, aux_ref, g_ref, jp_ref, y_ref, *, scale):
    m = pl.program_id(1)
    blk = ct_ref.shape[0]
    nblk = y_ref.shape[1] // blk
    rows = lax.broadcasted_iota(jnp.int32, (blk, BRANCH_W), 0)
    p = _dot(ct_ref[...], eo_ref[0, :, 0:BRANCH_W])
    qn = _dot(stn_ref[...], eo_ref[0, :, BRANCH_W:2 * BRANCH_W])
    sgn = (1 - 2 * ((m * blk + rows) & 1)).astype(F32)
    p = p + sgn * aux_ref[0, SUBLANES:SUBLANES + 1, :]
    lo = pl.multiple_of(m * blk, blk)
    hi = pl.multiple_of((nblk - 1 - m) * blk, blk)
    y_ref[0, pl.ds(lo, blk), :] = ((p + qn) * (_silu(g_ref[0, pl.ds(lo, blk), :].astype(F32)) * scale)).astype(BF16)
    flipped = _dot(jp_ref[...], (p - qn).astype(BF16))
    sel = lax.broadcasted_iota(jnp.int32, (2 * SUBLANES, BRANCH_W), 0) == m
    row0 = jnp.sum(jnp.where(sel, aux_ref[0], 0.0), axis=0, keepdims=True)
    y_ref[0, pl.ds(hi, blk), :] = (jnp.where(rows == 0, row0, flipped)
                                   * (_silu(g_ref[0, pl.ds(hi, blk), :].astype(F32)) * scale)).astype(BF16)


def _fourier(fn, blk):
    bsz, n, _ = fn.shape
    tb = _fourier_tables(n, blk)
    half = n // 2
    nh = half // blk
    nblk = 2 * nh
    rows16 = n // BF16_ROWS
    full2 = lambda shape: pl.BlockSpec(shape, lambda b, j: (0,) * len(shape))
    eo = pl.pallas_call(
        _fn_fold_kernel,
        grid=(bsz, nh),
        in_specs=[pl.BlockSpec((1, blk, BRANCH_W), lambda b, j: (b, j, 0)),
                  pl.BlockSpec((1, blk, BRANCH_W), lambda b, j: (b, nblk - 1 - j, 0)),
                  pl.BlockSpec((1, BF16_ROWS, BRANCH_W),
                               lambda b, j: (b, jnp.minimum((blk // BF16_ROWS) * (nblk - j), rows16 - 1), 0)),
                  full2((blk, blk)), full2((BRANCH_W, BRANCH_W)), full2((BRANCH_W, BRANCH_W))],
        out_specs=pl.BlockSpec((1, blk, 2 * BRANCH_W), lambda b, j: (b, j, 0)),
        out_shape=jax.ShapeDtypeStruct((bsz, half, 2 * BRANCH_W), BF16),
        compiler_params=_params("arbitrary", "arbitrary"),
        name="fn_fold",
    )(fn, fn, fn, tb["jp"], tb["cc"], tb["sc"])
    full1 = lambda shape: pl.BlockSpec(shape, lambda b: (0,) * len(shape))
    aux = pl.pallas_call(
        _fn_aux_kernel,
        grid=(bsz,),
        in_specs=[pl.BlockSpec((1, n, BRANCH_W), lambda b: (b, 0, 0)),
                  full1((2 * SUBLANES, n)), full1((2 * SUBLANES, n)),
                  full1((BRANCH_W, BRANCH_W)), full1((BRANCH_W, BRANCH_W))],
        out_specs=pl.BlockSpec((1, 2 * SUBLANES, BRANCH_W), lambda b: (b, 0, 0)),
        out_shape=jax.ShapeDtypeStruct((bsz, 2 * SUBLANES, BRANCH_W), F32),
        compiler_params=_params("arbitrary"),
        name="fn_aux",
    )(fn, tb["tc"], tb["ts"], tb["cc"], tb["sc"])
    return pl.pallas_call(
        functools.partial(_fn_main_kernel, scale=float(1.0 / np.sqrt(n * FN_GD))),
        grid=(bsz, nh),
        in_specs=[pl.BlockSpec((blk, half), lambda b, m: (m, 0)),
                  pl.BlockSpec((blk, half), lambda b, m: (m, 0)),
                  pl.BlockSpec((1, half, 2 * BRANCH_W), lambda b, m: (b, 0, 0)),
                  pl.BlockSpec((1, 2 * SUBLANES, BRANCH_W), lambda b, m: (b, 0, 0)),
                  pl.BlockSpec((1, n, BRANCH_W), lambda b, m: (b, 0, 1)),
                  full2((blk, blk))],
        out_specs=pl.BlockSpec((1, n, BRANCH_W), lambda b, m: (b, 0, 0)),
        out_shape=jax.ShapeDtypeStruct((bsz, n, BRANCH_W), BF16),
        compiler_params=_params("arbitrary", "arbitrary"),
        name="fn_main",
    )(tb["ct"], tb["stn"], eo, aux, fn, tb["jp"])


def _prepare_weights(norm_w, w_in, w_out, rg_conv_w, rg_conv_b, rg_gate_a_w, rg_gate_a_b, rg_gate_x_w, rg_gate_x_b,
                     rg_lambda, sc_conv_w, ssd_conv_w, ssd_conv_b, ssd_dt_bias, ssd_a_log, ssd_d, ssd_norm_w,
                     final_norm_w):
    depth = w_in.shape[0]
    assert REF_DT_HI - REF_DT_LO == SSD_DT
    w = jnp.concatenate([w_in[:, :, :REF_DT_LO], w_in[:, :, REF_DT_HI:],
                         jnp.tile(w_in[:, :, REF_DT_LO:REF_DT_HI], (1, 1, LANES // SSD_DT))], axis=2).astype(BF16)
    assert w.shape[2] == C_END
    heads_per_half = RG_HALF // RG_HD
    eye = jnp.eye(heads_per_half, dtype=F32)

    def blockdiag(wg):
        wg = wg.reshape(depth, 2, -1, heads_per_half, RG_HD, RG_HD)
        return jnp.einsum("ldhjio,jk->ldhjiko", wg, eye).reshape(depth, 2, -1, RG_HALF, RG_HALF)

    halves = lambda b: b.reshape(depth, 2, -1, 1, RG_HALF)
    rep = LANES // SSD_DT
    return dict(
        norm_w=norm_w.reshape(depth, 1, D_MODEL), final_w=final_norm_w.reshape(1, D_MODEL),
        w_in=w, w_out=w_out.astype(BF16), sc_cw=sc_conv_w,
        rg_cw=rg_conv_w, rg_cb=rg_conv_b.reshape(depth, 1, BRANCH_W),
        rg_wg=(0.5 * jnp.concatenate([blockdiag(rg_gate_a_w), blockdiag(rg_gate_x_w)], axis=-1)).astype(BF16),
        rg_bg=0.5 * jnp.concatenate([halves(rg_gate_a_b), halves(rg_gate_x_b)], axis=-1),
        rg_lam=rg_lambda,
        ssd_cw=ssd_conv_w, ssd_cb=ssd_conv_b.reshape(depth, 1, SSD_XBC),
        ssd_dtb=jnp.tile(ssd_dt_bias.reshape(depth, 1, SSD_DT), (1, 1, rep)),
        ssd_alog=jnp.tile(ssd_a_log.reshape(depth, 1, SSD_DT), (1, 1, rep)),
        ssd_dskip=jnp.repeat(ssd_d, SSD_HD, axis=1).reshape(depth, 1, BRANCH_W),
        ssd_nw=ssd_norm_w.reshape(depth, 1, BRANCH_W),
    )


def _mix(x, mod, pw, l, mod_row, rowlen, tm, tt, blk, interleave, rg_h0, ssd_h0, with_output, final):
    rg, xbc, z, dt, fn, y_sc = _inproj(x, mod, pw, l, rowlen, tm, mod_row, interleave)
    y_rg, rg_st = _rg(rg, pw, l, rg_h0, tt, interleave)
    y_ssd, ssd_st = _ssd(xbc, z, dt, pw, l, ssd_h0)
    if not with_output:
        return None, rg_st, ssd_st
    y_fn = _fourier(fn, blk)
    return _outproj(y_rg, y_sc, y_fn, y_ssd, x, mod, pw, l, final, tm, mod_row, interleave), rg_st, ssd_st


def kernel(x, c, ctx, c_ctx, ada_w, ada_b, norm_w, w_in, w_out, rg_conv_w, rg_conv_b, rg_gate_a_w, rg_gate_a_b,
           rg_gate_x_w, rg_gate_x_b, rg_lambda, sc_conv_w, ssd_conv_w, ssd_conv_b, ssd_dt_bias, ssd_a_log, ssd_d,
           ssd_norm_w, final_norm_w):
    bsz, seq, _ = x.shape
    ctx_len = ctx.shape[1]
    depth = w_in.shape[0]
    assert bsz + 1 <= SUBLANES and seq % SUBLANES == 0
    pw = _prepare_weights(norm_w, w_in, w_out, rg_conv_w, rg_conv_b, rg_gate_a_w, rg_gate_a_b, rg_gate_x_w,
                          rg_gate_x_b, rg_lambda, sc_conv_w, ssd_conv_w, ssd_conv_b, ssd_dt_bias, ssd_a_log, ssd_d,
                          ssd_norm_w, final_norm_w)
    c8 = jnp.concatenate([c, c_ctx[None, :], jnp.zeros((SUBLANES - bsz - 1, D_MODEL), F32)], axis=0)
    mod = _ada(c8, ada_w, ada_b.reshape(depth, 1, 3 * D_MODEL))
    zeros_rg = jnp.zeros((bsz, SUBLANES, BRANCH_W), F32)
    zeros_ssd = jnp.zeros((bsz, 2, SSD_HEADS // 2, SSD_STATE, LANES), F32)
    tm_lat = seq // SUBLANES
    tt_lat = min(1024, seq)
    tt_ctx = min(256, ctx_len)
    blk_lat = min(512, seq // 2)
    blk_ctx = min(512, ctx_len // 2)
    for l in range(depth):
        last = l == depth - 1
        new_ctx, rg_st, ssd_st = _mix(ctx, mod, pw, l, bsz, ctx_len, ctx_len, tt_ctx, blk_ctx, False,
                                      zeros_rg, zeros_ssd, not last, False)
        x, _, _ = _mix(x, mod, pw, l, None, GRID_W, tm_lat, tt_lat, blk_lat, True, rg_st, ssd_st, True, last)
        if not last:
            ctx = new_ctx
    return x
```

```python
import functools

import numpy as np
import jax
import jax.numpy as jnp
from jax import lax
from jax.experimental import pallas as pl
from jax.experimental.pallas import tpu as pltpu

F32 = jnp.float32
BF16 = jnp.bfloat16
HIGHEST = lax.Precision.HIGHEST

D_MODEL = 1024
D_INNER = 2048
BRANCH_W = 512
GRID_W = 64
RG_HD = 64
RG_C = 8.0
RG_HALF = 256
SSD_HEADS = 8
SSD_HD = 64
SSD_GROUPS = 2
SSD_STATE = 64
SSD_XBC = BRANCH_W + 2 * SSD_GROUPS * SSD_STATE
SSD_CHUNK = 128
SSD_DT = 2 * SSD_HEADS
FN_GD = 128
NORM_EPS = 1e-6

LANES = 128
SUBLANES = 8
BF16_ROWS = 16
VMEM_LIMIT_BYTES = 60000 * 1024

C_RGX, C_XBC, C_RGG, C_Z, C_SC, C_FN, C_DT, C_END = 0, 512, 1280, 1792, 2304, 4352, 5376, 5504
REF_DT_LO, REF_DT_HI = 1280, 1296


def _sigmoid(v):
    return 0.5 + 0.5 * jnp.tanh(0.5 * v)


def _silu(v):
    h = 0.5 * v
    return h + h * jnp.tanh(h)


def _softplus(v):
    return jnp.maximum(v, 0.0) + jnp.log1p(jnp.exp(-jnp.abs(v)))


def _dot(a, b):
    return jnp.dot(a, b, preferred_element_type=F32)


def _split3(v):
    hi = v.astype(BF16).astype(F32)
    rest = v - hi
    mid = rest.astype(BF16).astype(F32)
    return hi, mid, rest - mid


def _pack3(v):
    hi, mid, lo = _split3(v)
    period = lax.broadcasted_iota(jnp.int32, v.shape, 1) // SSD_DT
    return jnp.where(period == 0, hi, jnp.where(period == 1, mid, jnp.where(period == 2, lo, 0.0))).astype(BF16)


def _params(*semantics):
    return pltpu.CompilerParams(dimension_semantics=semantics, vmem_limit_bytes=VMEM_LIMIT_BYTES)


def _layer_spec(l, shape, single_buffer=False):
    mode = dict(pipeline_mode=pl.Buffered(1)) if single_buffer else {}
    return pl.BlockSpec((None,) + tuple(shape), lambda *_: (l,) + (0,) * len(shape), **mode)


def _ada_kernel(c_ref, w_ref, b_ref, o_ref):
    o_ref[...] = jnp.dot(_silu(c_ref[...]), w_ref[...], precision=HIGHEST,
                         preferred_element_type=F32) + b_ref[...]


def _ada(c8, w, b):
    depth = w.shape[0]
    tn = 512
    return pl.pallas_call(
        _ada_kernel,
        grid=(depth, 3 * D_MODEL // tn),
        in_specs=[pl.BlockSpec((SUBLANES, D_MODEL), lambda l, j: (0, 0)),
                  pl.BlockSpec((None, D_MODEL, tn), lambda l, j: (l, 0, j)),
                  pl.BlockSpec((None, 1, tn), lambda l, j: (l, 0, j))],
        out_specs=pl.BlockSpec((None, SUBLANES, tn), lambda l, j: (l, 0, j)),
        out_shape=jax.ShapeDtypeStruct((depth, SUBLANES, 3 * D_MODEL), F32),
        compiler_params=_params("arbitrary", "arbitrary"),
        name="ada",
    )(c8, w, b)


def _inproj_kernel(x_ref, xp_ref, xn_ref, nw_ref, sh_ref, sc_ref, w_ref, cw_ref, rcw_ref, rcb_ref, scw_ref, scb_ref,
                   rg_ref, xbc_ref, z_ref, dt_ref, fn_ref, ysc_ref, *, rowlen, mod_row):
    i = pl.program_id(1)
    tm = x_ref.shape[1]
    halo = xp_ref.shape[1]
    row = pl.program_id(0) if mod_row is None else mod_row
    x = jnp.concatenate([xp_ref[0], x_ref[0], xn_ref[0]], axis=0)
    ms = jnp.mean(x * x, axis=-1, keepdims=True)
    h = (x * lax.rsqrt(ms + NORM_EPS)) * nw_ref[...]
    h = h * (1.0 + sc_ref[pl.ds(row, 1), :]) + sh_ref[pl.ds(row, 1), :]
    hx = h.astype(BF16)
    hb = hx[halo:halo + tm]
    keep_prev = jnp.where(i == 0, 0.0, 1.0)
    keep_next = jnp.where(i == pl.num_programs(1) - 1, 0.0, 1.0)

    def proj(lo, hi):
        return _dot(hb, w_ref[:, lo:hi])

    def conv4(lo, hi, cw, cb):
        p = _dot(hx, w_ref[:, lo:hi])
        p = jnp.concatenate([p[:halo] * keep_prev, p[halo:halo + tm], p[halo + tm:] * keep_next], axis=0)
        n = tm + 2 * halo
        mid = slice(halo, halo + tm)
        return (cw[0:1, :] * pltpu.roll(p, 2, 0)[mid] + cw[1:2, :] * pltpu.roll(p, 1, 0)[mid]
                + cw[2:3, :] * p[mid] + cw[3:4, :] * pltpu.roll(p, n - 1, 0)[mid] + cb)

    v = proj(C_SC + 512, C_SC + 1024) * proj(C_SC + 1024, C_SC + 1536)
    pos = lax.broadcasted_iota(jnp.int32, (tm, BRANCH_W), 0) & (rowlen - 1)
    vm1 = jnp.where(pos == 0, 0.0, pltpu.roll(v, 1, 0))
    vp1 = jnp.where(pos == rowlen - 1, 0.0, pltpu.roll(v, tm - 1, 0))
    vc = cw_ref[0:1, :] * vm1 + cw_ref[1:2, :] * v + cw_ref[2:3, :] * vp1
    ysc_ref[0] = (proj(C_SC, C_SC + 512) * vc * _silu(proj(C_SC + 1536, C_FN))).astype(BF16)
    xbc_ref[0] = _silu(conv4(C_XBC, C_RGG, scw_ref[...], scb_ref[...])).astype(BF16)
    rg_ref[0, :, 0:BRANCH_W] = conv4(C_RGX, C_XBC, rcw_ref[...], rcb_ref[...]).astype(BF16)
    rg_ref[0, :, BRANCH_W:] = proj(C_RGG, C_Z).astype(BF16)
    z_ref[0] = proj(C_Z, C_SC).astype(BF16)
    fn_ref[0] = proj(C_FN, C_DT).astype(BF16)
    dt_ref[0] = proj(C_DT, C_END)


def _inproj(x, mod, pw, l, rowlen, tm, mod_row):
    bsz, t, _ = x.shape
    halo = BF16_ROWS
    assert t % tm == 0 and tm % rowlen == 0 and rowlen & (rowlen - 1) == 0 and tm % halo == 0
    per_tile = tm // halo
    last_halo = t // halo - 1
    tok = lambda width: pl.BlockSpec((1, tm, width), lambda b, i: (b, i, 0))
    lay = lambda *shape: _layer_spec(l, shape)
    out = lambda width, dt: jax.ShapeDtypeStruct((bsz, t, width), dt)
    return pl.pallas_call(
        functools.partial(_inproj_kernel, rowlen=rowlen, mod_row=mod_row),
        grid=(bsz, t // tm),
        in_specs=[tok(D_MODEL),
                  pl.BlockSpec((1, halo, D_MODEL), lambda b, i: (b, jnp.maximum(i * per_tile - 1, 0), 0)),
                  pl.BlockSpec((1, halo, D_MODEL), lambda b, i: (b, jnp.minimum((i + 1) * per_tile, last_halo), 0)),
                  lay(1, D_MODEL),
                  pl.BlockSpec((None, SUBLANES, D_MODEL), lambda b, i: (l, 0, 0)),
                  pl.BlockSpec((None, SUBLANES, D_MODEL), lambda b, i: (l, 0, 1)),
                  _layer_spec(l, (D_MODEL, C_END), single_buffer=True),
                  lay(3, BRANCH_W), lay(4, BRANCH_W), lay(1, BRANCH_W),
                  lay(4, SSD_XBC), lay(1, SSD_XBC)],
        out_specs=[tok(1024), tok(SSD_XBC), tok(512), tok(LANES), tok(1024), tok(512)],
        out_shape=[out(1024, BF16), out(SSD_XBC, BF16), out(512, BF16), out(LANES, F32), out(1024, BF16),
                   out(512, BF16)],
        compiler_params=_params("arbitrary", "arbitrary"),
        name="inproj",
    )(x, x, x, pw["norm_w"], mod, mod, pw["w_in"], pw["sc_cw"], pw["rg_cw"], pw["rg_cb"], pw["ssd_cw"], pw["ssd_cb"])


def _outproj_kernel(yrg_ref, ysc_ref, yfn_ref, yssd_ref, w_ref, x_ref, g_ref, fw_ref, o_ref, *, final, mod_row):
    row = pl.program_id(0) if mod_row is None else mod_row
    acc = _dot(yrg_ref[0], w_ref[0:512, :])
    acc += _dot(ysc_ref[0], w_ref[512:1024, :])
    acc += _dot(yfn_ref[0], w_ref[1024:1536, :])
    acc += _dot(yssd_ref[0], w_ref[1536:2048, :])
    xn = x_ref[0] + g_ref[pl.ds(row, 1), :] * acc
    if final:
        ms = jnp.mean(xn * xn, axis=-1, keepdims=True)
        xn = (xn * lax.rsqrt(ms + NORM_EPS)) * fw_ref[...]
    o_ref[0] = xn


def _outproj(y_rg, y_sc, y_fn, y_ssd, x, mod, pw, l, final, tm, mod_row):
    bsz, t, _ = x.shape
    ytok = pl.BlockSpec((1, tm, BRANCH_W), lambda b, i: (b, i, 0))
    xtok = pl.BlockSpec((1, tm, D_MODEL), lambda b, i: (b, i, 0))
    return pl.pallas_call(
        functools.partial(_outproj_kernel, final=final, mod_row=mod_row),
        grid=(bsz, t // tm),
        in_specs=[ytok, ytok, ytok, ytok,
                  _layer_spec(l, (D_INNER, D_MODEL), single_buffer=True),
                  xtok,
                  pl.BlockSpec((None, SUBLANES, D_MODEL), lambda b, i: (l, 0, 2)),
                  pl.BlockSpec((1, D_MODEL), lambda b, i: (0, 0))],
        out_specs=xtok,
        out_shape=jax.ShapeDtypeStruct((bsz, t, D_MODEL), F32),
        compiler_params=_params("arbitrary", "arbitrary"),
        name="outproj",
    )(y_rg, y_sc, y_fn, y_ssd, pw["w_out"], x, mod, pw["final_w"])


MIN_NORMAL_F32 = float(np.finfo(np.float32).tiny)
LOG2E = float(np.log2(np.e))


def _rg_coefficients(ub, wg_half, bg_half, lam):
    uh = 0.5 * ub.astype(F32)
    th = jnp.tanh(_dot(ub, wg_half) + bg_half)
    k = (-0.5 * RG_C * np.log2(np.e)) * _softplus(-lam)
    a = jnp.exp2(k + k * th[:, :RG_HALF])
    x = 1.0 - a * a
    root = x * lax.rsqrt(jnp.maximum(x, MIN_NORMAL_F32))
    return a, root * (uh + uh * th[:, RG_HALF:])


def _scan_tile(a, v, h_in, sub, reverse):
    rows = a.shape[0]
    for s in (1, 2, 4):
        if reverse:
            keep = sub < SUBLANES - s
            a_sh = jnp.where(keep, pltpu.roll(a, rows - s, 0), 1.0)
            v_sh = jnp.where(keep, pltpu.roll(v, rows - s, 0), 0.0)
        else:
            keep = sub >= s
            a_sh = jnp.where(keep, pltpu.roll(a, s, 0), 1.0)
            v_sh = jnp.where(keep, pltpu.roll(v, s, 0), 0.0)
        v = v + a * v_sh
        a = a * a_sh
    ngroups = rows // SUBLANES
    out = [None] * ngroups
    h = h_in
    order = range(ngroups - 1, -1, -1) if reverse else range(ngroups)
    for g in order:
        lo = g * SUBLANES
        hg = v[lo:lo + SUBLANES] + a[lo:lo + SUBLANES] * h
        out[g] = hg
        h = hg[0:1] if reverse else hg[SUBLANES - 1:SUBLANES]
    return jnp.concatenate(out, axis=0), h


def _rg_kernel(u_ref, g_ref, wg_ref, bg_ref, lam_ref, h0_ref, y_ref, ht_ref, hb_ref, *, t, tt):
    nt = t // tt
    sub = lax.broadcasted_iota(jnp.int32, (tt, RG_HALF), 0) & (SUBLANES - 1)

    def coeffs(s, d):
        r0 = pl.multiple_of(s * tt, tt)
        return _rg_coefficients(u_ref[0, pl.ds(r0, tt), :], wg_ref[d], bg_ref[d], lam_ref[d:d + 1, :])

    def rev_body(i, h):
        s = nt - 1 - i
        a, v = coeffs(s, 1)
        hh, hn = _scan_tile(a, v, h, sub, True)
        hb_ref[pl.ds(pl.multiple_of(s * tt, tt), tt), :] = hh
        return hn

    h_rev = lax.fori_loop(0, nt, rev_body, h0_ref[0, 1:2, :])

    def fwd_body(s, h):
        a, v = coeffs(s, 0)
        hh, hn = _scan_tile(a, v, h, sub, False)
        r0 = pl.multiple_of(s * tt, tt)
        g = g_ref[0, pl.ds(r0, tt), :].astype(F32)
        y_ref[0, pl.ds(r0, tt), :] = ((hh + hb_ref[pl.ds(r0, tt), :]) * _silu(g)).astype(BF16)
        return hn

    h_fwd = lax.fori_loop(0, nt, fwd_body, h0_ref[0, 0:1, :])
    ht_ref[0] = jnp.concatenate([h_fwd, h_rev, jnp.zeros((SUBLANES - 2, RG_HALF), F32)], axis=0)


RG_GATHER = BF16_ROWS


def _rg_interleaved_kernel(u_ref, g_ref, wg_ref, bg_ref, lam_ref, perm_ref, permt_ref, h0_ref, y_ref,
                           s_ref, af_ref, ab_ref, up_ref, *, t, tt):
    nt = t // tt
    ng = tt // SUBLANES
    chunk = t // SUBLANES
    steps = tt // SUBLANES
    nsub = steps // RG_GATHER
    block = SUBLANES * RG_GATHER
    shape = (SUBLANES, RG_HALF)

    def source_rows(k, m, s):
        return pl.ds(pl.multiple_of(s * chunk + k * steps + m * RG_GATHER, RG_GATHER), RG_GATHER)

    def gather(ref, k):
        blocks = []
        for m in range(nsub):
            rows = jnp.concatenate([ref[0, source_rows(k, m, s), :] for s in range(SUBLANES)], axis=0)
            blocks.append(_dot(perm_ref[...], rows).astype(BF16))
        return jnp.concatenate(blocks, axis=0)

    def sweep(d):
        def body(i, carry):
            h, acc = carry
            s = i if d == 0 else nt - 1 - i
            r0 = pl.multiple_of(s * tt, tt)
            if d == 1:
                ub = gather(u_ref, s)
                up_ref[pl.ds(r0, tt), :] = ub
            else:
                ub = up_ref[pl.ds(r0, tt), :]
            a, v = _rg_coefficients(ub, wg_ref[d], bg_ref[d], lam_ref[d:d + 1, :])
            hs, ps = [None] * ng, [None] * ng
            for g in (range(ng) if d == 0 else range(ng - 1, -1, -1)):
                ag = a[g * SUBLANES:(g + 1) * SUBLANES]
                h = ag * h + v[g * SUBLANES:(g + 1) * SUBLANES]
                acc = ag * acc
                hs[g], ps[g] = h, acc
            local = jnp.concatenate(hs, axis=0)
            if d == 1:
                s_ref[pl.ds(r0, tt), :] = local
                ab_ref[pl.ds(r0, tt), :] = jnp.concatenate(ps, axis=0)
            else:
                s_ref[pl.ds(r0, tt), :] += local
                af_ref[pl.ds(r0, tt), :] = jnp.concatenate(ps, axis=0)
            return h, acc

        return lax.fori_loop(0, nt, body, (jnp.zeros(shape, F32), jnp.ones(shape, F32)))

    hb, pb = sweep(1)
    hf, pf = sweep(0)
    cf = [h0_ref[0, 0:1, :]]
    for s in range(SUBLANES - 1):
        cf.append(hf[s:s + 1] + pf[s:s + 1] * cf[s])
    cb = [None] * SUBLANES
    cb[SUBLANES - 1] = h0_ref[0, 1:2, :]
    for s in range(SUBLANES - 1, 0, -1):
        cb[s - 1] = hb[s:s + 1] + pb[s:s + 1] * cb[s]
    cf = jnp.concatenate(cf, axis=0)[None]
    cb = jnp.concatenate(cb, axis=0)[None]

    def fix(s, carry):
        r0 = pl.multiple_of(s * tt, tt)
        corr = (af_ref[pl.ds(r0, tt), :].reshape(ng, SUBLANES, RG_HALF) * cf
                + ab_ref[pl.ds(r0, tt), :].reshape(ng, SUBLANES, RG_HALF) * cb).reshape(tt, RG_HALF)
        g = gather(g_ref, s).astype(F32)
        y = ((s_ref[pl.ds(r0, tt), :] + corr) * _silu(g)).astype(BF16)
        for m in range(nsub):
            back = _dot(permt_ref[...], y[m * block:(m + 1) * block]).astype(BF16)
            for c in range(SUBLANES):
                y_ref[0, source_rows(s, m, c), :] = back[c * RG_GATHER:(c + 1) * RG_GATHER]
        return carry

    lax.fori_loop(0, nt, fix, 0)


def _rg(rg, pw, l, h0, tt, interleaved):
    bsz, t, _ = rg.shape
    assert t % tt == 0
    nhalf = BRANCH_W // RG_HALF
    seq = lambda col0: pl.BlockSpec((1, t, RG_HALF), lambda b, h: (b, 0, col0 + h))
    in_specs = [seq(0), seq(nhalf),
                pl.BlockSpec((None, 2, None, RG_HALF, 2 * RG_HALF), lambda b, h: (l, 0, h, 0, 0)),
                pl.BlockSpec((None, 2, None, 1, 2 * RG_HALF), lambda b, h: (l, 0, h, 0, 0)),
                pl.BlockSpec((None, 2, RG_HALF), lambda b, h: (l, 0, h)),
                pl.BlockSpec((1, SUBLANES, RG_HALF), lambda b, h: (b, 0, h))]
    args = (rg, rg, pw["rg_wg"], pw["rg_bg"], pw["rg_lam"], h0)
    y_spec = pl.BlockSpec((1, t, RG_HALF), lambda b, h: (b, 0, h))
    y_shape = jax.ShapeDtypeStruct((bsz, t, BRANCH_W), BF16)
    seq_scratch = pltpu.VMEM((t, RG_HALF), F32)
    if interleaved:
        block = SUBLANES * RG_GATHER
        assert tt % block == 0
        pos = np.arange(block)
        perm = np.zeros((block, block))
        perm[pos, RG_GATHER * (pos % SUBLANES) + pos // SUBLANES] = 1.0
        perm = jnp.asarray(perm, dtype=F32).astype(BF16)
        pspec = pl.BlockSpec((block, block), lambda b, h: (0, 0))
        y = pl.pallas_call(
            functools.partial(_rg_interleaved_kernel, t=t, tt=tt),
            grid=(bsz, nhalf), in_specs=in_specs[:5] + [pspec, pspec] + in_specs[5:],
            out_specs=y_spec, out_shape=y_shape,
            scratch_shapes=[seq_scratch, seq_scratch, seq_scratch, pltpu.VMEM((t, RG_HALF), BF16)],
            compiler_params=_params("arbitrary", "arbitrary"), name="rglru_interleaved",
        )(*args[:5], perm, perm.T, args[5])
        return y, None
    return pl.pallas_call(
        functools.partial(_rg_kernel, t=t, tt=tt),
        grid=(bsz, nhalf), in_specs=in_specs,
        out_specs=[y_spec, pl.BlockSpec((1, SUBLANES, RG_HALF), lambda b, h: (b, 0, h))],
        out_shape=[y_shape, jax.ShapeDtypeStruct((bsz, SUBLANES, BRANCH_W), F32)],
        scratch_shapes=[seq_scratch],
        compiler_params=_params("arbitrary", "arbitrary"), name="rglru",
    )(*args)


def _ssd_kernel(xbc_ref, z_ref, dt_ref, dtb_ref, alog_ref, dsk_ref, nw_ref, ecol_ref, ehead_ref,
                hmask_ref, h0_ref, y_ref, ht_ref, yb_ref, cs_ref, loc_ref, st_ref, *, t):
    q = SSD_CHUNK
    nc = t // q
    npairs = SSD_HEADS // 2
    pairs_per_group = npairs // SSD_GROUPS
    lane1 = lax.broadcasted_iota(jnp.int32, (1, LANES), 1)
    a_row = -LOG2E * jnp.exp(alog_ref[...])
    fwd_lane = (lax.broadcasted_iota(jnp.int32, (q, LANES), 1) & (SSD_DT - 1)) < SSD_HEADS
    ri = lax.broadcasted_iota(jnp.int32, (q, q), 0)
    ci = lax.broadcasted_iota(jnp.int32, (q, q), 1)
    lower_b = jnp.where(ri >= ci, 1.0, 0.0).astype(BF16)
    head0_s = lax.broadcasted_iota(jnp.int32, (SSD_STATE, LANES), 1) < SSD_HD

    def pair_vec(row, l0, l1):
        return jnp.where(lane1 < SSD_HD, row[:, l0:l0 + 1], row[:, l1:l1 + 1])

    def pass1(i, carry):
        c = nc - 1 - i
        r0 = pl.multiple_of(c * q, q)
        xs_b = xbc_ref[0, pl.ds(r0, q), 0:BRANCH_W]
        bm_b = xbc_ref[0, pl.ds(r0, q), BRANCH_W:BRANCH_W + LANES]
        cm_b = xbc_ref[0, pl.ds(r0, q), BRANCH_W + LANES:SSD_XBC]
        dtv = _softplus(dt_ref[0, pl.ds(r0, q), :] + dtb_ref[...])
        da = dtv * a_row
        pre = sum(_dot(lower_b, part.astype(BF16)) for part in _split3(da))
        tot = pre[q - 1:q, :]
        cs = jnp.where(fwd_lane, pre, tot - pre + da)
        wgt = dtv * jnp.exp2(tot - cs)
        etot = jnp.exp2(tot)
        cs_p = _pack3(cs)
        cs_col = _dot(cs_p, ecol_ref[...])
        ecs_b = jnp.exp2(_dot(cs_p, ehead_ref[:, BRANCH_W:]))
        adjt = (cs - LOG2E * jnp.log(dtv)).T
        diagt = (LOG2E * jnp.log(dtv + pltpu.roll(dtv, SSD_HEADS, 1))).T
        wgtt = wgt.T
        bt = bm_b.astype(F32).T
        xs_h0 = xs_b * hmask_ref[0:1, :]
        xs_h1 = xs_b * hmask_ref[1:2, :]
        ys = []
        for g in range(SSD_GROUPS):
            gs = slice(g * SSD_STATE, (g + 1) * SSD_STATE)
            cbm = lax.dot_general(cm_b[:, gs], bm_b[:, gs], (((1,), (1,)), ((), ())), preferred_element_type=F32)
            btg = bt[gs, :]
            for pp in range(pairs_per_group):
                pair = g * pairs_per_group + pp
                ps = slice(pair * LANES, (pair + 1) * LANES)
                state = st_ref[pair]
                lf0 = 2 * pair
                lb0 = SSD_HEADS + lf0
                ms = []
                for lf in (lf0, lf0 + 1):
                    lb = SSD_HEADS + lf
                    arg = jnp.where(ri > ci, cs_col[:, lf * LANES:(lf + 1) * LANES] - adjt[lf:lf + 1, :],
                                    jnp.where(ri < ci, cs_col[:, lb * LANES:(lb + 1) * LANES] - adjt[lb:lb + 1, :],
                                              diagt[lf:lf + 1, :]))
                    ms.append((cbm * jnp.exp2(arg)).astype(BF16))
                yd = _dot(jnp.concatenate(ms, axis=1), jnp.concatenate([xs_h0[:, ps], xs_h1[:, ps]], axis=0))
                yo = _dot(cm_b[:, gs], state.astype(BF16)) * ecs_b[:, ps]
                ys.append(yd + yo)
                stack = jnp.concatenate([(btg * wgtt[l:l + 1, :]).astype(BF16) for l in (lf0, lf0 + 1, lb0, lb0 + 1)],
                                        axis=0)
                res = _dot(stack, xs_b[:, ps])
                n = SSD_STATE
                loc_ref[c, pair] = jnp.where(head0_s, res[0:n], res[n:2 * n])
                st_ref[pair] = state * pair_vec(etot, lb0, lb0 + 1) + jnp.where(head0_s, res[2 * n:3 * n], res[3 * n:])
        yb_ref[pl.ds(r0, q), :] = jnp.concatenate(ys, axis=1) + dsk_ref[...] * xs_b.astype(F32)
        cs_ref[pl.ds(r0, q), :] = cs
        return carry

    def pass2(c, carry):
        r0 = pl.multiple_of(c * q, q)
        cs = cs_ref[pl.ds(r0, q), :]
        cm_b = xbc_ref[0, pl.ds(r0, q), BRANCH_W + LANES:SSD_XBC]
        ecs_f = jnp.exp2(_dot(_pack3(cs), ehead_ref[:, :BRANCH_W]))
        etot = jnp.exp2(cs[q - 1:q, :])
        ys = []
        for pair in range(npairs):
            g = pair // pairs_per_group
            state = st_ref[pair]
            lf0 = 2 * pair
            ys.append(_dot(cm_b[:, g * SSD_STATE:(g + 1) * SSD_STATE], state.astype(BF16))
                      * ecs_f[:, pair * LANES:(pair + 1) * LANES])
            st_ref[pair] = state * pair_vec(etot, lf0, lf0 + 1) + loc_ref[c, pair]
        yt = yb_ref[pl.ds(r0, q), :] + jnp.concatenate(ys, axis=1)
        yt = yt * _silu(z_ref[0, pl.ds(r0, q), :].astype(F32))
        ms = jnp.mean(yt * yt, axis=-1, keepdims=True)
        y_ref[0, pl.ds(r0, q), :] = ((yt * lax.rsqrt(ms + NORM_EPS)) * nw_ref[...]).astype(BF16)
        return carry

    for p in range(npairs):
        st_ref[p] = h0_ref[0, 1, p]
    lax.fori_loop(0, nc, pass1, 0, unroll=2)
    for p in range(npairs):
        ht_ref[0, 1, p] = st_ref[p]
        st_ref[p] = h0_ref[0, 0, p]
    lax.fori_loop(0, nc, pass2, 0, unroll=4)
    for p in range(npairs):
        ht_ref[0, 0, p] = st_ref[p]


@functools.lru_cache(maxsize=None)
def _ssd_tables():
    lanes = np.arange(LANES)[:, None]
    src = np.where(lanes < 3 * SSD_DT, lanes % SSD_DT, -1)
    ecol = (src == np.arange(SSD_DT * LANES)[None, :] // LANES)
    ehead = (src == np.arange(2 * BRANCH_W)[None, :] // SSD_HD)
    even_head = (np.arange(BRANCH_W) // SSD_HD) % 2 == 0
    hmask = np.zeros((BF16_ROWS, BRANCH_W))
    hmask[0] = even_head
    hmask[1] = ~even_head
    as_bf16 = lambda a: jnp.asarray(a, dtype=F32).astype(BF16)
    return as_bf16(ecol), as_bf16(ehead), as_bf16(hmask)


def _ssd(xbc, z, dt, pw, l, h0):
    bsz, t, _ = xbc.shape
    assert t % SSD_CHUNK == 0
    npairs = SSD_HEADS // 2
    ecol, ehead, hmask = _ssd_tables()
    full = lambda shape: pl.BlockSpec(shape, lambda b: (0,) * len(shape))
    lay = lambda *shape: _layer_spec(l, shape)
    seq = lambda width: pl.BlockSpec((1, t, width), lambda b: (b, 0, 0))
    st_spec = pl.BlockSpec((1, 2, npairs, SSD_STATE, LANES), lambda b: (b, 0, 0, 0, 0))
    return pl.pallas_call(
        functools.partial(_ssd_kernel, t=t),
        grid=(bsz,),
        in_specs=[seq(SSD_XBC), seq(BRANCH_W), seq(LANES), lay(1, LANES), lay(1, LANES),
                  lay(1, BRANCH_W), lay(1, BRANCH_W), full(ecol.shape), full(ehead.shape), full(hmask.shape),
                  st_spec],
        out_specs=[seq(BRANCH_W), st_spec],
        out_shape=[jax.ShapeDtypeStruct((bsz, t, BRANCH_W), BF16),
                   jax.ShapeDtypeStruct((bsz, 2, npairs, SSD_STATE, LANES), F32)],
        scratch_shapes=[pltpu.VMEM((t, BRANCH_W), F32), pltpu.VMEM((t, LANES), F32),
                        pltpu.VMEM((t // SSD_CHUNK, npairs, SSD_STATE, LANES), F32),
                        pltpu.VMEM((npairs, SSD_STATE, LANES), F32)],
        compiler_params=_params("arbitrary"),
        name="ssd",
    )(xbc, z, dt, pw["ssd_dtb"], pw["ssd_alog"], pw["ssd_dskip"], pw["ssd_nw"], ecol, ehead, hmask, h0)


@functools.lru_cache(maxsize=None)
def _fourier_tables(n, blk):
    half = n // 2
    nh = half // blk
    assert nh * blk == half and nh <= SUBLANES
    p = np.arange(FN_GD)
    ang_c = 2.0 * np.pi * ((p[:, None] * p[None, :]) % FN_GD) / FN_GD
    eye = np.eye(BRANCH_W // FN_GD)
    cc = np.kron(eye, np.cos(ang_c))
    sc = np.kron(eye, np.sin(ang_c))
    k = np.arange(half)
    ang_t = 2.0 * np.pi * ((k[:, None] * k[None, :]) % n) / n
    ct = np.cos(ang_t)
    stn = -np.sin(ang_t)
    jp = np.zeros((blk, blk))
    r = np.arange(1, blk)
    jp[r, blk - r] = 1.0
    tt = np.arange(n)
    ks = blk * (np.arange(nh) + 1)
    ang_k = 2.0 * np.pi * ((ks[:, None] * tt[None, :]) % n) / n
    tc = np.zeros((2 * SUBLANES, n))
    ts = np.zeros((2 * SUBLANES, n))
    tc[:nh] = np.cos(ang_k)
    ts[:nh] = np.sin(ang_k)
    tc[SUBLANES, half] = 1.0
    as_bf16 = lambda a: jnp.asarray(a, dtype=F32).astype(BF16)
    return dict(cc=as_bf16(cc), sc=as_bf16(sc), ct=as_bf16(ct), stn=as_bf16(stn), jp=as_bf16(jp),
                tc=as_bf16(tc), ts=as_bf16(ts))


def _fn_fold_kernel(xj_ref, xm_ref, xr_ref, jp_ref, cc_ref, sc_ref, eo_ref):
    j = pl.program_id(1)
    xj = xj_ref[0]
    blk = xj.shape[0]
    mir = _dot(jp_ref[...], xm_ref[0])
    row0 = xr_ref[0, 0:1, :].astype(F32) * jnp.where(j == 0, 0.0, 1.0)
    rows = lax.broadcasted_iota(jnp.int32, (blk, BRANCH_W), 0)
    mir = jnp.where(rows == 0, row0, mir).astype(BF16)
    e = _dot(xj, cc_ref[...]) + _dot(mir, cc_ref[...])
    o = _dot(xj, sc_ref[...]) - _dot(mir, sc_ref[...])
    eo_ref[0] = jnp.concatenate([e, o], axis=1).astype(BF16)


def _fn_aux_kernel(x_ref, tc_ref, ts_ref, cc_ref, sc_ref, o_ref):
    x = x_ref[0]
    xc = _dot(tc_ref[...], x).astype(BF16)
    xs = _dot(ts_ref[...], x).astype(BF16)
    o_ref[0] = _dot(xc, cc_ref[...]) + _dot(xs, sc_ref[...])


def _fn_main_kernel(ct_ref, stn_ref, eo_ref, aux_ref, g_ref, jp_ref, y_ref, *, scale):
    m = pl.program_id(1)
    blk = ct_ref.shape[0]
    nblk = y_ref.shape[1] // blk
    rows = lax.broadcasted_iota(jnp.int32, (blk, BRANCH_W), 0)
    p = _dot(ct_ref[...], eo_ref[0, :, 0:BRANCH_W])
    qn = _dot(stn_ref[...], eo_ref[0, :, BRANCH_W:2 * BRANCH_W])
    sgn = (1 - 2 * ((m * blk + rows) & 1)).astype(F32)
    p = p + sgn * aux_ref[0, SUBLANES:SUBLANES + 1, :]
    lo = pl.multiple_of(m * blk, blk)
    hi = pl.multiple_of((nblk - 1 - m) * blk, blk)
    y_ref[0, pl.ds(lo, blk), :] = ((p + qn) * (_silu(g_ref[0, pl.ds(lo, blk), :].astype(F32)) * scale)).astype(BF16)
    flipped = _dot(jp_ref[...], (p - qn).astype(BF16))
    sel = lax.broadcasted_iota(jnp.int32, (2 * SUBLANES, BRANCH_W), 0) == m
    row0 = jnp.sum(jnp.where(sel, aux_ref[0], 0.0), axis=0, keepdims=True)
    y_ref[0, pl.ds(hi, blk), :] = (jnp.where(rows == 0, row0, flipped)
                                   * (_silu(g_ref[0, pl.ds(hi, blk), :].astype(F32)) * scale)).astype(BF16)


def _fourier(fn, blk):
    bsz, n, _ = fn.shape
    tb = _fourier_tables(n, blk)
    half = n // 2
    nh = half // blk
    nblk = 2 * nh
    rows16 = n // BF16_ROWS
    full2 = lambda shape: pl.BlockSpec(shape, lambda b, j: (0,) * len(shape))
    eo = pl.pallas_call(
        _fn_fold_kernel,
        grid=(bsz, nh),
        in_specs=[pl.BlockSpec((1, blk, BRANCH_W), lambda b, j: (b, j, 0)),
                  pl.BlockSpec((1, blk, BRANCH_W), lambda b, j: (b, nblk - 1 - j, 0)),
                  pl.BlockSpec((1, BF16_ROWS, BRANCH_W),
                               lambda b, j: (b, jnp.minimum((blk // BF16_ROWS) * (nblk - j), rows16 - 1), 0)),
                  full2((blk, blk)), full2((BRANCH_W, BRANCH_W)), full2((BRANCH_W, BRANCH_W))],
        out_specs=pl.BlockSpec((1, blk, 2 * BRANCH_W), lambda b, j: (b, j, 0)),
        out_shape=jax.ShapeDtypeStruct((bsz, half, 2 * BRANCH_W), BF16),
        compiler_params=_params("arbitrary", "arbitrary"),
        name="fn_fold",
    )(fn, fn, fn, tb["jp"], tb["cc"], tb["sc"])
    full1 = lambda shape: pl.BlockSpec(shape, lambda b: (0,) * len(shape))
    aux = pl.pallas_call(
        _fn_aux_kernel,
        grid=(bsz,),
        in_specs=[pl.BlockSpec((1, n, BRANCH_W), lambda b: (b, 0, 0)),
                  full1((2 * SUBLANES, n)), full1((2 * SUBLANES, n)),
                  full1((BRANCH_W, BRANCH_W)), full1((BRANCH_W, BRANCH_W))],
        out_specs=pl.BlockSpec((1, 2 * SUBLANES, BRANCH_W), lambda b: (b, 0, 0)),
        out_shape=jax.ShapeDtypeStruct((bsz, 2 * SUBLANES, BRANCH_W), F32),
        compiler_params=_params("arbitrary"),
        name="fn_aux",
    )(fn, tb["tc"], tb["ts"], tb["cc"], tb["sc"])
    return pl.pallas_call(
        functools.partial(_fn_main_kernel, scale=float(1.0 / np.sqrt(n * FN_GD))),
        grid=(bsz, nh),
        in_specs=[pl.BlockSpec((blk, half), lambda b, m: (m, 0)),
                  pl.BlockSpec((blk, half), lambda b, m: (m, 0)),
                  pl.BlockSpec((1, half, 2 * BRANCH_W), lambda b, m: (b, 0, 0)),
                  pl.BlockSpec((1, 2 * SUBLANES, BRANCH_W), lambda b, m: (b, 0, 0)),
                  pl.BlockSpec((1, n, BRANCH_W), lambda b, m: (b, 0, 1)),
                  full2((blk, blk))],
        out_specs=pl.BlockSpec((1, n, BRANCH_W), lambda b, m: (b, 0, 0)),
        out_shape=jax.ShapeDtypeStruct((bsz, n, BRANCH_W), BF16),
        compiler_params=_params("arbitrary", "arbitrary"),
        name="fn_main",
    )(tb["ct"], tb["stn"], eo, aux, fn, tb["jp"])


def _prepare_weights(norm_w, w_in, w_out, rg_conv_w, rg_conv_b, rg_gate_a_w, rg_gate_a_b, rg_gate_x_w, rg_gate_x_b,
                     rg_lambda, sc_conv_w, ssd_conv_w, ssd_conv_b, ssd_dt_bias, ssd_a_log, ssd_d, ssd_norm_w,
                     final_norm_w):
    depth = w_in.shape[0]
    assert REF_DT_HI - REF_DT_LO == SSD_DT
    w = jnp.concatenate([w_in[:, :, :REF_DT_LO], w_in[:, :, REF_DT_HI:],
                         jnp.tile(w_in[:, :, REF_DT_LO:REF_DT_HI], (1, 1, LANES // SSD_DT))], axis=2).astype(BF16)
    assert w.shape[2] == C_END
    heads_per_half = RG_HALF // RG_HD
    eye = jnp.eye(heads_per_half, dtype=F32)

    def blockdiag(wg):
        wg = wg.reshape(depth, 2, -1, heads_per_half, RG_HD, RG_HD)
        return jnp.einsum("ldhjio,jk->ldhjiko", wg, eye).reshape(depth, 2, -1, RG_HALF, RG_HALF)

    halves = lambda b: b.reshape(depth, 2, -1, 1, RG_HALF)
    rep = LANES // SSD_DT
    return dict(
        norm_w=norm_w.reshape(depth, 1, D_MODEL), final_w=final_norm_w.reshape(1, D_MODEL),
        w_in=w, w_out=w_out.astype(BF16), sc_cw=sc_conv_w,
        rg_cw=rg_conv_w, rg_cb=rg_conv_b.reshape(depth, 1, BRANCH_W),
        rg_wg=(0.5 * jnp.concatenate([blockdiag(rg_gate_a_w), blockdiag(rg_gate_x_w)], axis=-1)).astype(BF16),
        rg_bg=0.5 * jnp.concatenate([halves(rg_gate_a_b), halves(rg_gate_x_b)], axis=-1),
        rg_lam=rg_lambda,
        ssd_cw=ssd_conv_w, ssd_cb=ssd_conv_b.reshape(depth, 1, SSD_XBC),
        ssd_dtb=jnp.tile(ssd_dt_bias.reshape(depth, 1, SSD_DT), (1, 1, rep)),
        ssd_alog=jnp.tile(ssd_a_log.reshape(depth, 1, SSD_DT), (1, 1, rep)),
        ssd_dskip=jnp.repeat(ssd_d, SSD_HD, axis=1).reshape(depth, 1, BRANCH_W),
        ssd_nw=ssd_norm_w.reshape(depth, 1, BRANCH_W),
    )


def _mix(x, mod, pw, l, mod_row, rowlen, tm, tt, blk, interleave, rg_h0, ssd_h0, with_output, final):
    rg, xbc, z, dt, fn, y_sc = _inproj(x, mod, pw, l, rowlen, tm, mod_row)
    y_rg, rg_st = _rg(rg, pw, l, rg_h0, tt, interleave)
    y_ssd, ssd_st = _ssd(xbc, z, dt, pw, l, ssd_h0)
    if not with_output:
        return None, rg_st, ssd_st
    y_fn = _fourier(fn, blk)
    return _outproj(y_rg, y_sc, y_fn, y_ssd, x, mod, pw, l, final, tm, mod_row), rg_st, ssd_st


def kernel(x, c, ctx, c_ctx, ada_w, ada_b, norm_w, w_in, w_out, rg_conv_w, rg_conv_b, rg_gate_a_w, rg_gate_a_b,
           rg_gate_x_w, rg_gate_x_b, rg_lambda, sc_conv_w, ssd_conv_w, ssd_conv_b, ssd_dt_bias, ssd_a_log, ssd_d,
           ssd_norm_w, final_norm_w):
    bsz, seq, _ = x.shape
    ctx_len = ctx.shape[1]
    depth = w_in.shape[0]
    assert bsz + 1 <= SUBLANES and seq % SUBLANES == 0
    pw = _prepare_weights(norm_w, w_in, w_out, rg_conv_w, rg_conv_b, rg_gate_a_w, rg_gate_a_b, rg_gate_x_w,
                          rg_gate_x_b, rg_lambda, sc_conv_w, ssd_conv_w, ssd_conv_b, ssd_dt_bias, ssd_a_log, ssd_d,
                          ssd_norm_w, final_norm_w)
    c8 = jnp.concatenate([c, c_ctx[None, :], jnp.zeros((SUBLANES - bsz - 1, D_MODEL), F32)], axis=0)
    mod = _ada(c8, ada_w, ada_b.reshape(depth, 1, 3 * D_MODEL))
    zeros_rg = jnp.zeros((bsz, SUBLANES, BRANCH_W), F32)
    zeros_ssd = jnp.zeros((bsz, 2, SSD_HEADS // 2, SSD_STATE, LANES), F32)
    tm_lat = min(1024, seq)
    tt_lat = min(1024, seq)
    tt_ctx = min(256, ctx_len)
    blk_lat = min(512, seq // 2)
    blk_ctx = min(512, ctx_len // 2)
    for l in range(depth):
        last = l == depth - 1
        new_ctx, rg_st, ssd_st = _mix(ctx, mod, pw, l, bsz, ctx_len, ctx_len, tt_ctx, blk_ctx, False,
                                      zeros_rg, zeros_ssd, not last, False)
        x, _, _ = _mix(x, mod, pw, l, None, GRID_W, tm_lat, tt_lat, blk_lat, True, rg_st, ssd_st, True, last)
        if not last:
            ctx = new_ctx
    return x
```

```python
import functools

import numpy as np
import jax
import jax.numpy as jnp
from jax import lax
from jax.experimental import pallas as pl
from jax.experimental.pallas import tpu as pltpu

F32 = jnp.float32
BF16 = jnp.bfloat16
HIGHEST = lax.Precision.HIGHEST

D_MODEL = 1024
D_INNER = 2048
BRANCH_W = 512
GRID_W = 64
RG_HD = 64
RG_C = 8.0
RG_HALF = 256
SSD_HEADS = 8
SSD_HD = 64
SSD_GROUPS = 2
SSD_STATE = 64
SSD_XBC = BRANCH_W + 2 * SSD_GROUPS * SSD_STATE
SSD_CHUNK = 128
SSD_DT = 2 * SSD_HEADS
FN_GD = 128
FN_FLIP = 128
NORM_EPS = 1e-6

LANES = 128
SUBLANES = 8
BF16_ROWS = 16
VMEM_LIMIT_BYTES = 60000 * 1024

C_RGX, C_XBC, C_RGG, C_Z, C_SC, C_FN, C_DT, C_END = 0, 512, 1280, 1792, 2304, 4352, 5376, 5504
REF_DT_LO, REF_DT_HI = 1280, 1296


def _sigmoid(v):
    return 0.5 + 0.5 * jnp.tanh(0.5 * v)


def _silu(v):
    h = 0.5 * v
    return h + h * jnp.tanh(h)


def _softplus(v):
    return jnp.maximum(v, 0.0) + jnp.log1p(jnp.exp(-jnp.abs(v)))


def _dot(a, b):
    return jnp.dot(a, b, preferred_element_type=F32)


def _split3(v):
    hi = v.astype(BF16).astype(F32)
    rest = v - hi
    mid = rest.astype(BF16).astype(F32)
    return hi, mid, rest - mid


def _pack3(v):
    hi, mid, lo = _split3(v)
    period = lax.broadcasted_iota(jnp.int32, v.shape, 1) // SSD_DT
    return jnp.where(period == 0, hi, jnp.where(period == 1, mid, jnp.where(period == 2, lo, 0.0))).astype(BF16)


def _params(*semantics):
    return pltpu.CompilerParams(dimension_semantics=semantics, vmem_limit_bytes=VMEM_LIMIT_BYTES)


def _layer_spec(l, shape, single_buffer=False):
    mode = dict(pipeline_mode=pl.Buffered(1)) if single_buffer else {}
    return pl.BlockSpec((None,) + tuple(shape), lambda *_: (l,) + (0,) * len(shape), **mode)


def _ada_kernel(c_ref, w_ref, b_ref, o_ref):
    o_ref[...] = jnp.dot(_silu(c_ref[...]), w_ref[...], precision=HIGHEST,
                         preferred_element_type=F32) + b_ref[...]


def _ada(c8, w, b):
    depth = w.shape[0]
    tn = 512
    return pl.pallas_call(
        _ada_kernel,
        grid=(depth, 3 * D_MODEL // tn),
        in_specs=[pl.BlockSpec((SUBLANES, D_MODEL), lambda l, j: (0, 0)),
                  pl.BlockSpec((None, D_MODEL, tn), lambda l, j: (l, 0, j)),
                  pl.BlockSpec((None, 1, tn), lambda l, j: (l, 0, j))],
        out_specs=pl.BlockSpec((None, SUBLANES, tn), lambda l, j: (l, 0, j)),
        out_shape=jax.ShapeDtypeStruct((depth, SUBLANES, 3 * D_MODEL), F32),
        compiler_params=_params("arbitrary", "arbitrary"),
        name="ada",
    )(c8, w, b)


def _inproj_kernel(*refs, rowlen, mod_row, halo, scan_only):
    if halo:
        x_ref, xp_ref, xn_ref, *refs = refs
    else:
        x_ref, *refs = refs
    nw_ref, sh_ref, sc_ref, w_ref, cw_ref, rcw_ref, rcb_ref, scw_ref, scb_ref, *outs = refs
    nb, tm, _ = x_ref.shape
    rows = nb * tm
    i = pl.program_id(1)
    row = pl.program_id(0) if mod_row is None else mod_row
    if halo:
        x = jnp.concatenate([xp_ref[0], x_ref[0], xn_ref[0]], axis=0)
    else:
        x = x_ref[...].reshape(rows, D_MODEL)
    ms = jnp.mean(x * x, axis=-1, keepdims=True)
    h = (x * lax.rsqrt(ms + NORM_EPS)) * nw_ref[...]
    h = h * (1.0 + sc_ref[pl.ds(row, 1), :]) + sh_ref[pl.ds(row, 1), :]
    hx = h.astype(BF16)
    hb = hx[halo:halo + rows]

    def proj(lo, hi):
        return _dot(hb, w_ref[:, lo:hi])

    def conv4(lo, hi, cw, cb):
        p = _dot(hx, w_ref[:, lo:hi])
        n = p.shape[0]
        if halo:
            keep_prev = jnp.where(i == 0, 0.0, 1.0)
            keep_next = jnp.where(i == pl.num_programs(1) - 1, 0.0, 1.0)
            p = jnp.concatenate([p[:halo] * keep_prev, p[halo:halo + rows], p[halo + rows:] * keep_next], axis=0)
            mid = slice(halo, halo + rows)
            taps = [pltpu.roll(p, 2, 0)[mid], pltpu.roll(p, 1, 0)[mid], p[mid], pltpu.roll(p, n - 1, 0)[mid]]
        else:
            pos = lax.broadcasted_iota(jnp.int32, p.shape, 0) & (tm - 1)
            taps = [jnp.where(pos >= 2, pltpu.roll(p, 2, 0), 0.0), jnp.where(pos >= 1, pltpu.roll(p, 1, 0), 0.0), p,
                    jnp.where(pos <= tm - 2, pltpu.roll(p, n - 1, 0), 0.0)]
        return sum(cw[k:k + 1, :] * taps[k] for k in range(4)) + cb

    def put(ref, value, lo=0):
        ref[:, :, lo:lo + value.shape[1]] = value.reshape(nb, tm, value.shape[1]).astype(ref.dtype)

    if scan_only:
        rg_ref, xbc_ref, z_ref, dt_ref = outs
    else:
        rg_ref, xbc_ref, z_ref, dt_ref, fn_ref, ysc_ref = outs
        v = proj(C_SC + 512, C_SC + 1024) * proj(C_SC + 1024, C_SC + 1536)
        pos = lax.broadcasted_iota(jnp.int32, (rows, BRANCH_W), 0) & (rowlen - 1)
        vm1 = jnp.where(pos == 0, 0.0, pltpu.roll(v, 1, 0))
        vp1 = jnp.where(pos == rowlen - 1, 0.0, pltpu.roll(v, rows - 1, 0))
        vc = cw_ref[0:1, :] * vm1 + cw_ref[1:2, :] * v + cw_ref[2:3, :] * vp1
        put(ysc_ref, proj(C_SC, C_SC + 512) * vc * _silu(proj(C_SC + 1536, C_FN)))
    put(xbc_ref, _silu(conv4(C_XBC, C_RGG, scw_ref[...], scb_ref[...])))
    put(rg_ref, conv4(C_RGX, C_XBC, rcw_ref[...], rcb_ref[...]))
    put(rg_ref, proj(C_RGG, C_Z), BRANCH_W)
    put(z_ref, proj(C_Z, C_SC))
    if not scan_only:
        put(fn_ref, proj(C_FN, C_DT))
    put(dt_ref, proj(C_DT, C_END))


def _inproj(x, mod, pw, l, rowlen, tm, mod_row, scan_only=False):
    bsz, t, _ = x.shape
    whole = tm == t and mod_row is not None and tm & (tm - 1) == 0
    halo = 0 if whole else BF16_ROWS
    nb = bsz if whole else 1
    assert t % tm == 0 and tm % rowlen == 0 and rowlen & (rowlen - 1) == 0 and tm % BF16_ROWS == 0
    per_tile = tm // BF16_ROWS
    last_halo = t // BF16_ROWS - 1
    tok = lambda width: pl.BlockSpec((nb, tm, width), lambda b, i: (b, i, 0))
    lay = lambda *shape: _layer_spec(l, shape)
    out = lambda width, dt: jax.ShapeDtypeStruct((bsz, t, width), dt)
    x_specs, xs = [tok(D_MODEL)], [x]
    if halo:
        x_specs += [pl.BlockSpec((1, halo, D_MODEL), lambda b, i: (b, jnp.maximum(i * per_tile - 1, 0), 0)),
                    pl.BlockSpec((1, halo, D_MODEL), lambda b, i: (b, jnp.minimum((i + 1) * per_tile, last_halo), 0))]
        xs += [x, x]
    widths = [(1024, BF16), (SSD_XBC, BF16), (512, BF16), (LANES, F32)]
    if not scan_only:
        widths += [(1024, BF16), (512, BF16)]
    return pl.pallas_call(
        functools.partial(_inproj_kernel, rowlen=rowlen, mod_row=mod_row, halo=halo, scan_only=scan_only),
        grid=(bsz // nb, t // tm),
        in_specs=x_specs + [lay(1, D_MODEL),
                            pl.BlockSpec((None, SUBLANES, D_MODEL), lambda b, i: (l, 0, 0)),
                            pl.BlockSpec((None, SUBLANES, D_MODEL), lambda b, i: (l, 0, 1)),
                            _layer_spec(l, (D_MODEL, C_END), single_buffer=True),
                            lay(3, BRANCH_W), lay(4, BRANCH_W), lay(1, BRANCH_W),
                            lay(4, SSD_XBC), lay(1, SSD_XBC)],
        out_specs=[tok(w) for w, _ in widths],
        out_shape=[out(w, dt) for w, dt in widths],
        compiler_params=_params("arbitrary", "arbitrary"),
        name="inproj",
    )(*xs, pw["norm_w"], mod, mod, pw["w_in"], pw["sc_cw"], pw["rg_cw"], pw["rg_cb"], pw["ssd_cw"], pw["ssd_cb"])


def _outproj_kernel(yrg_ref, ysc_ref, yfn_ref, yssd_ref, w_ref, x_ref, g_ref, fw_ref, o_ref, *, final, mod_row):
    row = pl.program_id(0) if mod_row is None else mod_row
    nb, tm, _ = x_ref.shape
    flat = lambda ref: ref[...].reshape(nb * tm, ref.shape[2])
    acc = _dot(flat(yrg_ref), w_ref[0:512, :])
    acc += _dot(flat(ysc_ref), w_ref[512:1024, :])
    acc += _dot(flat(yfn_ref), w_ref[1024:1536, :])
    acc += _dot(flat(yssd_ref), w_ref[1536:2048, :])
    xn = flat(x_ref) + g_ref[pl.ds(row, 1), :] * acc
    if final:
        ms = jnp.mean(xn * xn, axis=-1, keepdims=True)
        xn = (xn * lax.rsqrt(ms + NORM_EPS)) * fw_ref[...]
    o_ref[...] = xn.reshape(nb, tm, D_MODEL)


def _outproj(y_rg, y_sc, y_fn, y_ssd, x, mod, pw, l, final, tm, mod_row):
    bsz, t, _ = x.shape
    nb = bsz if (tm == t and mod_row is not None) else 1
    ytok = pl.BlockSpec((nb, tm, BRANCH_W), lambda b, i: (b, i, 0))
    xtok = pl.BlockSpec((nb, tm, D_MODEL), lambda b, i: (b, i, 0))
    return pl.pallas_call(
        functools.partial(_outproj_kernel, final=final, mod_row=mod_row),
        grid=(bsz // nb, t // tm),
        in_specs=[ytok, ytok, ytok, ytok,
                  _layer_spec(l, (D_INNER, D_MODEL), single_buffer=True),
                  xtok,
                  pl.BlockSpec((None, SUBLANES, D_MODEL), lambda b, i: (l, 0, 2)),
                  pl.BlockSpec((1, D_MODEL), lambda b, i: (0, 0))],
        out_specs=xtok,
        out_shape=jax.ShapeDtypeStruct((bsz, t, D_MODEL), F32),
        compiler_params=_params("arbitrary", "arbitrary"),
        name="outproj",
    )(y_rg, y_sc, y_fn, y_ssd, pw["w_out"], x, mod, pw["final_w"])


MIN_NORMAL_F32 = float(np.finfo(np.float32).tiny)
LOG2E = float(np.log2(np.e))


def _rg_coefficients(ub, wg_half, bg_half, lam):
    uh = 0.5 * ub.astype(F32)
    th = jnp.tanh(_dot(ub, wg_half) + bg_half)
    k = (-0.5 * RG_C * np.log2(np.e)) * _softplus(-lam)
    a = jnp.exp2(k + k * th[:, :RG_HALF])
    x = 1.0 - a * a
    root = x * lax.rsqrt(jnp.maximum(x, MIN_NORMAL_F32))
    return a, root * (uh + uh * th[:, RG_HALF:])


def _scan_tile(a, v, h_in, sub, reverse):
    rows = a.shape[0]
    for s in (1, 2, 4):
        if reverse:
            keep = sub < SUBLANES - s
            a_sh = jnp.where(keep, pltpu.roll(a, rows - s, 0), 1.0)
            v_sh = jnp.where(keep, pltpu.roll(v, rows - s, 0), 0.0)
        else:
            keep = sub >= s
            a_sh = jnp.where(keep, pltpu.roll(a, s, 0), 1.0)
            v_sh = jnp.where(keep, pltpu.roll(v, s, 0), 0.0)
        v = v + a * v_sh
        a = a * a_sh
    ngroups = rows // SUBLANES
    out = [None] * ngroups
    h = h_in
    order = range(ngroups - 1, -1, -1) if reverse else range(ngroups)
    for g in order:
        lo = g * SUBLANES
        hg = v[lo:lo + SUBLANES] + a[lo:lo + SUBLANES] * h
        out[g] = hg
        h = hg[0:1] if reverse else hg[SUBLANES - 1:SUBLANES]
    return jnp.concatenate(out, axis=0), h


def _rg_kernel(u_ref, g_ref, wg_ref, bg_ref, lam_ref, h0_ref, y_ref, ht_ref, hb_ref, *, t, tt):
    nt = t // tt
    sub = lax.broadcasted_iota(jnp.int32, (tt, RG_HALF), 0) & (SUBLANES - 1)

    def coeffs(s, d):
        r0 = pl.multiple_of(s * tt, tt)
        return _rg_coefficients(u_ref[0, pl.ds(r0, tt), :], wg_ref[d], bg_ref[d], lam_ref[d:d + 1, :])

    def rev_body(i, h):
        s = nt - 1 - i
        a, v = coeffs(s, 1)
        hh, hn = _scan_tile(a, v, h, sub, True)
        hb_ref[pl.ds(pl.multiple_of(s * tt, tt), tt), :] = hh
        return hn

    h_rev = lax.fori_loop(0, nt, rev_body, h0_ref[0, 1:2, :])

    def fwd_body(s, h):
        a, v = coeffs(s, 0)
        hh, hn = _scan_tile(a, v, h, sub, False)
        r0 = pl.multiple_of(s * tt, tt)
        g = g_ref[0, pl.ds(r0, tt), :].astype(F32)
        y_ref[0, pl.ds(r0, tt), :] = ((hh + hb_ref[pl.ds(r0, tt), :]) * _silu(g)).astype(BF16)
        return hn

    h_fwd = lax.fori_loop(0, nt, fwd_body, h0_ref[0, 0:1, :])
    ht_ref[0] = jnp.concatenate([h_fwd, h_rev, jnp.zeros((SUBLANES - 2, RG_HALF), F32)], axis=0)


RG_GATHER = BF16_ROWS


def _rg_interleaved_kernel(u_ref, g_ref, wg_ref, bg_ref, lam_ref, perm_ref, permt_ref, h0_ref, y_ref,
                           s_ref, af_ref, ab_ref, up_ref, *, t, tt):
    nt = t // tt
    ng = tt // SUBLANES
    chunk = t // SUBLANES
    steps = tt // SUBLANES
    nsub = steps // RG_GATHER
    block = SUBLANES * RG_GATHER
    shape = (SUBLANES, RG_HALF)

    def source_rows(k, m, s):
        return pl.ds(pl.multiple_of(s * chunk + k * steps + m * RG_GATHER, RG_GATHER), RG_GATHER)

    def gather(ref, k):
        blocks = []
        for m in range(nsub):
            rows = jnp.concatenate([ref[0, source_rows(k, m, s), :] for s in range(SUBLANES)], axis=0)
            blocks.append(_dot(perm_ref[...], rows).astype(BF16))
        return jnp.concatenate(blocks, axis=0)

    def sweep(d):
        def body(i, carry):
            h, acc = carry
            s = i if d == 0 else nt - 1 - i
            r0 = pl.multiple_of(s * tt, tt)
            if d == 1:
                ub = gather(u_ref, s)
                up_ref[pl.ds(r0, tt), :] = ub
            else:
                ub = up_ref[pl.ds(r0, tt), :]
            a, v = _rg_coefficients(ub, wg_ref[d], bg_ref[d], lam_ref[d:d + 1, :])
            hs, ps = [None] * ng, [None] * ng
            for g in (range(ng) if d == 0 else range(ng - 1, -1, -1)):
                ag = a[g * SUBLANES:(g + 1) * SUBLANES]
                h = ag * h + v[g * SUBLANES:(g + 1) * SUBLANES]
                acc = ag * acc
                hs[g], ps[g] = h, acc
            local = jnp.concatenate(hs, axis=0)
            if d == 1:
                s_ref[pl.ds(r0, tt), :] = local
                ab_ref[pl.ds(r0, tt), :] = jnp.concatenate(ps, axis=0)
            else:
                s_ref[pl.ds(r0, tt), :] += local
                af_ref[pl.ds(r0, tt), :] = jnp.concatenate(ps, axis=0)
            return h, acc

        return lax.fori_loop(0, nt, body, (jnp.zeros(shape, F32), jnp.ones(shape, F32)))

    hb, pb = sweep(1)
    hf, pf = sweep(0)
    cf = [h0_ref[0, 0:1, :]]
    for s in range(SUBLANES - 1):
        cf.append(hf[s:s + 1] + pf[s:s + 1] * cf[s])
    cb = [None] * SUBLANES
    cb[SUBLANES - 1] = h0_ref[0, 1:2, :]
    for s in range(SUBLANES - 1, 0, -1):
        cb[s - 1] = hb[s:s + 1] + pb[s:s + 1] * cb[s]
    cf = jnp.concatenate(cf, axis=0)[None]
    cb = jnp.concatenate(cb, axis=0)[None]

    def fix(s, carry):
        r0 = pl.multiple_of(s * tt, tt)
        corr = (af_ref[pl.ds(r0, tt), :].reshape(ng, SUBLANES, RG_HALF) * cf
                + ab_ref[pl.ds(r0, tt), :].reshape(ng, SUBLANES, RG_HALF) * cb).reshape(tt, RG_HALF)
        g = gather(g_ref, s).astype(F32)
        y = ((s_ref[pl.ds(r0, tt), :] + corr) * _silu(g)).astype(BF16)
        for m in range(nsub):
            back = _dot(permt_ref[...], y[m * block:(m + 1) * block]).astype(BF16)
            for c in range(SUBLANES):
                y_ref[0, source_rows(s, m, c), :] = back[c * RG_GATHER:(c + 1) * RG_GATHER]
        return carry

    lax.fori_loop(0, nt, fix, 0)


def _rg(rg, pw, l, h0, tt, interleaved):
    bsz, t, _ = rg.shape
    assert t % tt == 0
    nhalf = BRANCH_W // RG_HALF
    seq = lambda col0: pl.BlockSpec((1, t, RG_HALF), lambda b, h: (b, 0, col0 + h))
    in_specs = [seq(0), seq(nhalf),
                pl.BlockSpec((None, 2, None, RG_HALF, 2 * RG_HALF), lambda b, h: (l, 0, h, 0, 0)),
                pl.BlockSpec((None, 2, None, 1, 2 * RG_HALF), lambda b, h: (l, 0, h, 0, 0)),
                pl.BlockSpec((None, 2, RG_HALF), lambda b, h: (l, 0, h)),
                pl.BlockSpec((1, SUBLANES, RG_HALF), lambda b, h: (b, 0, h))]
    args = (rg, rg, pw["rg_wg"], pw["rg_bg"], pw["rg_lam"], h0)
    y_spec = pl.BlockSpec((1, t, RG_HALF), lambda b, h: (b, 0, h))
    y_shape = jax.ShapeDtypeStruct((bsz, t, BRANCH_W), BF16)
    seq_scratch = pltpu.VMEM((t, RG_HALF), F32)
    if interleaved:
        block = SUBLANES * RG_GATHER
        assert tt % block == 0
        pos = np.arange(block)
        perm = np.zeros((block, block))
        perm[pos, RG_GATHER * (pos % SUBLANES) + pos // SUBLANES] = 1.0
        perm = jnp.asarray(perm, dtype=F32).astype(BF16)
        pspec = pl.BlockSpec((block, block), lambda b, h: (0, 0))
        y = pl.pallas_call(
            functools.partial(_rg_interleaved_kernel, t=t, tt=tt),
            grid=(bsz, nhalf), in_specs=in_specs[:5] + [pspec, pspec] + in_specs[5:],
            out_specs=y_spec, out_shape=y_shape,
            scratch_shapes=[seq_scratch, seq_scratch, seq_scratch, pltpu.VMEM((t, RG_HALF), BF16)],
            compiler_params=_params("arbitrary", "arbitrary"), name="rglru_interleaved",
        )(*args[:5], perm, perm.T, args[5])
        return y, None
    return pl.pallas_call(
        functools.partial(_rg_kernel, t=t, tt=tt),
        grid=(bsz, nhalf), in_specs=in_specs,
        out_specs=[y_spec, pl.BlockSpec((1, SUBLANES, RG_HALF), lambda b, h: (b, 0, h))],
        out_shape=[y_shape, jax.ShapeDtypeStruct((bsz, SUBLANES, BRANCH_W), F32)],
        scratch_shapes=[seq_scratch],
        compiler_params=_params("arbitrary", "arbitrary"), name="rglru",
    )(*args)


def _ssd_kernel(xbc_ref, z_ref, dt_ref, dtb_ref, alog_ref, dsk_ref, nw_ref, ecol_ref, ehead_ref,
                hmask_ref, h0_ref, y_ref, ht_ref, yb_ref, cs_ref, loc_ref, st_ref, *, t):
    q = SSD_CHUNK
    nc = t // q
    npairs = SSD_HEADS // 2
    pairs_per_group = npairs // SSD_GROUPS
    lane1 = lax.broadcasted_iota(jnp.int32, (1, LANES), 1)
    a_row = -LOG2E * jnp.exp(alog_ref[...])
    fwd_lane = (lax.broadcasted_iota(jnp.int32, (q, LANES), 1) & (SSD_DT - 1)) < SSD_HEADS
    ri = lax.broadcasted_iota(jnp.int32, (q, q), 0)
    ci = lax.broadcasted_iota(jnp.int32, (q, q), 1)
    lower_b = jnp.where(ri >= ci, 1.0, 0.0).astype(BF16)
    head0_s = lax.broadcasted_iota(jnp.int32, (SSD_STATE, LANES), 1) < SSD_HD

    def pair_vec(row, l0, l1):
        return jnp.where(lane1 < SSD_HD, row[:, l0:l0 + 1], row[:, l1:l1 + 1])

    def pass1(i, carry):
        c = nc - 1 - i
        r0 = pl.multiple_of(c * q, q)
        xs_b = xbc_ref[0, pl.ds(r0, q), 0:BRANCH_W]
        bm_b = xbc_ref[0, pl.ds(r0, q), BRANCH_W:BRANCH_W + LANES]
        cm_b = xbc_ref[0, pl.ds(r0, q), BRANCH_W + LANES:SSD_XBC]
        dtv = _softplus(dt_ref[0, pl.ds(r0, q), :] + dtb_ref[...])
        da = dtv * a_row
        pre = sum(_dot(lower_b, part.astype(BF16)) for part in _split3(da))
        tot = pre[q - 1:q, :]
        cs = jnp.where(fwd_lane, pre, tot - pre + da)
        wgt = dtv * jnp.exp2(tot - cs)
        etot = jnp.exp2(tot)
        cs_p = _pack3(cs)
        cs_col = _dot(cs_p, ecol_ref[...])
        ecs_b = jnp.exp2(_dot(cs_p, ehead_ref[:, BRANCH_W:]))
        adjt = (cs - LOG2E * jnp.log(dtv)).T
        diagt = (LOG2E * jnp.log(dtv + pltpu.roll(dtv, SSD_HEADS, 1))).T
        wgtt = wgt.T
        bt = bm_b.astype(F32).T
        xs_h0 = xs_b * hmask_ref[0:1, :]
        xs_h1 = xs_b * hmask_ref[1:2, :]
        ys = []
        for g in range(SSD_GROUPS):
            gs = slice(g * SSD_STATE, (g + 1) * SSD_STATE)
            cbm = lax.dot_general(cm_b[:, gs], bm_b[:, gs], (((1,), (1,)), ((), ())), preferred_element_type=F32)
            btg = bt[gs, :]
            for pp in range(pairs_per_group):
                pair = g * pairs_per_group + pp
                ps = slice(pair * LANES, (pair + 1) * LANES)
                state = st_ref[pair]
                lf0 = 2 * pair
                lb0 = SSD_HEADS + lf0
                ms = []
                for lf in (lf0, lf0 + 1):
                    lb = SSD_HEADS + lf
                    arg = jnp.where(ri > ci, cs_col[:, lf * LANES:(lf + 1) * LANES] - adjt[lf:lf + 1, :],
                                    jnp.where(ri < ci, cs_col[:, lb * LANES:(lb + 1) * LANES] - adjt[lb:lb + 1, :],
                                              diagt[lf:lf + 1, :]))
                    ms.append((cbm * jnp.exp2(arg)).astype(BF16))
                yd = _dot(jnp.concatenate(ms, axis=1), jnp.concatenate([xs_h0[:, ps], xs_h1[:, ps]], axis=0))
                yo = _dot(cm_b[:, gs], state.astype(BF16)) * ecs_b[:, ps]
                ys.append(yd + yo)
                stack = jnp.concatenate([(btg * wgtt[l:l + 1, :]).astype(BF16) for l in (lf0, lf0 + 1, lb0, lb0 + 1)],
                                        axis=0)
                res = _dot(stack, xs_b[:, ps])
                n = SSD_STATE
                loc_ref[c, pair] = jnp.where(head0_s, res[0:n], res[n:2 * n])
                st_ref[pair] = state * pair_vec(etot, lb0, lb0 + 1) + jnp.where(head0_s, res[2 * n:3 * n], res[3 * n:])
        yb_ref[pl.ds(r0, q), :] = jnp.concatenate(ys, axis=1) + dsk_ref[...] * xs_b.astype(F32)
        cs_ref[pl.ds(r0, q), :] = cs
        return carry

    def pass2(c, carry):
        r0 = pl.multiple_of(c * q, q)
        cs = cs_ref[pl.ds(r0, q), :]
        cm_b = xbc_ref[0, pl.ds(r0, q), BRANCH_W + LANES:SSD_XBC]
        ecs_f = jnp.exp2(_dot(_pack3(cs), ehead_ref[:, :BRANCH_W]))
        etot = jnp.exp2(cs[q - 1:q, :])
        ys = []
        for pair in range(npairs):
            g = pair // pairs_per_group
            state = st_ref[pair]
            lf0 = 2 * pair
            ys.append(_dot(cm_b[:, g * SSD_STATE:(g + 1) * SSD_STATE], state.astype(BF16))
                      * ecs_f[:, pair * LANES:(pair + 1) * LANES])
            st_ref[pair] = state * pair_vec(etot, lf0, lf0 + 1) + loc_ref[c, pair]
        yt = yb_ref[pl.ds(r0, q), :] + jnp.concatenate(ys, axis=1)
        yt = yt * _silu(z_ref[0, pl.ds(r0, q), :].astype(F32))
        ms = jnp.mean(yt * yt, axis=-1, keepdims=True)
        y_ref[0, pl.ds(r0, q), :] = ((yt * lax.rsqrt(ms + NORM_EPS)) * nw_ref[...]).astype(BF16)
        return carry

    for p in range(npairs):
        st_ref[p] = h0_ref[0, 1, p]
    lax.fori_loop(0, nc, pass1, 0, unroll=2)
    for p in range(npairs):
        ht_ref[0, 1, p] = st_ref[p]
        st_ref[p] = h0_ref[0, 0, p]
    lax.fori_loop(0, nc, pass2, 0, unroll=4)
    for p in range(npairs):
        ht_ref[0, 0, p] = st_ref[p]


@functools.lru_cache(maxsize=None)
def _ssd_tables():
    lanes = np.arange(LANES)[:, None]
    src = np.where(lanes < 3 * SSD_DT, lanes % SSD_DT, -1)
    ecol = (src == np.arange(SSD_DT * LANES)[None, :] // LANES)
    ehead = (src == np.arange(2 * BRANCH_W)[None, :] // SSD_HD)
    even_head = (np.arange(BRANCH_W) // SSD_HD) % 2 == 0
    hmask = np.zeros((BF16_ROWS, BRANCH_W))
    hmask[0] = even_head
    hmask[1] = ~even_head
    as_bf16 = lambda a: jnp.asarray(a, dtype=F32).astype(BF16)
    return as_bf16(ecol), as_bf16(ehead), as_bf16(hmask)


def _ssd(xbc, z, dt, pw, l, h0):
    bsz, t, _ = xbc.shape
    assert t % SSD_CHUNK == 0
    npairs = SSD_HEADS // 2
    ecol, ehead, hmask = _ssd_tables()
    full = lambda shape: pl.BlockSpec(shape, lambda b: (0,) * len(shape))
    lay = lambda *shape: _layer_spec(l, shape)
    seq = lambda width: pl.BlockSpec((1, t, width), lambda b: (b, 0, 0))
    st_spec = pl.BlockSpec((1, 2, npairs, SSD_STATE, LANES), lambda b: (b, 0, 0, 0, 0))
    return pl.pallas_call(
        functools.partial(_ssd_kernel, t=t),
        grid=(bsz,),
        in_specs=[seq(SSD_XBC), seq(BRANCH_W), seq(LANES), lay(1, LANES), lay(1, LANES),
                  lay(1, BRANCH_W), lay(1, BRANCH_W), full(ecol.shape), full(ehead.shape), full(hmask.shape),
                  st_spec],
        out_specs=[seq(BRANCH_W), st_spec],
        out_shape=[jax.ShapeDtypeStruct((bsz, t, BRANCH_W), BF16),
                   jax.ShapeDtypeStruct((bsz, 2, npairs, SSD_STATE, LANES), F32)],
        scratch_shapes=[pltpu.VMEM((t, BRANCH_W), F32), pltpu.VMEM((t, LANES), F32),
                        pltpu.VMEM((t // SSD_CHUNK, npairs, SSD_STATE, LANES), F32),
                        pltpu.VMEM((npairs, SSD_STATE, LANES), F32)],
        compiler_params=_params("arbitrary"),
        name="ssd",
    )(xbc, z, dt, pw["ssd_dtb"], pw["ssd_alog"], pw["ssd_dskip"], pw["ssd_nw"], ecol, ehead, hmask, h0)


@functools.lru_cache(maxsize=None)
def _fourier_tables(n, blk):
    half = n // 2
    nh = half // blk
    assert nh * blk == half and nh <= SUBLANES
    p = np.arange(FN_GD)
    ang_c = 2.0 * np.pi * ((p[:, None] * p[None, :]) % FN_GD) / FN_GD
    eye = np.eye(BRANCH_W // FN_GD)
    cc = np.kron(eye, np.cos(ang_c))
    sc = np.kron(eye, np.sin(ang_c))
    k = np.arange(half)
    ang_t = 2.0 * np.pi * ((k[:, None] * k[None, :]) % n) / n
    ct = np.cos(ang_t)
    stn = -np.sin(ang_t)
    assert blk % FN_FLIP == 0
    jp = np.zeros((FN_FLIP, FN_FLIP))
    r = np.arange(1, FN_FLIP)
    jp[r, FN_FLIP - r] = 1.0
    tt = np.arange(n)
    ks = blk * (np.arange(nh) + 1)
    ang_k = 2.0 * np.pi * ((ks[:, None] * tt[None, :]) % n) / n
    tc = np.zeros((2 * SUBLANES, n))
    ts = np.zeros((2 * SUBLANES, n))
    tc[:nh] = np.cos(ang_k)
    ts[:nh] = np.sin(ang_k)
    tc[SUBLANES, half] = 1.0
    as_bf16 = lambda a: jnp.asarray(a, dtype=F32).astype(BF16)
    return dict(cc=as_bf16(cc), sc=as_bf16(sc), ct=as_bf16(ct), stn=as_bf16(stn), jp=as_bf16(jp),
                tc=as_bf16(tc), ts=as_bf16(ts))


def _mirror(jp_ref, src, row0):
    n = src.shape[0]
    nsub = n // FN_FLIP
    first = lax.broadcasted_iota(jnp.int32, (FN_FLIP, src.shape[1]), 0) == 0
    out = []
    for a in range(nsub):
        s = nsub - 1 - a
        head = row0 if a == 0 else src[(s + 1) * FN_FLIP:(s + 1) * FN_FLIP + 1].astype(F32)
        out.append(jnp.where(first, head, _dot(jp_ref[...], src[s * FN_FLIP:(s + 1) * FN_FLIP])))
    return jnp.concatenate(out, axis=0)


def _fn_fold_kernel(xj_ref, xm_ref, xr_ref, jp_ref, cc_ref, sc_ref, eo_ref):
    j = pl.program_id(1)
    xj = xj_ref[0]
    row0 = xr_ref[0, 0:1, :].astype(F32) * jnp.where(j == 0, 0.0, 1.0)
    mir = _mirror(jp_ref, xm_ref[0], row0).astype(BF16)
    e = _dot(xj, cc_ref[...]) + _dot(mir, cc_ref[...])
    o = _dot(xj, sc_ref[...]) - _dot(mir, sc_ref[...])
    eo_ref[0] = jnp.concatenate([e, o], axis=1).astype(BF16)


def _fn_aux_kernel(x_ref, tc_ref, ts_ref, cc_ref, sc_ref, o_ref):
    x = x_ref[0]
    xc = _dot(tc_ref[...], x).astype(BF16)
    xs = _dot(ts_ref[...], x).astype(BF16)
    o_ref[0] = _dot(xc, cc_ref[...]) + _dot(xs, sc_ref[...])


def _fn_main_kernel(ct_ref, stn_ref, eo_ref, aux_ref, g_ref, jp_ref, y_ref, *, scale):
    m = pl.program_id(1)
    blk = ct_ref.shape[0]
    nblk = y_ref.shape[1] // blk
    rows = lax.broadcasted_iota(jnp.int32, (blk, BRANCH_W), 0)
    p = _dot(ct_ref[...], eo_ref[0, :, 0:BRANCH_W])
    qn = _dot(stn_ref[...], eo_ref[0, :, BRANCH_W:2 * BRANCH_W])
    sgn = (1 - 2 * ((m * blk + rows) & 1)).astype(F32)
    p = p + sgn * aux_ref[0, SUBLANES:SUBLANES + 1, :]
    lo = pl.multiple_of(m * blk, blk)
    hi = pl.multiple_of((nblk - 1 - m) * blk, blk)
    y_ref[0, pl.ds(lo, blk), :] = ((p + qn) * (_silu(g_ref[0, pl.ds(lo, blk), :].astype(F32)) * scale)).astype(BF16)
    sel = lax.broadcasted_iota(jnp.int32, (2 * SUBLANES, BRANCH_W), 0) == m
    row0 = jnp.sum(jnp.where(sel, aux_ref[0], 0.0), axis=0, keepdims=True)
    y_ref[0, pl.ds(hi, blk), :] = (_mirror(jp_ref, (p - qn).astype(BF16), row0)
                                   * (_silu(g_ref[0, pl.ds(hi, blk), :].astype(F32)) * scale)).astype(BF16)


def _fourier(fn, blk):
    bsz, n, _ = fn.shape
    tb = _fourier_tables(n, blk)
    half = n // 2
    nh = half // blk
    nblk = 2 * nh
    rows16 = n // BF16_ROWS
    full2 = lambda shape: pl.BlockSpec(shape, lambda b, j: (0,) * len(shape))
    eo = pl.pallas_call(
        _fn_fold_kernel,
        grid=(bsz, nh),
        in_specs=[pl.BlockSpec((1, blk, BRANCH_W), lambda b, j: (b, j, 0)),
                  pl.BlockSpec((1, blk, BRANCH_W), lambda b, j: (b, nblk - 1 - j, 0)),
                  pl.BlockSpec((1, BF16_ROWS, BRANCH_W),
                               lambda b, j: (b, jnp.minimum((blk // BF16_ROWS) * (nblk - j), rows16 - 1), 0)),
                  full2((FN_FLIP, FN_FLIP)), full2((BRANCH_W, BRANCH_W)), full2((BRANCH_W, BRANCH_W))],
        out_specs=pl.BlockSpec((1, blk, 2 * BRANCH_W), lambda b, j: (b, j, 0)),
        out_shape=jax.ShapeDtypeStruct((bsz, half, 2 * BRANCH_W), BF16),
        compiler_params=_params("arbitrary", "arbitrary"),
        name="fn_fold",
    )(fn, fn, fn, tb["jp"], tb["cc"], tb["sc"])
    full1 = lambda shape: pl.BlockSpec(shape, lambda b: (0,) * len(shape))
    aux = pl.pallas_call(
        _fn_aux_kernel,
        grid=(bsz,),
        in_specs=[pl.BlockSpec((1, n, BRANCH_W), lambda b: (b, 0, 0)),
                  full1((2 * SUBLANES, n)), full1((2 * SUBLANES, n)),
                  full1((BRANCH_W, BRANCH_W)), full1((BRANCH_W, BRANCH_W))],
        out_specs=pl.BlockSpec((1, 2 * SUBLANES, BRANCH_W), lambda b: (b, 0, 0)),
        out_shape=jax.ShapeDtypeStruct((bsz, 2 * SUBLANES, BRANCH_W), F32),
        compiler_params=_params("arbitrary"),
        name="fn_aux",
    )(fn, tb["tc"], tb["ts"], tb["cc"], tb["sc"])
    return pl.pallas_call(
        functools.partial(_fn_main_kernel, scale=float(1.0 / np.sqrt(n * FN_GD))),
        grid=(bsz, nh),
        in_specs=[pl.BlockSpec((blk, half), lambda b, m: (m, 0)),
                  pl.BlockSpec((blk, half), lambda b, m: (m, 0)),
                  pl.BlockSpec((1, half, 2 * BRANCH_W), lambda b, m: (b, 0, 0)),
                  pl.BlockSpec((1, 2 * SUBLANES, BRANCH_W), lambda b, m: (b, 0, 0)),
                  pl.BlockSpec((1, n, BRANCH_W), lambda b, m: (b, 0, 1)),
                  full2((FN_FLIP, FN_FLIP))],
        out_specs=pl.BlockSpec((1, n, BRANCH_W), lambda b, m: (b, 0, 0)),
        out_shape=jax.ShapeDtypeStruct((bsz, n, BRANCH_W), BF16),
        compiler_params=_params("arbitrary", "arbitrary"),
        name="fn_main",
    )(tb["ct"], tb["stn"], eo, aux, fn, tb["jp"])


def _w_in_kernel(w_ref, rep_ref, o_ref):
    w = w_ref[...]
    o_ref[:, 0:REF_DT_LO] = w[:, 0:REF_DT_LO].astype(BF16)
    o_ref[:, REF_DT_LO:C_DT] = w[:, REF_DT_HI:].astype(BF16)
    o_ref[:, C_DT:C_END] = _dot(w[:, REF_DT_LO:REF_DT_LO + LANES].astype(BF16), rep_ref[...]).astype(BF16)


def _relayout_w_in(w_in):
    depth, rows, cols = w_in.shape
    assert REF_DT_HI - REF_DT_LO == SSD_DT and cols - SSD_DT + LANES == C_END
    tr = 128
    lane = np.arange(LANES)
    rep = jnp.asarray(lane[:, None] == lane[None, :] % SSD_DT, dtype=F32).astype(BF16)
    return pl.pallas_call(
        _w_in_kernel,
        grid=(depth, rows // tr),
        in_specs=[pl.BlockSpec((None, tr, cols), lambda l, i: (l, i, 0)),
                  pl.BlockSpec((LANES, LANES), lambda l, i: (0, 0))],
        out_specs=pl.BlockSpec((None, tr, C_END), lambda l, i: (l, i, 0)),
        out_shape=jax.ShapeDtypeStruct((depth, rows, C_END), BF16),
        compiler_params=_params("arbitrary", "arbitrary"),
        name="w_in_layout",
    )(w_in, rep)

def _prepare_weights(norm_w, w_in, w_out, rg_conv_w, rg_conv_b, rg_gate_a_w, rg_gate_a_b, rg_gate_x_w, rg_gate_x_b,
                     rg_lambda, sc_conv_w, ssd_conv_w, ssd_conv_b, ssd_dt_bias, ssd_a_log, ssd_d, ssd_norm_w,
                     final_norm_w):
    depth = w_in.shape[0]
    w = _relayout_w_in(w_in)
    heads_per_half = RG_HALF // RG_HD
    eye = jnp.eye(heads_per_half, dtype=F32)

    def blockdiag(wg):
        wg = wg.reshape(depth, 2, -1, heads_per_half, RG_HD, RG_HD)
        return jnp.einsum("ldhjio,jk->ldhjiko", wg, eye).reshape(depth, 2, -1, RG_HALF, RG_HALF)

    halves = lambda b: b.reshape(depth, 2, -1, 1, RG_HALF)
    rep = LANES // SSD_DT
    return dict(
        norm_w=norm_w.reshape(depth, 1, D_MODEL), final_w=final_norm_w.reshape(1, D_MODEL),
        w_in=w, w_out=w_out.astype(BF16), sc_cw=sc_conv_w,
        rg_cw=rg_conv_w, rg_cb=rg_conv_b.reshape(depth, 1, BRANCH_W),
        rg_wg=(0.5 * jnp.concatenate([blockdiag(rg_gate_a_w), blockdiag(rg_gate_x_w)], axis=-1)).astype(BF16),
        rg_bg=0.5 * jnp.concatenate([halves(rg_gate_a_b), halves(rg_gate_x_b)], axis=-1),
        rg_lam=rg_lambda,
        ssd_cw=ssd_conv_w, ssd_cb=ssd_conv_b.reshape(depth, 1, SSD_XBC),
        ssd_dtb=jnp.tile(ssd_dt_bias.reshape(depth, 1, SSD_DT), (1, 1, rep)),
        ssd_alog=jnp.tile(ssd_a_log.reshape(depth, 1, SSD_DT), (1, 1, rep)),
        ssd_dskip=jnp.repeat(ssd_d, SSD_HD, axis=1).reshape(depth, 1, BRANCH_W),
        ssd_nw=ssd_norm_w.reshape(depth, 1, BRANCH_W),
    )


def _mix(x, mod, pw, l, mod_row, rowlen, tm, tt, blk, interleave, rg_h0, ssd_h0, with_output, final):
    rg, xbc, z, dt, *rest = _inproj(x, mod, pw, l, rowlen, tm, mod_row, scan_only=not with_output)
    y_rg, rg_st = _rg(rg, pw, l, rg_h0, tt, interleave)
    y_ssd, ssd_st = _ssd(xbc, z, dt, pw, l, ssd_h0)
    if not with_output:
        return None, rg_st, ssd_st
    fn, y_sc = rest
    y_fn = _fourier(fn, blk)
    return _outproj(y_rg, y_sc, y_fn, y_ssd, x, mod, pw, l, final, tm, mod_row), rg_st, ssd_st


def kernel(x, c, ctx, c_ctx, ada_w, ada_b, norm_w, w_in, w_out, rg_conv_w, rg_conv_b, rg_gate_a_w, rg_gate_a_b,
           rg_gate_x_w, rg_gate_x_b, rg_lambda, sc_conv_w, ssd_conv_w, ssd_conv_b, ssd_dt_bias, ssd_a_log, ssd_d,
           ssd_norm_w, final_norm_w):
    bsz, seq, _ = x.shape
    ctx_len = ctx.shape[1]
    depth = w_in.shape[0]
    assert bsz + 1 <= SUBLANES and seq % SUBLANES == 0
    pw = _prepare_weights(norm_w, w_in, w_out, rg_conv_w, rg_conv_b, rg_gate_a_w, rg_gate_a_b, rg_gate_x_w,
                          rg_gate_x_b, rg_lambda, sc_conv_w, ssd_conv_w, ssd_conv_b, ssd_dt_bias, ssd_a_log, ssd_d,
                          ssd_norm_w, final_norm_w)
    c8 = jnp.concatenate([c, c_ctx[None, :], jnp.zeros((SUBLANES - bsz - 1, D_MODEL), F32)], axis=0)
    mod = _ada(c8, ada_w, ada_b.reshape(depth, 1, 3 * D_MODEL))
    zeros_rg = jnp.zeros((bsz, SUBLANES, BRANCH_W), F32)
    zeros_ssd = jnp.zeros((bsz, 2, SSD_HEADS // 2, SSD_STATE, LANES), F32)
    tm_lat = min(1024, seq)
    tt_lat = min(1024, seq)
    tt_ctx = min(256, ctx_len)
    blk_lat = min(512, seq // 2)
    blk_ctx = min(512, ctx_len // 2)
    for l in range(depth):
        last = l == depth - 1
        new_ctx, rg_st, ssd_st = _mix(ctx, mod, pw, l, bsz, ctx_len, ctx_len, tt_ctx, blk_ctx, False,
                                      zeros_rg, zeros_ssd, not last, False)
        x, _, _ = _mix(x, mod, pw, l, None, GRID_W, tm_lat, tt_lat, blk_lat, True, rg_st, ssd_st, True, last)
        if not last:
            ctx = new_ctx
    return x
```

```python
import functools

import numpy as np
import jax
import jax.numpy as jnp
from jax import lax
from jax.experimental import pallas as pl
from jax.experimental.pallas import tpu as pltpu

F32 = jnp.float32
BF16 = jnp.bfloat16
HIGHEST = lax.Precision.HIGHEST

D_MODEL = 1024
D_INNER = 2048
BRANCH_W = 512
GRID_W = 64
RG_HD = 64
RG_C = 8.0
RG_HALF = 256
SSD_HEADS = 8
SSD_HD = 64
SSD_GROUPS = 2
SSD_STATE = 64
SSD_XBC = BRANCH_W + 2 * SSD_GROUPS * SSD_STATE
SSD_CHUNK = 128
SSD_DT = 2 * SSD_HEADS
FN_GD = 128
FN_FLIP = 128
NORM_EPS = 1e-6

LANES = 128
SUBLANES = 8
BF16_ROWS = 16
VMEM_LIMIT_BYTES = 60000 * 1024

C_RGX, C_XBC, C_RGG, C_Z, C_SC, C_FN, C_DT, C_END = 0, 512, 1280, 1792, 2304, 4352, 5376, 5504
REF_DT_LO, REF_DT_HI = 1280, 1296


def _sigmoid(v):
    return 0.5 + 0.5 * jnp.tanh(0.5 * v)


def _silu(v):
    h = 0.5 * v
    return h + h * jnp.tanh(h)


def _softplus(v):
    return jnp.maximum(v, 0.0) + jnp.log1p(jnp.exp(-jnp.abs(v)))


def _dot(a, b):
    return jnp.dot(a, b, preferred_element_type=F32)


def _split3(v):
    hi = v.astype(BF16).astype(F32)
    rest = v - hi
    mid = rest.astype(BF16).astype(F32)
    return hi, mid, rest - mid


def _pack3(v):
    hi, mid, lo = _split3(v)
    period = lax.broadcasted_iota(jnp.int32, v.shape, 1) // SSD_DT
    return jnp.where(period == 0, hi, jnp.where(period == 1, mid, jnp.where(period == 2, lo, 0.0))).astype(BF16)


def _params(*semantics):
    return pltpu.CompilerParams(dimension_semantics=semantics, vmem_limit_bytes=VMEM_LIMIT_BYTES)


def _layer_spec(l, shape, single_buffer=False):
    mode = dict(pipeline_mode=pl.Buffered(1)) if single_buffer else {}
    return pl.BlockSpec((None,) + tuple(shape), lambda *_: (l,) + (0,) * len(shape), **mode)


def _ada_kernel(c_ref, w_ref, b_ref, o_ref):
    o_ref[...] = jnp.dot(_silu(c_ref[...]), w_ref[...], precision=HIGHEST,
                         preferred_element_type=F32) + b_ref[...]


def _ada(c8, w, b):
    depth = w.shape[0]
    tn = 512
    return pl.pallas_call(
        _ada_kernel,
        grid=(depth, 3 * D_MODEL // tn),
        in_specs=[pl.BlockSpec((SUBLANES, D_MODEL), lambda l, j: (0, 0)),
                  pl.BlockSpec((None, D_MODEL, tn), lambda l, j: (l, 0, j)),
                  pl.BlockSpec((None, 1, tn), lambda l, j: (l, 0, j))],
        out_specs=pl.BlockSpec((None, SUBLANES, tn), lambda l, j: (l, 0, j)),
        out_shape=jax.ShapeDtypeStruct((depth, SUBLANES, 3 * D_MODEL), F32),
        compiler_params=_params("arbitrary", "arbitrary"),
        name="ada",
    )(c8, w, b)


def _inproj_kernel(*refs, rowlen, mod_row, halo, scan_only):
    if halo:
        x_ref, xp_ref, xn_ref, *refs = refs
    else:
        x_ref, *refs = refs
    nw_ref, sh_ref, sc_ref, wa_ref, wb_ref, cw_ref, rcw_ref, rcb_ref, scw_ref, scb_ref, *outs = refs
    nb, tm, _ = x_ref.shape
    rows = nb * tm
    i = pl.program_id(1)
    row = pl.program_id(0) if mod_row is None else mod_row
    if halo:
        x = jnp.concatenate([xp_ref[0], x_ref[0], xn_ref[0]], axis=0)
    else:
        x = x_ref[...].reshape(rows, D_MODEL)
    ms = jnp.mean(x * x, axis=-1, keepdims=True)
    h = (x * lax.rsqrt(ms + NORM_EPS)) * nw_ref[...]
    h = h * (1.0 + sc_ref[pl.ds(row, 1), :]) + sh_ref[pl.ds(row, 1), :]
    hx = h.astype(BF16)
    hb = hx[halo:halo + rows]

    def wcols(lo, hi):
        if hi <= REF_DT_LO:
            return wa_ref[:, lo:hi]
        if lo >= C_DT:
            return wa_ref[:, lo - C_DT + REF_DT_LO:hi - C_DT + REF_DT_LO]
        assert lo >= REF_DT_LO and hi <= C_DT
        return wb_ref[:, lo - REF_DT_LO:hi - REF_DT_LO]

    def proj(lo, hi):
        return _dot(hb, wcols(lo, hi))

    def conv4(lo, hi, cw, cb):
        p = _dot(hx, wcols(lo, hi))
        n = p.shape[0]
        if halo:
            keep_prev = jnp.where(i == 0, 0.0, 1.0)
            keep_next = jnp.where(i == pl.num_programs(1) - 1, 0.0, 1.0)
            p = jnp.concatenate([p[:halo] * keep_prev, p[halo:halo + rows], p[halo + rows:] * keep_next], axis=0)
            mid = slice(halo, halo + rows)
            taps = [pltpu.roll(p, 2, 0)[mid], pltpu.roll(p, 1, 0)[mid], p[mid], pltpu.roll(p, n - 1, 0)[mid]]
        else:
            pos = lax.broadcasted_iota(jnp.int32, p.shape, 0) & (tm - 1)
            taps = [jnp.where(pos >= 2, pltpu.roll(p, 2, 0), 0.0), jnp.where(pos >= 1, pltpu.roll(p, 1, 0), 0.0), p,
                    jnp.where(pos <= tm - 2, pltpu.roll(p, n - 1, 0), 0.0)]
        return sum(cw[k:k + 1, :] * taps[k] for k in range(4)) + cb

    def put(ref, value, lo=0):
        ref[:, :, lo:lo + value.shape[1]] = value.reshape(nb, tm, value.shape[1]).astype(ref.dtype)

    if scan_only:
        rg_ref, xbc_ref, z_ref, dt_ref = outs
    else:
        rg_ref, xbc_ref, z_ref, dt_ref, fn_ref, ysc_ref = outs
        v = proj(C_SC + 512, C_SC + 1024) * proj(C_SC + 1024, C_SC + 1536)
        pos = lax.broadcasted_iota(jnp.int32, (rows, BRANCH_W), 0) & (rowlen - 1)
        vm1 = jnp.where(pos == 0, 0.0, pltpu.roll(v, 1, 0))
        vp1 = jnp.where(pos == rowlen - 1, 0.0, pltpu.roll(v, rows - 1, 0))
        vc = cw_ref[0:1, :] * vm1 + cw_ref[1:2, :] * v + cw_ref[2:3, :] * vp1
        put(ysc_ref, proj(C_SC, C_SC + 512) * vc * _silu(proj(C_SC + 1536, C_FN)))
    put(xbc_ref, _silu(conv4(C_XBC, C_RGG, scw_ref[...], scb_ref[...])))
    put(rg_ref, conv4(C_RGX, C_XBC, rcw_ref[...], rcb_ref[...]))
    put(rg_ref, proj(C_RGG, C_Z), BRANCH_W)
    put(z_ref, proj(C_Z, C_SC))
    if not scan_only:
        put(fn_ref, proj(C_FN, C_DT))
    put(dt_ref, proj(C_DT, C_END))


def _inproj(x, mod, pw, l, rowlen, tm, mod_row, scan_only=False):
    bsz, t, _ = x.shape
    whole = tm == t and mod_row is not None and tm & (tm - 1) == 0
    halo = 0 if whole else BF16_ROWS
    nb = bsz if whole else 1
    assert t % tm == 0 and tm % rowlen == 0 and rowlen & (rowlen - 1) == 0 and tm % BF16_ROWS == 0
    per_tile = tm // BF16_ROWS
    last_halo = t // BF16_ROWS - 1
    tok = lambda width: pl.BlockSpec((nb, tm, width), lambda b, i: (b, i, 0))
    lay = lambda *shape: _layer_spec(l, shape)
    out = lambda width, dt: jax.ShapeDtypeStruct((bsz, t, width), dt)
    x_specs, xs = [tok(D_MODEL)], [x]
    if halo:
        x_specs += [pl.BlockSpec((1, halo, D_MODEL), lambda b, i: (b, jnp.maximum(i * per_tile - 1, 0), 0)),
                    pl.BlockSpec((1, halo, D_MODEL), lambda b, i: (b, jnp.minimum((i + 1) * per_tile, last_halo), 0))]
        xs += [x, x]
    widths = [(1024, BF16), (SSD_XBC, BF16), (512, BF16), (LANES, F32)]
    if not scan_only:
        widths += [(1024, BF16), (512, BF16)]
    return pl.pallas_call(
        functools.partial(_inproj_kernel, rowlen=rowlen, mod_row=mod_row, halo=halo, scan_only=scan_only),
        grid=(bsz // nb, t // tm),
        in_specs=x_specs + [lay(1, D_MODEL),
                            pl.BlockSpec((None, SUBLANES, D_MODEL), lambda b, i: (l, 0, 0)),
                            pl.BlockSpec((None, SUBLANES, D_MODEL), lambda b, i: (l, 0, 1)),
                            _layer_spec(l, pw["w_in_a"].shape[1:], single_buffer=True),
                            _layer_spec(l, pw["w_in_b"].shape[1:], single_buffer=True),
                            lay(3, BRANCH_W), lay(4, BRANCH_W), lay(1, BRANCH_W),
                            lay(4, SSD_XBC), lay(1, SSD_XBC)],
        out_specs=[tok(w) for w, _ in widths],
        out_shape=[out(w, dt) for w, dt in widths],
        compiler_params=_params("arbitrary", "arbitrary"),
        name="inproj",
    )(*xs, pw["norm_w"], mod, mod, pw["w_in_a"], pw["w_in_b"], pw["sc_cw"], pw["rg_cw"], pw["rg_cb"], pw["ssd_cw"],
      pw["ssd_cb"])


def _outproj_kernel(yrg_ref, ysc_ref, yfn_ref, yssd_ref, w_ref, x_ref, g_ref, fw_ref, o_ref, *, final, mod_row):
    row = pl.program_id(0) if mod_row is None else mod_row
    nb, tm, _ = x_ref.shape
    flat = lambda ref: ref[...].reshape(nb * tm, ref.shape[2])
    acc = _dot(flat(yrg_ref), w_ref[0:512, :])
    acc += _dot(flat(ysc_ref), w_ref[512:1024, :])
    acc += _dot(flat(yfn_ref), w_ref[1024:1536, :])
    acc += _dot(flat(yssd_ref), w_ref[1536:2048, :])
    xn = flat(x_ref) + g_ref[pl.ds(row, 1), :] * acc
    if final:
        ms = jnp.mean(xn * xn, axis=-1, keepdims=True)
        xn = (xn * lax.rsqrt(ms + NORM_EPS)) * fw_ref[...]
    o_ref[...] = xn.reshape(nb, tm, D_MODEL)


def _outproj(y_rg, y_sc, y_fn, y_ssd, x, mod, pw, l, final, tm, mod_row):
    bsz, t, _ = x.shape
    nb = bsz if (tm == t and mod_row is not None) else 1
    ytok = pl.BlockSpec((nb, tm, BRANCH_W), lambda b, i: (b, i, 0))
    xtok = pl.BlockSpec((nb, tm, D_MODEL), lambda b, i: (b, i, 0))
    return pl.pallas_call(
        functools.partial(_outproj_kernel, final=final, mod_row=mod_row),
        grid=(bsz // nb, t // tm),
        in_specs=[ytok, ytok, ytok, ytok,
                  _layer_spec(l, (D_INNER, D_MODEL), single_buffer=True),
                  xtok,
                  pl.BlockSpec((None, SUBLANES, D_MODEL), lambda b, i: (l, 0, 2)),
                  pl.BlockSpec((1, D_MODEL), lambda b, i: (0, 0))],
        out_specs=xtok,
        out_shape=jax.ShapeDtypeStruct((bsz, t, D_MODEL), F32),
        compiler_params=_params("arbitrary", "arbitrary"),
        name="outproj",
    )(y_rg, y_sc, y_fn, y_ssd, pw["w_out"], x, mod, pw["final_w"])


MIN_NORMAL_F32 = float(np.finfo(np.float32).tiny)
LOG2E = float(np.log2(np.e))


def _rg_coefficients(ub, wg_half, bg_half, lam):
    uh = 0.5 * ub.astype(F32)
    th = jnp.tanh(_dot(ub, wg_half) + bg_half)
    k = (-0.5 * RG_C * np.log2(np.e)) * _softplus(-lam)
    a = jnp.exp2(k + k * th[:, :RG_HALF])
    x = 1.0 - a * a
    root = x * lax.rsqrt(jnp.maximum(x, MIN_NORMAL_F32))
    return a, root * (uh + uh * th[:, RG_HALF:])


def _scan_tile(a, v, h_in, sub, reverse):
    rows = a.shape[0]
    for s in (1, 2, 4):
        if reverse:
            keep = sub < SUBLANES - s
            a_sh = jnp.where(keep, pltpu.roll(a, rows - s, 0), 1.0)
            v_sh = jnp.where(keep, pltpu.roll(v, rows - s, 0), 0.0)
        else:
            keep = sub >= s
            a_sh = jnp.where(keep, pltpu.roll(a, s, 0), 1.0)
            v_sh = jnp.where(keep, pltpu.roll(v, s, 0), 0.0)
        v = v + a * v_sh
        a = a * a_sh
    ngroups = rows // SUBLANES
    out = [None] * ngroups
    h = h_in
    order = range(ngroups - 1, -1, -1) if reverse else range(ngroups)
    for g in order:
        lo = g * SUBLANES
        hg = v[lo:lo + SUBLANES] + a[lo:lo + SUBLANES] * h
        out[g] = hg
        h = hg[0:1] if reverse else hg[SUBLANES - 1:SUBLANES]
    return jnp.concatenate(out, axis=0), h


def _rg_kernel(u_ref, g_ref, wg_ref, bg_ref, lam_ref, h0_ref, y_ref, ht_ref, hb_ref, *, t, tt):
    nt = t // tt
    sub = lax.broadcasted_iota(jnp.int32, (tt, RG_HALF), 0) & (SUBLANES - 1)

    def coeffs(s, d):
        r0 = pl.multiple_of(s * tt, tt)
        return _rg_coefficients(u_ref[0, pl.ds(r0, tt), :], wg_ref[d], bg_ref[d], lam_ref[d:d + 1, :])

    def rev_body(i, h):
        s = nt - 1 - i
        a, v = coeffs(s, 1)
        hh, hn = _scan_tile(a, v, h, sub, True)
        hb_ref[pl.ds(pl.multiple_of(s * tt, tt), tt), :] = hh
        return hn

    h_rev = lax.fori_loop(0, nt, rev_body, h0_ref[0, 1:2, :])

    def fwd_body(s, h):
        a, v = coeffs(s, 0)
        hh, hn = _scan_tile(a, v, h, sub, False)
        r0 = pl.multiple_of(s * tt, tt)
        g = g_ref[0, pl.ds(r0, tt), :].astype(F32)
        y_ref[0, pl.ds(r0, tt), :] = ((hh + hb_ref[pl.ds(r0, tt), :]) * _silu(g)).astype(BF16)
        return hn

    h_fwd = lax.fori_loop(0, nt, fwd_body, h0_ref[0, 0:1, :])
    ht_ref[0] = jnp.concatenate([h_fwd, h_rev, jnp.zeros((SUBLANES - 2, RG_HALF), F32)], axis=0)


RG_GATHER = BF16_ROWS


def _rg_interleaved_kernel(u_ref, g_ref, wg_ref, bg_ref, lam_ref, perm_ref, permt_ref, h0_ref, y_ref,
                           s_ref, af_ref, ab_ref, up_ref, *, t, tt):
    nt = t // tt
    ng = tt // SUBLANES
    chunk = t // SUBLANES
    steps = tt // SUBLANES
    nsub = steps // RG_GATHER
    block = SUBLANES * RG_GATHER
    shape = (SUBLANES, RG_HALF)

    def source_rows(k, m, s):
        return pl.ds(pl.multiple_of(s * chunk + k * steps + m * RG_GATHER, RG_GATHER), RG_GATHER)

    def gather(ref, k):
        blocks = []
        for m in range(nsub):
            rows = jnp.concatenate([ref[0, source_rows(k, m, s), :] for s in range(SUBLANES)], axis=0)
            blocks.append(_dot(perm_ref[...], rows).astype(BF16))
        return jnp.concatenate(blocks, axis=0)

    def sweep(d):
        def body(i, carry):
            h, acc = carry
            s = i if d == 0 else nt - 1 - i
            r0 = pl.multiple_of(s * tt, tt)
            if d == 1:
                ub = gather(u_ref, s)
                up_ref[pl.ds(r0, tt), :] = ub
            else:
                ub = up_ref[pl.ds(r0, tt), :]
            a, v = _rg_coefficients(ub, wg_ref[d], bg_ref[d], lam_ref[d:d + 1, :])
            hs, ps = [None] * ng, [None] * ng
            for g in (range(ng) if d == 0 else range(ng - 1, -1, -1)):
                ag = a[g * SUBLANES:(g + 1) * SUBLANES]
                h = ag * h + v[g * SUBLANES:(g + 1) * SUBLANES]
                acc = ag * acc
                hs[g], ps[g] = h, acc
            local = jnp.concatenate(hs, axis=0)
            if d == 1:
                s_ref[pl.ds(r0, tt), :] = local
                ab_ref[pl.ds(r0, tt), :] = jnp.concatenate(ps, axis=0)
            else:
                s_ref[pl.ds(r0, tt), :] += local
                af_ref[pl.ds(r0, tt), :] = jnp.concatenate(ps, axis=0)
            return h, acc

        return lax.fori_loop(0, nt, body, (jnp.zeros(shape, F32), jnp.ones(shape, F32)))

    hb, pb = sweep(1)
    hf, pf = sweep(0)
    cf = [h0_ref[0, 0:1, :]]
    for s in range(SUBLANES - 1):
        cf.append(hf[s:s + 1] + pf[s:s + 1] * cf[s])
    cb = [None] * SUBLANES
    cb[SUBLANES - 1] = h0_ref[0, 1:2, :]
    for s in range(SUBLANES - 1, 0, -1):
        cb[s - 1] = hb[s:s + 1] + pb[s:s + 1] * cb[s]
    cf = jnp.concatenate(cf, axis=0)[None]
    cb = jnp.concatenate(cb, axis=0)[None]

    def fix(s, carry):
        r0 = pl.multiple_of(s * tt, tt)
        corr = (af_ref[pl.ds(r0, tt), :].reshape(ng, SUBLANES, RG_HALF) * cf
                + ab_ref[pl.ds(r0, tt), :].reshape(ng, SUBLANES, RG_HALF) * cb).reshape(tt, RG_HALF)
        g = gather(g_ref, s).astype(F32)
        y = ((s_ref[pl.ds(r0, tt), :] + corr) * _silu(g)).astype(BF16)
        for m in range(nsub):
            back = _dot(permt_ref[...], y[m * block:(m + 1) * block]).astype(BF16)
            for c in range(SUBLANES):
                y_ref[0, source_rows(s, m, c), :] = back[c * RG_GATHER:(c + 1) * RG_GATHER]
        return carry

    lax.fori_loop(0, nt, fix, 0)


def _rg(rg, pw, l, h0, tt, interleaved):
    bsz, t, _ = rg.shape
    assert t % tt == 0
    nhalf = BRANCH_W // RG_HALF
    seq = lambda col0: pl.BlockSpec((1, t, RG_HALF), lambda b, h: (b, 0, col0 + h))
    in_specs = [seq(0), seq(nhalf),
                pl.BlockSpec((None, 2, None, RG_HALF, 2 * RG_HALF), lambda b, h: (l, 0, h, 0, 0)),
                pl.BlockSpec((None, 2, None, 1, 2 * RG_HALF), lambda b, h: (l, 0, h, 0, 0)),
                pl.BlockSpec((None, 2, RG_HALF), lambda b, h: (l, 0, h)),
                pl.BlockSpec((1, SUBLANES, RG_HALF), lambda b, h: (b, 0, h))]
    args = (rg, rg, pw["rg_wg"], pw["rg_bg"], pw["rg_lam"], h0)
    y_spec = pl.BlockSpec((1, t, RG_HALF), lambda b, h: (b, 0, h))
    y_shape = jax.ShapeDtypeStruct((bsz, t, BRANCH_W), BF16)
    seq_scratch = pltpu.VMEM((t, RG_HALF), F32)
    if interleaved:
        block = SUBLANES * RG_GATHER
        assert tt % block == 0
        pos = np.arange(block)
        perm = np.zeros((block, block))
        perm[pos, RG_GATHER * (pos % SUBLANES) + pos // SUBLANES] = 1.0
        perm = jnp.asarray(perm, dtype=F32).astype(BF16)
        pspec = pl.BlockSpec((block, block), lambda b, h: (0, 0))
        y = pl.pallas_call(
            functools.partial(_rg_interleaved_kernel, t=t, tt=tt),
            grid=(bsz, nhalf), in_specs=in_specs[:5] + [pspec, pspec] + in_specs[5:],
            out_specs=y_spec, out_shape=y_shape,
            scratch_shapes=[seq_scratch, seq_scratch, seq_scratch, pltpu.VMEM((t, RG_HALF), BF16)],
            compiler_params=_params("arbitrary", "arbitrary"), name="rglru_interleaved",
        )(*args[:5], perm, perm.T, args[5])
        return y, None
    return pl.pallas_call(
        functools.partial(_rg_kernel, t=t, tt=tt),
        grid=(bsz, nhalf), in_specs=in_specs,
        out_specs=[y_spec, pl.BlockSpec((1, SUBLANES, RG_HALF), lambda b, h: (b, 0, h))],
        out_shape=[y_shape, jax.ShapeDtypeStruct((bsz, SUBLANES, BRANCH_W), F32)],
        scratch_shapes=[seq_scratch],
        compiler_params=_params("arbitrary", "arbitrary"), name="rglru",
    )(*args)


def _ssd_kernel(xbc_ref, z_ref, dt_ref, dtb_ref, alog_ref, dsk_ref, nw_ref, ecol_ref, ehead_ref,
                hmask_ref, h0_ref, y_ref, ht_ref, yb_ref, cs_ref, adjt_ref, diagt_ref, wgtt_ref, etot_ref,
                loc_ref, st_ref, *, t):
    q = SSD_CHUNK
    nc = t // q
    npairs = SSD_HEADS // 2
    pairs_per_group = npairs // SSD_GROUPS
    lane1 = lax.broadcasted_iota(jnp.int32, (1, LANES), 1)
    a_row = -LOG2E * jnp.exp(alog_ref[...])
    fwd_lane = (lax.broadcasted_iota(jnp.int32, (q, LANES), 1) & (SSD_DT - 1)) < SSD_HEADS
    ri = lax.broadcasted_iota(jnp.int32, (q, q), 0)
    ci = lax.broadcasted_iota(jnp.int32, (q, q), 1)
    lower_b = jnp.where(ri >= ci, 1.0, 0.0).astype(BF16)
    head0_s = lax.broadcasted_iota(jnp.int32, (SSD_STATE, LANES), 1) < SSD_HD

    def pair_vec(row, l0, l1):
        return jnp.where(lane1 < SSD_HD, row[:, l0:l0 + 1], row[:, l1:l1 + 1])

    def small_rows(c):
        return pl.ds(pl.multiple_of(c * SSD_DT, SSD_DT), SSD_DT)

    def pass0(c, carry):
        r0 = pl.multiple_of(c * q, q)
        dtv = _softplus(dt_ref[0, pl.ds(r0, q), :] + dtb_ref[...])
        da = dtv * a_row
        pre = sum(_dot(lower_b, part.astype(BF16)) for part in _split3(da))
        tot = pre[q - 1:q, :]
        cs = jnp.where(fwd_lane, pre, tot - pre + da)
        cs_ref[pl.ds(r0, q), :] = cs
        adjt_ref[small_rows(c), :] = (cs - LOG2E * jnp.log(dtv)).T[:SSD_DT]
        diagt_ref[small_rows(c), :] = (LOG2E * jnp.log(dtv + pltpu.roll(dtv, SSD_HEADS, 1))).T[:SSD_DT]
        wgtt_ref[small_rows(c), :] = (dtv * jnp.exp2(tot - cs)).T[:SSD_DT]
        etot_ref[small_rows(c), :] = jnp.broadcast_to(jnp.exp2(tot), (SSD_DT, LANES))
        return carry

    def pass1(i, carry):
        c = nc - 1 - i
        r0 = pl.multiple_of(c * q, q)
        xs_b = xbc_ref[0, pl.ds(r0, q), 0:BRANCH_W]
        bm_b = xbc_ref[0, pl.ds(r0, q), BRANCH_W:BRANCH_W + LANES]
        cm_b = xbc_ref[0, pl.ds(r0, q), BRANCH_W + LANES:SSD_XBC]
        cs_p = _pack3(cs_ref[pl.ds(r0, q), :])
        cs_col = _dot(cs_p, ecol_ref[...])
        ecs_b = jnp.exp2(_dot(cs_p, ehead_ref[:, BRANCH_W:]))
        adjt = adjt_ref[small_rows(c), :]
        diagt = diagt_ref[small_rows(c), :]
        wgtt = wgtt_ref[small_rows(c), :]
        etot = etot_ref[pl.ds(pl.multiple_of(c * SSD_DT, SSD_DT), 1), :]
        bt = bm_b.astype(F32).T
        xs_h0 = xs_b * hmask_ref[0:1, :]
        xs_h1 = xs_b * hmask_ref[1:2, :]
        ys = []
        for g in range(SSD_GROUPS):
            gs = slice(g * SSD_STATE, (g + 1) * SSD_STATE)
            cbm = lax.dot_general(cm_b[:, gs], bm_b[:, gs], (((1,), (1,)), ((), ())), preferred_element_type=F32)
            btg = bt[gs, :]
            for pp in range(pairs_per_group):
                pair = g * pairs_per_group + pp
                ps = slice(pair * LANES, (pair + 1) * LANES)
                state = st_ref[pair]
                lf0 = 2 * pair
                lb0 = SSD_HEADS + lf0
                ms = []
                for lf in (lf0, lf0 + 1):
                    lb = SSD_HEADS + lf
                    arg = jnp.where(ri > ci, cs_col[:, lf * LANES:(lf + 1) * LANES] - adjt[lf:lf + 1, :],
                                    jnp.where(ri < ci, cs_col[:, lb * LANES:(lb + 1) * LANES] - adjt[lb:lb + 1, :],
                                              diagt[lf:lf + 1, :]))
                    ms.append((cbm * jnp.exp2(arg)).astype(BF16))
                yd = _dot(jnp.concatenate(ms, axis=1), jnp.concatenate([xs_h0[:, ps], xs_h1[:, ps]], axis=0))
                yo = _dot(cm_b[:, gs], state.astype(BF16)) * ecs_b[:, ps]
                ys.append(yd + yo)
                stack = jnp.concatenate([(btg * wgtt[l:l + 1, :]).astype(BF16) for l in (lf0, lf0 + 1, lb0, lb0 + 1)],
                                        axis=0)
                res = _dot(stack, xs_b[:, ps])
                n = SSD_STATE
                loc_ref[c, pair] = jnp.where(head0_s, res[0:n], res[n:2 * n])
                st_ref[pair] = state * pair_vec(etot, lb0, lb0 + 1) + jnp.where(head0_s, res[2 * n:3 * n], res[3 * n:])
        yb_ref[pl.ds(r0, q), :] = jnp.concatenate(ys, axis=1) + dsk_ref[...] * xs_b.astype(F32)
        return carry

    def pass2(c, carry):
        r0 = pl.multiple_of(c * q, q)
        cs = cs_ref[pl.ds(r0, q), :]
        cm_b = xbc_ref[0, pl.ds(r0, q), BRANCH_W + LANES:SSD_XBC]
        ecs_f = jnp.exp2(_dot(_pack3(cs), ehead_ref[:, :BRANCH_W]))
        etot = jnp.exp2(cs[q - 1:q, :])
        ys = []
        for pair in range(npairs):
            g = pair // pairs_per_group
            state = st_ref[pair]
            lf0 = 2 * pair
            ys.append(_dot(cm_b[:, g * SSD_STATE:(g + 1) * SSD_STATE], state.astype(BF16))
                      * ecs_f[:, pair * LANES:(pair + 1) * LANES])
            st_ref[pair] = state * pair_vec(etot, lf0, lf0 + 1) + loc_ref[c, pair]
        yt = yb_ref[pl.ds(r0, q), :] + jnp.concatenate(ys, axis=1)
        yt = yt * _silu(z_ref[0, pl.ds(r0, q), :].astype(F32))
        ms = jnp.mean(yt * yt, axis=-1, keepdims=True)
        y_ref[0, pl.ds(r0, q), :] = ((yt * lax.rsqrt(ms + NORM_EPS)) * nw_ref[...]).astype(BF16)
        return carry

    lax.fori_loop(0, nc, pass0, 0, unroll=4 if nc % 4 == 0 else 1)
    for p in range(npairs):
        st_ref[p] = h0_ref[0, 1, p]
    lax.fori_loop(0, nc, pass1, 0, unroll=2)
    for p in range(npairs):
        ht_ref[0, 1, p] = st_ref[p]
        st_ref[p] = h0_ref[0, 0, p]
    lax.fori_loop(0, nc, pass2, 0, unroll=4)
    for p in range(npairs):
        ht_ref[0, 0, p] = st_ref[p]


@functools.lru_cache(maxsize=None)
def _ssd_tables():
    lanes = np.arange(LANES)[:, None]
    src = np.where(lanes < 3 * SSD_DT, lanes % SSD_DT, -1)
    ecol = (src == np.arange(SSD_DT * LANES)[None, :] // LANES)
    ehead = (src == np.arange(2 * BRANCH_W)[None, :] // SSD_HD)
    even_head = (np.arange(BRANCH_W) // SSD_HD) % 2 == 0
    hmask = np.zeros((BF16_ROWS, BRANCH_W))
    hmask[0] = even_head
    hmask[1] = ~even_head
    as_bf16 = lambda a: jnp.asarray(a, dtype=F32).astype(BF16)
    return as_bf16(ecol), as_bf16(ehead), as_bf16(hmask)


def _ssd(xbc, z, dt, pw, l, h0):
    bsz, t, _ = xbc.shape
    assert t % SSD_CHUNK == 0
    npairs = SSD_HEADS // 2
    ecol, ehead, hmask = _ssd_tables()
    full = lambda shape: pl.BlockSpec(shape, lambda b: (0,) * len(shape))
    lay = lambda *shape: _layer_spec(l, shape)
    seq = lambda width: pl.BlockSpec((1, t, width), lambda b: (b, 0, 0))
    st_spec = pl.BlockSpec((1, 2, npairs, SSD_STATE, LANES), lambda b: (b, 0, 0, 0, 0))
    return pl.pallas_call(
        functools.partial(_ssd_kernel, t=t),
        grid=(bsz,),
        in_specs=[seq(SSD_XBC), seq(BRANCH_W), seq(LANES), lay(1, LANES), lay(1, LANES),
                  lay(1, BRANCH_W), lay(1, BRANCH_W), full(ecol.shape), full(ehead.shape), full(hmask.shape),
                  st_spec],
        out_specs=[seq(BRANCH_W), st_spec],
        out_shape=[jax.ShapeDtypeStruct((bsz, t, BRANCH_W), BF16),
                   jax.ShapeDtypeStruct((bsz, 2, npairs, SSD_STATE, LANES), F32)],
        scratch_shapes=[pltpu.VMEM((t, BRANCH_W), F32), pltpu.VMEM((t, LANES), F32)]
        + [pltpu.VMEM((t // SSD_CHUNK * SSD_DT, LANES), F32)] * 4
        + [pltpu.VMEM((t // SSD_CHUNK, npairs, SSD_STATE, LANES), F32),
                        pltpu.VMEM((npairs, SSD_STATE, LANES), F32)],
        compiler_params=_params("arbitrary"),
        name="ssd",
    )(xbc, z, dt, pw["ssd_dtb"], pw["ssd_alog"], pw["ssd_dskip"], pw["ssd_nw"], ecol, ehead, hmask, h0)


@functools.lru_cache(maxsize=None)
def _fourier_tables(n, blk):
    half = n // 2
    nh = half // blk
    assert nh * blk == half and nh <= SUBLANES
    p = np.arange(FN_GD)
    ang_c = 2.0 * np.pi * ((p[:, None] * p[None, :]) % FN_GD) / FN_GD
    eye = np.eye(BRANCH_W // FN_GD)
    cc = np.kron(eye, np.cos(ang_c))
    sc = np.kron(eye, np.sin(ang_c))
    k = np.arange(half)
    ang_t = 2.0 * np.pi * ((k[:, None] * k[None, :]) % n) / n
    ct = np.cos(ang_t)
    stn = -np.sin(ang_t)
    assert blk % FN_FLIP == 0
    jp = np.zeros((FN_FLIP, FN_FLIP))
    r = np.arange(1, FN_FLIP)
    jp[r, FN_FLIP - r] = 1.0
    tt = np.arange(n)
    ks = blk * (np.arange(nh) + 1)
    ang_k = 2.0 * np.pi * ((ks[:, None] * tt[None, :]) % n) / n
    tc = np.zeros((2 * SUBLANES, n))
    ts = np.zeros((2 * SUBLANES, n))
    tc[:nh] = np.cos(ang_k)
    ts[:nh] = np.sin(ang_k)
    tc[SUBLANES, half] = 1.0
    as_bf16 = lambda a: jnp.asarray(a, dtype=F32).astype(BF16)
    return dict(cc=as_bf16(cc), sc=as_bf16(sc), ct=as_bf16(ct), stn=as_bf16(stn), jp=as_bf16(jp),
                tc=as_bf16(tc), ts=as_bf16(ts))


def _mirror(jp_ref, src, row0):
    n = src.shape[0]
    nsub = n // FN_FLIP
    first = lax.broadcasted_iota(jnp.int32, (FN_FLIP, src.shape[1]), 0) == 0
    out = []
    for a in range(nsub):
        s = nsub - 1 - a
        head = row0 if a == 0 else src[(s + 1) * FN_FLIP:(s + 1) * FN_FLIP + 1].astype(F32)
        out.append(jnp.where(first, head, _dot(jp_ref[...], src[s * FN_FLIP:(s + 1) * FN_FLIP])))
    return jnp.concatenate(out, axis=0)


def _fn_fold_kernel(xj_ref, xm_ref, xr_ref, jp_ref, cc_ref, sc_ref, eo_ref):
    j = pl.program_id(1)
    xj = xj_ref[0]
    row0 = xr_ref[0, 0:1, :].astype(F32) * jnp.where(j == 0, 0.0, 1.0)
    mir = _mirror(jp_ref, xm_ref[0], row0).astype(BF16)
    e = _dot(xj, cc_ref[...]) + _dot(mir, cc_ref[...])
    o = _dot(xj, sc_ref[...]) - _dot(mir, sc_ref[...])
    eo_ref[0] = jnp.concatenate([e, o], axis=1).astype(BF16)


def _fn_aux_kernel(x_ref, tc_ref, ts_ref, cc_ref, sc_ref, o_ref):
    x = x_ref[0]
    xc = _dot(tc_ref[...], x).astype(BF16)
    xs = _dot(ts_ref[...], x).astype(BF16)
    o_ref[0] = _dot(xc, cc_ref[...]) + _dot(xs, sc_ref[...])


def _fn_main_kernel(ct_ref, stn_ref, eo_ref, aux_ref, g_ref, jp_ref, y_ref, *, scale):
    m = pl.program_id(1)
    blk = ct_ref.shape[0]
    nblk = y_ref.shape[1] // blk
    rows = lax.broadcasted_iota(jnp.int32, (blk, BRANCH_W), 0)
    p = _dot(ct_ref[...], eo_ref[0, :, 0:BRANCH_W])
    qn = _dot(stn_ref[...], eo_ref[0, :, BRANCH_W:2 * BRANCH_W])
    sgn = (1 - 2 * ((m * blk + rows) & 1)).astype(F32)
    p = p + sgn * aux_ref[0, SUBLANES:SUBLANES + 1, :]
    lo = pl.multiple_of(m * blk, blk)
    hi = pl.multiple_of((nblk - 1 - m) * blk, blk)
    y_ref[0, pl.ds(lo, blk), :] = ((p + qn) * (_silu(g_ref[0, pl.ds(lo, blk), :].astype(F32)) * scale)).astype(BF16)
    sel = lax.broadcasted_iota(jnp.int32, (2 * SUBLANES, BRANCH_W), 0) == m
    row0 = jnp.sum(jnp.where(sel, aux_ref[0], 0.0), axis=0, keepdims=True)
    y_ref[0, pl.ds(hi, blk), :] = (_mirror(jp_ref, (p - qn).astype(BF16), row0)
                                   * (_silu(g_ref[0, pl.ds(hi, blk), :].astype(F32)) * scale)).astype(BF16)


def _fourier(fn, blk):
    bsz, n, _ = fn.shape
    tb = _fourier_tables(n, blk)
    half = n // 2
    nh = half // blk
    nblk = 2 * nh
    rows16 = n // BF16_ROWS
    full2 = lambda shape: pl.BlockSpec(shape, lambda b, j: (0,) * len(shape))
    eo = pl.pallas_call(
        _fn_fold_kernel,
        grid=(bsz, nh),
        in_specs=[pl.BlockSpec((1, blk, BRANCH_W), lambda b, j: (b, j, 0)),
                  pl.BlockSpec((1, blk, BRANCH_W), lambda b, j: (b, nblk - 1 - j, 0)),
                  pl.BlockSpec((1, BF16_ROWS, BRANCH_W),
                               lambda b, j: (b, jnp.minimum((blk // BF16_ROWS) * (nblk - j), rows16 - 1), 0)),
                  full2((FN_FLIP, FN_FLIP)), full2((BRANCH_W, BRANCH_W)), full2((BRANCH_W, BRANCH_W))],
        out_specs=pl.BlockSpec((1, blk, 2 * BRANCH_W), lambda b, j: (b, j, 0)),
        out_shape=jax.ShapeDtypeStruct((bsz, half, 2 * BRANCH_W), BF16),
        compiler_params=_params("arbitrary", "arbitrary"),
        name="fn_fold",
    )(fn, fn, fn, tb["jp"], tb["cc"], tb["sc"])
    full1 = lambda shape: pl.BlockSpec(shape, lambda b: (0,) * len(shape))
    aux = pl.pallas_call(
        _fn_aux_kernel,
        grid=(bsz,),
        in_specs=[pl.BlockSpec((1, n, BRANCH_W), lambda b: (b, 0, 0)),
                  full1((2 * SUBLANES, n)), full1((2 * SUBLANES, n)),
                  full1((BRANCH_W, BRANCH_W)), full1((BRANCH_W, BRANCH_W))],
        out_specs=pl.BlockSpec((1, 2 * SUBLANES, BRANCH_W), lambda b: (b, 0, 0)),
        out_shape=jax.ShapeDtypeStruct((bsz, 2 * SUBLANES, BRANCH_W), F32),
        compiler_params=_params("arbitrary"),
        name="fn_aux",
    )(fn, tb["tc"], tb["ts"], tb["cc"], tb["sc"])
    return pl.pallas_call(
        functools.partial(_fn_main_kernel, scale=float(1.0 / np.sqrt(n * FN_GD))),
        grid=(bsz, nh),
        in_specs=[pl.BlockSpec((blk, half), lambda b, m: (m, 0)),
                  pl.BlockSpec((blk, half), lambda b, m: (m, 0)),
                  pl.BlockSpec((1, half, 2 * BRANCH_W), lambda b, m: (b, 0, 0)),
                  pl.BlockSpec((1, 2 * SUBLANES, BRANCH_W), lambda b, m: (b, 0, 0)),
                  pl.BlockSpec((1, n, BRANCH_W), lambda b, m: (b, 0, 1)),
                  full2((FN_FLIP, FN_FLIP))],
        out_specs=pl.BlockSpec((1, n, BRANCH_W), lambda b, m: (b, 0, 0)),
        out_shape=jax.ShapeDtypeStruct((bsz, n, BRANCH_W), BF16),
        compiler_params=_params("arbitrary", "arbitrary"),
        name="fn_main",
    )(tb["ct"], tb["stn"], eo, aux, fn, tb["jp"])


def _prepare_weights(norm_w, w_in, w_out, rg_conv_w, rg_conv_b, rg_gate_a_w, rg_gate_a_b, rg_gate_x_w, rg_gate_x_b,
                     rg_lambda, sc_conv_w, ssd_conv_w, ssd_conv_b, ssd_dt_bias, ssd_a_log, ssd_d, ssd_norm_w,
                     final_norm_w):
    depth = w_in.shape[0]
    assert REF_DT_HI - REF_DT_LO == SSD_DT and w_in.shape[2] - SSD_DT + LANES == C_END
    w_a = jnp.concatenate([w_in[:, :, :REF_DT_LO], jnp.tile(w_in[:, :, REF_DT_LO:REF_DT_HI], (1, 1, LANES // SSD_DT))],
                          axis=2).astype(BF16)
    w_b = w_in[:, :, REF_DT_HI:].astype(BF16)
    heads_per_half = RG_HALF // RG_HD
    eye = jnp.eye(heads_per_half, dtype=F32)

    def blockdiag(wg):
        wg = wg.reshape(depth, 2, -1, heads_per_half, RG_HD, RG_HD)
        return jnp.einsum("ldhjio,jk->ldhjiko", wg, eye).reshape(depth, 2, -1, RG_HALF, RG_HALF)

    halves = lambda b: b.reshape(depth, 2, -1, 1, RG_HALF)
    rep = LANES // SSD_DT
    return dict(
        norm_w=norm_w.reshape(depth, 1, D_MODEL), final_w=final_norm_w.reshape(1, D_MODEL),
        w_in_a=w_a, w_in_b=w_b, w_out=w_out.astype(BF16), sc_cw=sc_conv_w,
        rg_cw=rg_conv_w, rg_cb=rg_conv_b.reshape(depth, 1, BRANCH_W),
        rg_wg=(0.5 * jnp.concatenate([blockdiag(rg_gate_a_w), blockdiag(rg_gate_x_w)], axis=-1)).astype(BF16),
        rg_bg=0.5 * jnp.concatenate([halves(rg_gate_a_b), halves(rg_gate_x_b)], axis=-1),
        rg_lam=rg_lambda,
        ssd_cw=ssd_conv_w, ssd_cb=ssd_conv_b.reshape(depth, 1, SSD_XBC),
        ssd_dtb=jnp.tile(ssd_dt_bias.reshape(depth, 1, SSD_DT), (1, 1, rep)),
        ssd_alog=jnp.tile(ssd_a_log.reshape(depth, 1, SSD_DT), (1, 1, rep)),
        ssd_dskip=jnp.repeat(ssd_d, SSD_HD, axis=1).reshape(depth, 1, BRANCH_W),
        ssd_nw=ssd_norm_w.reshape(depth, 1, BRANCH_W),
    )


def _mix(x, mod, pw, l, mod_row, rowlen, tm, tt, blk, interleave, rg_h0, ssd_h0, with_output, final):
    rg, xbc, z, dt, *rest = _inproj(x, mod, pw, l, rowlen, tm, mod_row, scan_only=not with_output)
    y_rg, rg_st = _rg(rg, pw, l, rg_h0, tt, interleave)
    y_ssd, ssd_st = _ssd(xbc, z, dt, pw, l, ssd_h0)
    if not with_output:
        return None, rg_st, ssd_st
    fn, y_sc = rest
    y_fn = _fourier(fn, blk)
    return _outproj(y_rg, y_sc, y_fn, y_ssd, x, mod, pw, l, final, tm, mod_row), rg_st, ssd_st


def kernel(x, c, ctx, c_ctx, ada_w, ada_b, norm_w, w_in, w_out, rg_conv_w, rg_conv_b, rg_gate_a_w, rg_gate_a_b,
           rg_gate_x_w, rg_gate_x_b, rg_lambda, sc_conv_w, ssd_conv_w, ssd_conv_b, ssd_dt_bias, ssd_a_log, ssd_d,
           ssd_norm_w, final_norm_w):
    bsz, seq, _ = x.shape
    ctx_len = ctx.shape[1]
    depth = w_in.shape[0]
    assert bsz + 1 <= SUBLANES and seq % SUBLANES == 0
    pw = _prepare_weights(norm_w, w_in, w_out, rg_conv_w, rg_conv_b, rg_gate_a_w, rg_gate_a_b, rg_gate_x_w,
                          rg_gate_x_b, rg_lambda, sc_conv_w, ssd_conv_w, ssd_conv_b, ssd_dt_bias, ssd_a_log, ssd_d,
                          ssd_norm_w, final_norm_w)
    c8 = jnp.concatenate([c, c_ctx[None, :], jnp.zeros((SUBLANES - bsz - 1, D_MODEL), F32)], axis=0)
    mod = _ada(c8, ada_w, ada_b.reshape(depth, 1, 3 * D_MODEL))
    zeros_rg = jnp.zeros((bsz, SUBLANES, BRANCH_W), F32)
    zeros_ssd = jnp.zeros((bsz, 2, SSD_HEADS // 2, SSD_STATE, LANES), F32)
    tm_lat = min(1024, seq)
    tt_lat = min(1024, seq)
    tt_ctx = min(256, ctx_len)
    blk_lat = min(512, seq // 2)
    blk_ctx = min(512, ctx_len // 2)
    for l in range(depth):
        last = l == depth - 1
        new_ctx, rg_st, ssd_st = _mix(ctx, mod, pw, l, bsz, ctx_len, ctx_len, tt_ctx, blk_ctx, False,
                                      zeros_rg, zeros_ssd, not last, False)
        x, _, _ = _mix(x, mod, pw, l, None, GRID_W, tm_lat, tt_lat, blk_lat, True, rg_st, ssd_st, True, last)
        if not last:
            ctx = new_ctx
    return x
```

```python
import functools

import numpy as np
import jax
import jax.numpy as jnp
from jax import lax
from jax.experimental import pallas as pl
from jax.experimental.pallas import tpu as pltpu

F32 = jnp.float32
BF16 = jnp.bfloat16
HIGHEST = lax.Precision.HIGHEST

D_MODEL = 1024
D_INNER = 2048
BRANCH_W = 512
GRID_W = 64
RG_HD = 64
RG_C = 8.0
RG_HALF = 256
SSD_HEADS = 8
SSD_HD = 64
SSD_GROUPS = 2
SSD_STATE = 64
SSD_XBC = BRANCH_W + 2 * SSD_GROUPS * SSD_STATE
SSD_CHUNK = 128
SSD_DT = 2 * SSD_HEADS
FN_GD = 128
FN_FLIP = 128
NORM_EPS = 1e-6

LANES = 128
SUBLANES = 8
BF16_ROWS = 16
VMEM_LIMIT_BYTES = 60000 * 1024

C_RGX, C_XBC, C_RGG, C_Z, C_SC, C_FN, C_DT, C_END = 0, 512, 1280, 1792, 2304, 4352, 5376, 5504
REF_DT_LO, REF_DT_HI = 1280, 1296


def _sigmoid(v):
    return 0.5 + 0.5 * jnp.tanh(0.5 * v)


def _silu(v):
    h = 0.5 * v
    return h + h * jnp.tanh(h)


def _softplus(v):
    return jnp.maximum(v, 0.0) + jnp.log1p(jnp.exp(-jnp.abs(v)))


def _dot(a, b):
    return jnp.dot(a, b, preferred_element_type=F32)


def _split3(v):
    hi = v.astype(BF16).astype(F32)
    rest = v - hi
    mid = rest.astype(BF16).astype(F32)
    return hi, mid, rest - mid


def _pack3(v):
    hi, mid, lo = _split3(v)
    period = lax.broadcasted_iota(jnp.int32, v.shape, 1) // SSD_DT
    return jnp.where(period == 0, hi, jnp.where(period == 1, mid, jnp.where(period == 2, lo, 0.0))).astype(BF16)


def _params(*semantics):
    return pltpu.CompilerParams(dimension_semantics=semantics, vmem_limit_bytes=VMEM_LIMIT_BYTES)


def _layer_spec(l, shape, single_buffer=False):
    mode = dict(pipeline_mode=pl.Buffered(1)) if single_buffer else {}
    return pl.BlockSpec((None,) + tuple(shape), lambda *_: (l,) + (0,) * len(shape), **mode)


def _ada_kernel(c_ref, w_ref, b_ref, o_ref):
    o_ref[...] = jnp.dot(_silu(c_ref[...]), w_ref[...], precision=HIGHEST,
                         preferred_element_type=F32) + b_ref[...]


def _ada(c8, w, b):
    depth = w.shape[0]
    tn = 512
    return pl.pallas_call(
        _ada_kernel,
        grid=(depth, 3 * D_MODEL // tn),
        in_specs=[pl.BlockSpec((SUBLANES, D_MODEL), lambda l, j: (0, 0)),
                  pl.BlockSpec((None, D_MODEL, tn), lambda l, j: (l, 0, j)),
                  pl.BlockSpec((None, 1, tn), lambda l, j: (l, 0, j))],
        out_specs=pl.BlockSpec((None, SUBLANES, tn), lambda l, j: (l, 0, j)),
        out_shape=jax.ShapeDtypeStruct((depth, SUBLANES, 3 * D_MODEL), F32),
        compiler_params=_params("arbitrary", "arbitrary"),
        name="ada",
    )(c8, w, b)


def _inproj_kernel(*refs, rowlen, mod_row, halo, scan_only):
    if halo:
        x_ref, xp_ref, xn_ref, *refs = refs
    else:
        x_ref, *refs = refs
    nw_ref, sh_ref, sc_ref, wa_ref, wb_ref, cw_ref, rcw_ref, rcb_ref, scw_ref, scb_ref, *outs = refs
    nb, tm, _ = x_ref.shape
    rows = nb * tm
    i = pl.program_id(1)
    row = pl.program_id(0) if mod_row is None else mod_row
    if halo:
        x = jnp.concatenate([xp_ref[0], x_ref[0], xn_ref[0]], axis=0)
    else:
        x = x_ref[...].reshape(rows, D_MODEL)
    ms = jnp.mean(x * x, axis=-1, keepdims=True)
    h = (x * lax.rsqrt(ms + NORM_EPS)) * nw_ref[...]
    h = h * (1.0 + sc_ref[pl.ds(row, 1), :]) + sh_ref[pl.ds(row, 1), :]
    hx = h.astype(BF16)
    hb = hx[halo:halo + rows]

    def wcols(lo, hi):
        if hi <= REF_DT_LO:
            return wa_ref[:, lo:hi]
        if lo >= C_DT:
            return wa_ref[:, lo - C_DT + REF_DT_LO:hi - C_DT + REF_DT_LO]
        assert lo >= REF_DT_LO and hi <= C_DT
        return wb_ref[:, lo - REF_DT_LO:hi - REF_DT_LO]

    def proj(lo, hi):
        return _dot(hb, wcols(lo, hi))

    def proj_halo(lo, hi):
        return _dot(hx, wcols(lo, hi))

    def conv4(p, cw, cb):
        n = p.shape[0]
        if halo:
            keep_prev = jnp.where(i == 0, 0.0, 1.0)
            keep_next = jnp.where(i == pl.num_programs(1) - 1, 0.0, 1.0)
            p = jnp.concatenate([p[:halo] * keep_prev, p[halo:halo + rows], p[halo + rows:] * keep_next], axis=0)
            mid = slice(halo, halo + rows)
            taps = [pltpu.roll(p, 2, 0)[mid], pltpu.roll(p, 1, 0)[mid], p[mid], pltpu.roll(p, n - 1, 0)[mid]]
        else:
            pos = lax.broadcasted_iota(jnp.int32, p.shape, 0) & (tm - 1)
            taps = [jnp.where(pos >= 2, pltpu.roll(p, 2, 0), 0.0), jnp.where(pos >= 1, pltpu.roll(p, 1, 0), 0.0), p,
                    jnp.where(pos <= tm - 2, pltpu.roll(p, n - 1, 0), 0.0)]
        return sum(cw[k:k + 1, :] * taps[k] for k in range(4)) + cb

    def put(ref, value, lo=0):
        ref[:, :, lo:lo + value.shape[1]] = value.reshape(nb, tm, value.shape[1]).astype(ref.dtype)

    stages = []
    if scan_only:
        rg_ref, xbc_ref, z_ref, dt_ref = outs
    else:
        rg_ref, xbc_ref, z_ref, dt_ref, fn_ref, ysc_ref = outs
        conv3 = []

        def sc_conv(ps):
            v = ps[0] * ps[1]
            pos = lax.broadcasted_iota(jnp.int32, (rows, BRANCH_W), 0) & (rowlen - 1)
            vm1 = jnp.where(pos == 0, 0.0, pltpu.roll(v, 1, 0))
            vp1 = jnp.where(pos == rowlen - 1, 0.0, pltpu.roll(v, rows - 1, 0))
            conv3.append(cw_ref[0:1, :] * vm1 + cw_ref[1:2, :] * v + cw_ref[2:3, :] * vp1)

        stages.append((lambda: (proj(C_SC + 512, C_SC + 1024), proj(C_SC + 1024, C_SC + 1536)), sc_conv))
        stages.append((lambda: (proj(C_SC, C_SC + 512), proj(C_SC + 1536, C_FN)),
                       lambda ps: put(ysc_ref, ps[0] * conv3[0] * _silu(ps[1]))))
    stages.append((lambda: proj_halo(C_XBC, C_RGG),
                   lambda p: put(xbc_ref, _silu(conv4(p, scw_ref[...], scb_ref[...])))))
    stages.append((lambda: proj_halo(C_RGX, C_XBC), lambda p: put(rg_ref, conv4(p, rcw_ref[...], rcb_ref[...]))))
    stages.append((lambda: proj(C_RGG, C_Z), lambda p: put(rg_ref, p, BRANCH_W)))
    stages.append((lambda: proj(C_Z, C_SC), lambda p: put(z_ref, p)))
    if not scan_only:
        stages.append((lambda: proj(C_FN, C_DT), lambda p: put(fn_ref, p)))
    stages.append((lambda: proj(C_DT, C_END), lambda p: put(dt_ref, p)))
    pending = None
    for matmuls, epilogue in stages:
        result = matmuls()
        if pending is not None:
            pending[1](pending[0])
        pending = (result, epilogue)
    pending[1](pending[0])


def _inproj(x, mod, pw, l, rowlen, tm, mod_row, scan_only=False):
    bsz, t, _ = x.shape
    whole = tm == t and mod_row is not None and tm & (tm - 1) == 0
    halo = 0 if whole else BF16_ROWS
    nb = bsz if whole else 1
    assert t % tm == 0 and tm % rowlen == 0 and rowlen & (rowlen - 1) == 0 and tm % BF16_ROWS == 0
    per_tile = tm // BF16_ROWS
    last_halo = t // BF16_ROWS - 1
    tok = lambda width: pl.BlockSpec((nb, tm, width), lambda b, i: (b, i, 0))
    lay = lambda *shape: _layer_spec(l, shape)
    out = lambda width, dt: jax.ShapeDtypeStruct((bsz, t, width), dt)
    x_specs, xs = [tok(D_MODEL)], [x]
    if halo:
        x_specs += [pl.BlockSpec((1, halo, D_MODEL), lambda b, i: (b, jnp.maximum(i * per_tile - 1, 0), 0)),
                    pl.BlockSpec((1, halo, D_MODEL), lambda b, i: (b, jnp.minimum((i + 1) * per_tile, last_halo), 0))]
        xs += [x, x]
    widths = [(1024, BF16), (SSD_XBC, BF16), (512, BF16), (LANES, F32)]
    if not scan_only:
        widths += [(1024, BF16), (512, BF16)]
    return pl.pallas_call(
        functools.partial(_inproj_kernel, rowlen=rowlen, mod_row=mod_row, halo=halo, scan_only=scan_only),
        grid=(bsz // nb, t // tm),
        in_specs=x_specs + [lay(1, D_MODEL),
                            pl.BlockSpec((None, SUBLANES, D_MODEL), lambda b, i: (l, 0, 0)),
                            pl.BlockSpec((None, SUBLANES, D_MODEL), lambda b, i: (l, 0, 1)),
                            _layer_spec(l, pw["w_in_a"].shape[1:], single_buffer=True),
                            _layer_spec(l, pw["w_in_b"].shape[1:], single_buffer=True),
                            lay(3, BRANCH_W), lay(4, BRANCH_W), lay(1, BRANCH_W),
                            lay(4, SSD_XBC), lay(1, SSD_XBC)],
        out_specs=[tok(w) for w, _ in widths],
        out_shape=[out(w, dt) for w, dt in widths],
        compiler_params=_params("arbitrary", "arbitrary"),
        name="inproj",
    )(*xs, pw["norm_w"], mod, mod, pw["w_in_a"], pw["w_in_b"], pw["sc_cw"], pw["rg_cw"], pw["rg_cb"], pw["ssd_cw"],
      pw["ssd_cb"])


def _outproj_kernel(yrg_ref, ysc_ref, yfn_ref, yssd_ref, w_ref, x_ref, g_ref, fw_ref, o_ref, *, final, mod_row):
    row = pl.program_id(0) if mod_row is None else mod_row
    nb, tm, _ = x_ref.shape
    flat = lambda ref: ref[...].reshape(nb * tm, ref.shape[2])
    acc = _dot(flat(yrg_ref), w_ref[0:512, :])
    acc += _dot(flat(ysc_ref), w_ref[512:1024, :])
    acc += _dot(flat(yfn_ref), w_ref[1024:1536, :])
    acc += _dot(flat(yssd_ref), w_ref[1536:2048, :])
    xn = flat(x_ref) + g_ref[pl.ds(row, 1), :] * acc
    if final:
        ms = jnp.mean(xn * xn, axis=-1, keepdims=True)
        xn = (xn * lax.rsqrt(ms + NORM_EPS)) * fw_ref[...]
    o_ref[...] = xn.reshape(nb, tm, D_MODEL)


def _outproj(y_rg, y_sc, y_fn, y_ssd, x, mod, pw, l, final, tm, mod_row):
    bsz, t, _ = x.shape
    nb = bsz if (tm == t and mod_row is not None) else 1
    ytok = pl.BlockSpec((nb, tm, BRANCH_W), lambda b, i: (b, i, 0))
    xtok = pl.BlockSpec((nb, tm, D_MODEL), lambda b, i: (b, i, 0))
    return pl.pallas_call(
        functools.partial(_outproj_kernel, final=final, mod_row=mod_row),
        grid=(bsz // nb, t // tm),
        in_specs=[ytok, ytok, ytok, ytok,
                  _layer_spec(l, (D_INNER, D_MODEL), single_buffer=True),
                  xtok,
                  pl.BlockSpec((None, SUBLANES, D_MODEL), lambda b, i: (l, 0, 2)),
                  pl.BlockSpec((1, D_MODEL), lambda b, i: (0, 0))],
        out_specs=xtok,
        out_shape=jax.ShapeDtypeStruct((bsz, t, D_MODEL), F32),
        compiler_params=_params("arbitrary", "arbitrary"),
        name="outproj",
    )(y_rg, y_sc, y_fn, y_ssd, pw["w_out"], x, mod, pw["final_w"])


MIN_NORMAL_F32 = float(np.finfo(np.float32).tiny)
LOG2E = float(np.log2(np.e))


def _rg_coefficients(ub, wg_half, bg_half, lam):
    uh = 0.5 * ub.astype(F32)
    th = jnp.tanh(_dot(ub, wg_half) + bg_half)
    k = (-0.5 * RG_C * np.log2(np.e)) * _softplus(-lam)
    a = jnp.exp2(k + k * th[:, :RG_HALF])
    x = 1.0 - a * a
    root = x * lax.rsqrt(jnp.maximum(x, MIN_NORMAL_F32))
    return a, root * (uh + uh * th[:, RG_HALF:])


def _scan_tile(a, v, h_in, sub, reverse):
    rows = a.shape[0]
    for s in (1, 2, 4):
        if reverse:
            keep = sub < SUBLANES - s
            a_sh = jnp.where(keep, pltpu.roll(a, rows - s, 0), 1.0)
            v_sh = jnp.where(keep, pltpu.roll(v, rows - s, 0), 0.0)
        else:
            keep = sub >= s
            a_sh = jnp.where(keep, pltpu.roll(a, s, 0), 1.0)
            v_sh = jnp.where(keep, pltpu.roll(v, s, 0), 0.0)
        v = v + a * v_sh
        a = a * a_sh
    ngroups = rows // SUBLANES
    out = [None] * ngroups
    h = h_in
    order = range(ngroups - 1, -1, -1) if reverse else range(ngroups)
    for g in order:
        lo = g * SUBLANES
        hg = v[lo:lo + SUBLANES] + a[lo:lo + SUBLANES] * h
        out[g] = hg
        h = hg[0:1] if reverse else hg[SUBLANES - 1:SUBLANES]
    return jnp.concatenate(out, axis=0), h


def _rg_kernel(u_ref, g_ref, wg_ref, bg_ref, lam_ref, h0_ref, y_ref, ht_ref, hb_ref, *, t, tt):
    nt = t // tt
    sub = lax.broadcasted_iota(jnp.int32, (tt, RG_HALF), 0) & (SUBLANES - 1)

    def coeffs(s, d):
        r0 = pl.multiple_of(s * tt, tt)
        return _rg_coefficients(u_ref[0, pl.ds(r0, tt), :], wg_ref[d], bg_ref[d], lam_ref[d:d + 1, :])

    def rev_body(i, h):
        s = nt - 1 - i
        a, v = coeffs(s, 1)
        hh, hn = _scan_tile(a, v, h, sub, True)
        hb_ref[pl.ds(pl.multiple_of(s * tt, tt), tt), :] = hh
        return hn

    h_rev = lax.fori_loop(0, nt, rev_body, h0_ref[0, 1:2, :])

    def fwd_body(s, h):
        a, v = coeffs(s, 0)
        hh, hn = _scan_tile(a, v, h, sub, False)
        r0 = pl.multiple_of(s * tt, tt)
        g = g_ref[0, pl.ds(r0, tt), :].astype(F32)
        y_ref[0, pl.ds(r0, tt), :] = ((hh + hb_ref[pl.ds(r0, tt), :]) * _silu(g)).astype(BF16)
        return hn

    h_fwd = lax.fori_loop(0, nt, fwd_body, h0_ref[0, 0:1, :])
    ht_ref[0] = jnp.concatenate([h_fwd, h_rev, jnp.zeros((SUBLANES - 2, RG_HALF), F32)], axis=0)


RG_GATHER = BF16_ROWS


def _rg_interleaved_kernel(u_ref, g_ref, wg_ref, bg_ref, lam_ref, perm_ref, permt_ref, h0_ref, y_ref,
                           s_ref, af_ref, ab_ref, up_ref, *, t, tt):
    nt = t // tt
    ng = tt // SUBLANES
    chunk = t // SUBLANES
    steps = tt // SUBLANES
    nsub = steps // RG_GATHER
    block = SUBLANES * RG_GATHER
    shape = (SUBLANES, RG_HALF)

    def source_rows(k, m, s):
        return pl.ds(pl.multiple_of(s * chunk + k * steps + m * RG_GATHER, RG_GATHER), RG_GATHER)

    def gather(ref, k):
        blocks = []
        for m in range(nsub):
            rows = jnp.concatenate([ref[0, source_rows(k, m, s), :] for s in range(SUBLANES)], axis=0)
            blocks.append(_dot(perm_ref[...], rows).astype(BF16))
        return jnp.concatenate(blocks, axis=0)

    def sweep(d):
        def body(i, carry):
            h, acc = carry
            s = i if d == 0 else nt - 1 - i
            r0 = pl.multiple_of(s * tt, tt)
            if d == 1:
                ub = gather(u_ref, s)
                up_ref[pl.ds(r0, tt), :] = ub
            else:
                ub = up_ref[pl.ds(r0, tt), :]
            a, v = _rg_coefficients(ub, wg_ref[d], bg_ref[d], lam_ref[d:d + 1, :])
            hs, ps = [None] * ng, [None] * ng
            for g in (range(ng) if d == 0 else range(ng - 1, -1, -1)):
                ag = a[g * SUBLANES:(g + 1) * SUBLANES]
                h = ag * h + v[g * SUBLANES:(g + 1) * SUBLANES]
                acc = ag * acc
                hs[g], ps[g] = h, acc
            local = jnp.concatenate(hs, axis=0)
            if d == 1:
                s_ref[pl.ds(r0, tt), :] = local
                ab_ref[pl.ds(r0, tt), :] = jnp.concatenate(ps, axis=0)
            else:
                s_ref[pl.ds(r0, tt), :] += local
                af_ref[pl.ds(r0, tt), :] = jnp.concatenate(ps, axis=0)
            return h, acc

        return lax.fori_loop(0, nt, body, (jnp.zeros(shape, F32), jnp.ones(shape, F32)), unroll=2)

    hb, pb = sweep(1)
    hf, pf = sweep(0)
    cf = [h0_ref[0, 0:1, :]]
    for s in range(SUBLANES - 1):
        cf.append(hf[s:s + 1] + pf[s:s + 1] * cf[s])
    cb = [None] * SUBLANES
    cb[SUBLANES - 1] = h0_ref[0, 1:2, :]
    for s in range(SUBLANES - 1, 0, -1):
        cb[s - 1] = hb[s:s + 1] + pb[s:s + 1] * cb[s]
    cf = jnp.concatenate(cf, axis=0)[None]
    cb = jnp.concatenate(cb, axis=0)[None]

    def fix(s, carry):
        r0 = pl.multiple_of(s * tt, tt)
        corr = (af_ref[pl.ds(r0, tt), :].reshape(ng, SUBLANES, RG_HALF) * cf
                + ab_ref[pl.ds(r0, tt), :].reshape(ng, SUBLANES, RG_HALF) * cb).reshape(tt, RG_HALF)
        g = gather(g_ref, s).astype(F32)
        y = ((s_ref[pl.ds(r0, tt), :] + corr) * _silu(g)).astype(BF16)
        for m in range(nsub):
            back = _dot(permt_ref[...], y[m * block:(m + 1) * block]).astype(BF16)
            for c in range(SUBLANES):
                y_ref[0, source_rows(s, m, c), :] = back[c * RG_GATHER:(c + 1) * RG_GATHER]
        return carry

    lax.fori_loop(0, nt, fix, 0, unroll=2)


def _rg(rg, pw, l, h0, tt, interleaved):
    bsz, t, _ = rg.shape
    assert t % tt == 0
    nhalf = BRANCH_W // RG_HALF
    seq = lambda col0: pl.BlockSpec((1, t, RG_HALF), lambda b, h: (b, 0, col0 + h))
    in_specs = [seq(0), seq(nhalf),
                pl.BlockSpec((None, 2, None, RG_HALF, 2 * RG_HALF), lambda b, h: (l, 0, h, 0, 0)),
                pl.BlockSpec((None, 2, None, 1, 2 * RG_HALF), lambda b, h: (l, 0, h, 0, 0)),
                pl.BlockSpec((None, 2, RG_HALF), lambda b, h: (l, 0, h)),
                pl.BlockSpec((1, SUBLANES, RG_HALF), lambda b, h: (b, 0, h))]
    args = (rg, rg, pw["rg_wg"], pw["rg_bg"], pw["rg_lam"], h0)
    y_spec = pl.BlockSpec((1, t, RG_HALF), lambda b, h: (b, 0, h))
    y_shape = jax.ShapeDtypeStruct((bsz, t, BRANCH_W), BF16)
    seq_scratch = pltpu.VMEM((t, RG_HALF), F32)
    if interleaved:
        block = SUBLANES * RG_GATHER
        assert tt % block == 0
        pos = np.arange(block)
        perm = np.zeros((block, block))
        perm[pos, RG_GATHER * (pos % SUBLANES) + pos // SUBLANES] = 1.0
        perm = jnp.asarray(perm, dtype=F32).astype(BF16)
        pspec = pl.BlockSpec((block, block), lambda b, h: (0, 0))
        y = pl.pallas_call(
            functools.partial(_rg_interleaved_kernel, t=t, tt=tt),
            grid=(bsz, nhalf), in_specs=in_specs[:5] + [pspec, pspec] + in_specs[5:],
            out_specs=y_spec, out_shape=y_shape,
            scratch_shapes=[seq_scratch, seq_scratch, seq_scratch, pltpu.VMEM((t, RG_HALF), BF16)],
            compiler_params=_params("arbitrary", "arbitrary"), name="rglru_interleaved",
        )(*args[:5], perm, perm.T, args[5])
        return y, None
    return pl.pallas_call(
        functools.partial(_rg_kernel, t=t, tt=tt),
        grid=(bsz, nhalf), in_specs=in_specs,
        out_specs=[y_spec, pl.BlockSpec((1, SUBLANES, RG_HALF), lambda b, h: (b, 0, h))],
        out_shape=[y_shape, jax.ShapeDtypeStruct((bsz, SUBLANES, BRANCH_W), F32)],
        scratch_shapes=[seq_scratch],
        compiler_params=_params("arbitrary", "arbitrary"), name="rglru",
    )(*args)


def _ssd_kernel(xbc_ref, z_ref, dt_ref, dtb_ref, alog_ref, dsk_ref, nw_ref, ecol_ref, ehead_ref,
                hmask_ref, h0_ref, y_ref, ht_ref, yb_ref, cs_ref, adjt_ref, diagt_ref, wgtt_ref, etot_ref,
                loc_ref, st_ref, *, t):
    q = SSD_CHUNK
    nc = t // q
    npairs = SSD_HEADS // 2
    pairs_per_group = npairs // SSD_GROUPS
    lane1 = lax.broadcasted_iota(jnp.int32, (1, LANES), 1)
    a_row = -LOG2E * jnp.exp(alog_ref[...])
    fwd_lane = (lax.broadcasted_iota(jnp.int32, (q, LANES), 1) & (SSD_DT - 1)) < SSD_HEADS
    ri = lax.broadcasted_iota(jnp.int32, (q, q), 0)
    ci = lax.broadcasted_iota(jnp.int32, (q, q), 1)
    lower_b = jnp.where(ri >= ci, 1.0, 0.0).astype(BF16)
    head0_s = lax.broadcasted_iota(jnp.int32, (SSD_STATE, LANES), 1) < SSD_HD

    def pair_vec(row, l0, l1):
        return jnp.where(lane1 < SSD_HD, row[:, l0:l0 + 1], row[:, l1:l1 + 1])

    def small_rows(c):
        return pl.ds(pl.multiple_of(c * SSD_DT, SSD_DT), SSD_DT)

    def pass0(c, carry):
        r0 = pl.multiple_of(c * q, q)
        dtv = _softplus(dt_ref[0, pl.ds(r0, q), :] + dtb_ref[...])
        da = dtv * a_row
        pre = sum(_dot(lower_b, part.astype(BF16)) for part in _split3(da))
        tot = pre[q - 1:q, :]
        cs = jnp.where(fwd_lane, pre, tot - pre + da)
        cs_ref[pl.ds(r0, q), :] = cs
        adjt_ref[small_rows(c), :] = (cs - LOG2E * jnp.log(dtv)).T[:SSD_DT]
        diagt_ref[small_rows(c), :] = (LOG2E * jnp.log(dtv + pltpu.roll(dtv, SSD_HEADS, 1))).T[:SSD_DT]
        wgtt_ref[small_rows(c), :] = (dtv * jnp.exp2(tot - cs)).T[:SSD_DT]
        etot_ref[small_rows(c), :] = jnp.broadcast_to(jnp.exp2(tot), (SSD_DT, LANES))
        return carry

    def pass1(i, carry):
        c = nc - 1 - i
        r0 = pl.multiple_of(c * q, q)
        xs_b = xbc_ref[0, pl.ds(r0, q), 0:BRANCH_W]
        bm_b = xbc_ref[0, pl.ds(r0, q), BRANCH_W:BRANCH_W + LANES]
        cm_b = xbc_ref[0, pl.ds(r0, q), BRANCH_W + LANES:SSD_XBC]
        cs_p = _pack3(cs_ref[pl.ds(r0, q), :])
        cs_col = _dot(cs_p, ecol_ref[...])
        ecs_b = jnp.exp2(_dot(cs_p, ehead_ref[:, BRANCH_W:]))
        adjt = adjt_ref[small_rows(c), :]
        diagt = diagt_ref[small_rows(c), :]
        wgtt = wgtt_ref[small_rows(c), :]
        etot = etot_ref[pl.ds(pl.multiple_of(c * SSD_DT, SSD_DT), 1), :]
        bt = bm_b.astype(F32).T
        xs_h0 = xs_b * hmask_ref[0:1, :]
        xs_h1 = xs_b * hmask_ref[1:2, :]
        ys = []
        for g in range(SSD_GROUPS):
            gs = slice(g * SSD_STATE, (g + 1) * SSD_STATE)
            cbm = lax.dot_general(cm_b[:, gs], bm_b[:, gs], (((1,), (1,)), ((), ())), preferred_element_type=F32)
            btg = bt[gs, :]
            for pp in range(pairs_per_group):
                pair = g * pairs_per_group + pp
                ps = slice(pair * LANES, (pair + 1) * LANES)
                state = st_ref[pair]
                lf0 = 2 * pair
                lb0 = SSD_HEADS + lf0
                ms = []
                for lf in (lf0, lf0 + 1):
                    lb = SSD_HEADS + lf
                    arg = jnp.where(ri > ci, cs_col[:, lf * LANES:(lf + 1) * LANES] - adjt[lf:lf + 1, :],
                                    jnp.where(ri < ci, cs_col[:, lb * LANES:(lb + 1) * LANES] - adjt[lb:lb + 1, :],
                                              diagt[lf:lf + 1, :]))
                    ms.append((cbm * jnp.exp2(arg)).astype(BF16))
                yd = _dot(jnp.concatenate(ms, axis=1), jnp.concatenate([xs_h0[:, ps], xs_h1[:, ps]], axis=0))
                yo = _dot(cm_b[:, gs], state.astype(BF16)) * ecs_b[:, ps]
                ys.append(yd + yo)
                stack = jnp.concatenate([(btg * wgtt[l:l + 1, :]).astype(BF16) for l in (lf0, lf0 + 1, lb0, lb0 + 1)],
                                        axis=0)
                res = _dot(stack, xs_b[:, ps])
                n = SSD_STATE
                loc_ref[c, pair] = jnp.where(head0_s, res[0:n], res[n:2 * n])
                st_ref[pair] = state * pair_vec(etot, lb0, lb0 + 1) + jnp.where(head0_s, res[2 * n:3 * n], res[3 * n:])
        yb_ref[pl.ds(r0, q), :] = jnp.concatenate(ys, axis=1) + dsk_ref[...] * xs_b.astype(F32)
        return carry

    def pass2(c, carry):
        r0 = pl.multiple_of(c * q, q)
        cs = cs_ref[pl.ds(r0, q), :]
        cm_b = xbc_ref[0, pl.ds(r0, q), BRANCH_W + LANES:SSD_XBC]
        ecs_f = jnp.exp2(_dot(_pack3(cs), ehead_ref[:, :BRANCH_W]))
        etot = jnp.exp2(cs[q - 1:q, :])
        ys = []
        for pair in range(npairs):
            g = pair // pairs_per_group
            state = st_ref[pair]
            lf0 = 2 * pair
            ys.append(_dot(cm_b[:, g * SSD_STATE:(g + 1) * SSD_STATE], state.astype(BF16))
                      * ecs_f[:, pair * LANES:(pair + 1) * LANES])
            st_ref[pair] = state * pair_vec(etot, lf0, lf0 + 1) + loc_ref[c, pair]
        yt = yb_ref[pl.ds(r0, q), :] + jnp.concatenate(ys, axis=1)
        yt = yt * _silu(z_ref[0, pl.ds(r0, q), :].astype(F32))
        ms = jnp.mean(yt * yt, axis=-1, keepdims=True)
        y_ref[0, pl.ds(r0, q), :] = ((yt * lax.rsqrt(ms + NORM_EPS)) * nw_ref[...]).astype(BF16)
        return carry

    lax.fori_loop(0, nc, pass0, 0, unroll=4 if nc % 4 == 0 else 1)
    for p in range(npairs):
        st_ref[p] = h0_ref[0, 1, p]
    lax.fori_loop(0, nc, pass1, 0, unroll=2)
    for p in range(npairs):
        ht_ref[0, 1, p] = st_ref[p]
        st_ref[p] = h0_ref[0, 0, p]
    lax.fori_loop(0, nc, pass2, 0, unroll=4)
    for p in range(npairs):
        ht_ref[0, 0, p] = st_ref[p]


@functools.lru_cache(maxsize=None)
def _ssd_tables():
    lanes = np.arange(LANES)[:, None]
    src = np.where(lanes < 3 * SSD_DT, lanes % SSD_DT, -1)
    ecol = (src == np.arange(SSD_DT * LANES)[None, :] // LANES)
    ehead = (src == np.arange(2 * BRANCH_W)[None, :] // SSD_HD)
    even_head = (np.arange(BRANCH_W) // SSD_HD) % 2 == 0
    hmask = np.zeros((BF16_ROWS, BRANCH_W))
    hmask[0] = even_head
    hmask[1] = ~even_head
    as_bf16 = lambda a: jnp.asarray(a, dtype=F32).astype(BF16)
    return as_bf16(ecol), as_bf16(ehead), as_bf16(hmask)


def _ssd(xbc, z, dt, pw, l, h0):
    bsz, t, _ = xbc.shape
    assert t % SSD_CHUNK == 0
    npairs = SSD_HEADS // 2
    ecol, ehead, hmask = _ssd_tables()
    full = lambda shape: pl.BlockSpec(shape, lambda b: (0,) * len(shape))
    lay = lambda *shape: _layer_spec(l, shape)
    seq = lambda width: pl.BlockSpec((1, t, width), lambda b: (b, 0, 0))
    st_spec = pl.BlockSpec((1, 2, npairs, SSD_STATE, LANES), lambda b: (b, 0, 0, 0, 0))
    return pl.pallas_call(
        functools.partial(_ssd_kernel, t=t),
        grid=(bsz,),
        in_specs=[seq(SSD_XBC), seq(BRANCH_W), seq(LANES), lay(1, LANES), lay(1, LANES),
                  lay(1, BRANCH_W), lay(1, BRANCH_W), full(ecol.shape), full(ehead.shape), full(hmask.shape),
                  st_spec],
        out_specs=[seq(BRANCH_W), st_spec],
        out_shape=[jax.ShapeDtypeStruct((bsz, t, BRANCH_W), BF16),
                   jax.ShapeDtypeStruct((bsz, 2, npairs, SSD_STATE, LANES), F32)],
        scratch_shapes=[pltpu.VMEM((t, BRANCH_W), F32), pltpu.VMEM((t, LANES), F32)]
        + [pltpu.VMEM((t // SSD_CHUNK * SSD_DT, LANES), F32)] * 4
        + [pltpu.VMEM((t // SSD_CHUNK, npairs, SSD_STATE, LANES), F32),
                        pltpu.VMEM((npairs, SSD_STATE, LANES), F32)],
        compiler_params=_params("arbitrary"),
        name="ssd",
    )(xbc, z, dt, pw["ssd_dtb"], pw["ssd_alog"], pw["ssd_dskip"], pw["ssd_nw"], ecol, ehead, hmask, h0)


@functools.lru_cache(maxsize=None)
def _fourier_tables(n, blk):
    half = n // 2
    nh = half // blk
    assert nh * blk == half and nh <= SUBLANES
    p = np.arange(FN_GD)
    ang_c = 2.0 * np.pi * ((p[:, None] * p[None, :]) % FN_GD) / FN_GD
    eye = np.eye(BRANCH_W // FN_GD)
    cc = np.kron(eye, np.cos(ang_c))
    sc = np.kron(eye, np.sin(ang_c))
    k = np.arange(half)
    ang_t = 2.0 * np.pi * ((k[:, None] * k[None, :]) % n) / n
    ct = np.cos(ang_t)
    stn = -np.sin(ang_t)
    assert blk % FN_FLIP == 0
    jp = np.zeros((FN_FLIP, FN_FLIP))
    r = np.arange(1, FN_FLIP)
    jp[r, FN_FLIP - r] = 1.0
    tt = np.arange(n)
    ks = blk * (np.arange(nh) + 1)
    ang_k = 2.0 * np.pi * ((ks[:, None] * tt[None, :]) % n) / n
    tc = np.zeros((2 * SUBLANES, n))
    ts = np.zeros((2 * SUBLANES, n))
    tc[:nh] = np.cos(ang_k)
    ts[:nh] = np.sin(ang_k)
    tc[SUBLANES, half] = 1.0
    as_bf16 = lambda a: jnp.asarray(a, dtype=F32).astype(BF16)
    return dict(cc=as_bf16(cc), sc=as_bf16(sc), ct=as_bf16(ct), stn=as_bf16(stn), jp=as_bf16(jp),
                tc=as_bf16(tc), ts=as_bf16(ts))


def _mirror(jp_ref, src, row0):
    n = src.shape[0]
    nsub = n // FN_FLIP
    first = lax.broadcasted_iota(jnp.int32, (FN_FLIP, src.shape[1]), 0) == 0
    out = []
    for a in range(nsub):
        s = nsub - 1 - a
        head = row0 if a == 0 else src[(s + 1) * FN_FLIP:(s + 1) * FN_FLIP + 1].astype(F32)
        out.append(jnp.where(first, head, _dot(jp_ref[...], src[s * FN_FLIP:(s + 1) * FN_FLIP])))
    return jnp.concatenate(out, axis=0)


def _fn_fold_kernel(xj_ref, xm_ref, xr_ref, jp_ref, cc_ref, sc_ref, eo_ref):
    j = pl.program_id(1)
    xj = xj_ref[0]
    row0 = xr_ref[0, 0:1, :].astype(F32) * jnp.where(j == 0, 0.0, 1.0)
    mir = _mirror(jp_ref, xm_ref[0], row0).astype(BF16)
    e = _dot(xj, cc_ref[...]) + _dot(mir, cc_ref[...])
    o = _dot(xj, sc_ref[...]) - _dot(mir, sc_ref[...])
    eo_ref[0] = jnp.concatenate([e, o], axis=1).astype(BF16)


def _fn_aux_kernel(x_ref, tc_ref, ts_ref, cc_ref, sc_ref, o_ref):
    x = x_ref[0]
    xc = _dot(tc_ref[...], x).astype(BF16)
    xs = _dot(ts_ref[...], x).astype(BF16)
    o_ref[0] = _dot(xc, cc_ref[...]) + _dot(xs, sc_ref[...])


def _fn_main_kernel(ct_ref, stn_ref, eo_ref, aux_ref, g_ref, jp_ref, y_ref, *, scale):
    m = pl.program_id(1)
    blk = ct_ref.shape[0]
    nblk = y_ref.shape[1] // blk
    rows = lax.broadcasted_iota(jnp.int32, (blk, BRANCH_W), 0)
    p = _dot(ct_ref[...], eo_ref[0, :, 0:BRANCH_W])
    qn = _dot(stn_ref[...], eo_ref[0, :, BRANCH_W:2 * BRANCH_W])
    sgn = (1 - 2 * ((m * blk + rows) & 1)).astype(F32)
    p = p + sgn * aux_ref[0, SUBLANES:SUBLANES + 1, :]
    lo = pl.multiple_of(m * blk, blk)
    hi = pl.multiple_of((nblk - 1 - m) * blk, blk)
    y_ref[0, pl.ds(lo, blk), :] = ((p + qn) * (_silu(g_ref[0, pl.ds(lo, blk), :].astype(F32)) * scale)).astype(BF16)
    sel = lax.broadcasted_iota(jnp.int32, (2 * SUBLANES, BRANCH_W), 0) == m
    row0 = jnp.sum(jnp.where(sel, aux_ref[0], 0.0), axis=0, keepdims=True)
    y_ref[0, pl.ds(hi, blk), :] = (_mirror(jp_ref, (p - qn).astype(BF16), row0)
                                   * (_silu(g_ref[0, pl.ds(hi, blk), :].astype(F32)) * scale)).astype(BF16)


def _fourier(fn, blk):
    bsz, n, _ = fn.shape
    tb = _fourier_tables(n, blk)
    half = n // 2
    nh = half // blk
    nblk = 2 * nh
    rows16 = n // BF16_ROWS
    full2 = lambda shape: pl.BlockSpec(shape, lambda b, j: (0,) * len(shape))
    eo = pl.pallas_call(
        _fn_fold_kernel,
        grid=(bsz, nh),
        in_specs=[pl.BlockSpec((1, blk, BRANCH_W), lambda b, j: (b, j, 0)),
                  pl.BlockSpec((1, blk, BRANCH_W), lambda b, j: (b, nblk - 1 - j, 0)),
                  pl.BlockSpec((1, BF16_ROWS, BRANCH_W),
                               lambda b, j: (b, jnp.minimum((blk // BF16_ROWS) * (nblk - j), rows16 - 1), 0)),
                  full2((FN_FLIP, FN_FLIP)), full2((BRANCH_W, BRANCH_W)), full2((BRANCH_W, BRANCH_W))],
        out_specs=pl.BlockSpec((1, blk, 2 * BRANCH_W), lambda b, j: (b, j, 0)),
        out_shape=jax.ShapeDtypeStruct((bsz, half, 2 * BRANCH_W), BF16),
        compiler_params=_params("arbitrary", "arbitrary"),
        name="fn_fold",
    )(fn, fn, fn, tb["jp"], tb["cc"], tb["sc"])
    full1 = lambda shape: pl.BlockSpec(shape, lambda b: (0,) * len(shape))
    aux = pl.pallas_call(
        _fn_aux_kernel,
        grid=(bsz,),
        in_specs=[pl.BlockSpec((1, n, BRANCH_W), lambda b: (b, 0, 0)),
                  full1((2 * SUBLANES, n)), full1((2 * SUBLANES, n)),
                  full1((BRANCH_W, BRANCH_W)), full1((BRANCH_W, BRANCH_W))],
        out_specs=pl.BlockSpec((1, 2 * SUBLANES, BRANCH_W), lambda b: (b, 0, 0)),
        out_shape=jax.ShapeDtypeStruct((bsz, 2 * SUBLANES, BRANCH_W), F32),
        compiler_params=_params("arbitrary"),
        name="fn_aux",
    )(fn, tb["tc"], tb["ts"], tb["cc"], tb["sc"])
    return pl.pallas_call(
        functools.partial(_fn_main_kernel, scale=float(1.0 / np.sqrt(n * FN_GD))),
        grid=(bsz, nh),
        in_specs=[pl.BlockSpec((blk, half), lambda b, m: (m, 0)),
                  pl.BlockSpec((blk, half), lambda b, m: (m, 0)),
                  pl.BlockSpec((1, half, 2 * BRANCH_W), lambda b, m: (b, 0, 0)),
                  pl.BlockSpec((1, 2 * SUBLANES, BRANCH_W), lambda b, m: (b, 0, 0)),
                  pl.BlockSpec((1, n, BRANCH_W), lambda b, m: (b, 0, 1)),
                  full2((FN_FLIP, FN_FLIP))],
        out_specs=pl.BlockSpec((1, n, BRANCH_W), lambda b, m: (b, 0, 0)),
        out_shape=jax.ShapeDtypeStruct((bsz, n, BRANCH_W), BF16),
        compiler_params=_params("arbitrary", "arbitrary"),
        name="fn_main",
    )(tb["ct"], tb["stn"], eo, aux, fn, tb["jp"])


def _prepare_weights(norm_w, w_in, w_out, rg_conv_w, rg_conv_b, rg_gate_a_w, rg_gate_a_b, rg_gate_x_w, rg_gate_x_b,
                     rg_lambda, sc_conv_w, ssd_conv_w, ssd_conv_b, ssd_dt_bias, ssd_a_log, ssd_d, ssd_norm_w,
                     final_norm_w):
    depth = w_in.shape[0]
    assert REF_DT_HI - REF_DT_LO == SSD_DT and w_in.shape[2] - SSD_DT + LANES == C_END
    w_a = jnp.concatenate([w_in[:, :, :REF_DT_LO], jnp.tile(w_in[:, :, REF_DT_LO:REF_DT_HI], (1, 1, LANES // SSD_DT))],
                          axis=2).astype(BF16)
    w_b = w_in[:, :, REF_DT_HI:].astype(BF16)
    heads_per_half = RG_HALF // RG_HD
    eye = jnp.eye(heads_per_half, dtype=F32)

    def blockdiag(wg):
        wg = wg.reshape(depth, 2, -1, heads_per_half, RG_HD, RG_HD)
        return jnp.einsum("ldhjio,jk->ldhjiko", wg, eye).reshape(depth, 2, -1, RG_HALF, RG_HALF)

    halves = lambda b: b.reshape(depth, 2, -1, 1, RG_HALF)
    rep = LANES // SSD_DT
    return dict(
        norm_w=norm_w.reshape(depth, 1, D_MODEL), final_w=final_norm_w.reshape(1, D_MODEL),
        w_in_a=w_a, w_in_b=w_b, w_out=w_out.astype(BF16), sc_cw=sc_conv_w,
        rg_cw=rg_conv_w, rg_cb=rg_conv_b.reshape(depth, 1, BRANCH_W),
        rg_wg=(0.5 * jnp.concatenate([blockdiag(rg_gate_a_w), blockdiag(rg_gate_x_w)], axis=-1)).astype(BF16),
        rg_bg=0.5 * jnp.concatenate([halves(rg_gate_a_b), halves(rg_gate_x_b)], axis=-1),
        rg_lam=rg_lambda,
        ssd_cw=ssd_conv_w, ssd_cb=ssd_conv_b.reshape(depth, 1, SSD_XBC),
        ssd_dtb=jnp.tile(ssd_dt_bias.reshape(depth, 1, SSD_DT), (1, 1, rep)),
        ssd_alog=jnp.tile(ssd_a_log.reshape(depth, 1, SSD_DT), (1, 1, rep)),
        ssd_dskip=jnp.repeat(ssd_d, SSD_HD, axis=1).reshape(depth, 1, BRANCH_W),
        ssd_nw=ssd_norm_w.reshape(depth, 1, BRANCH_W),
    )


def _mix(x, mod, pw, l, mod_row, rowlen, tm, tt, blk, interleave, rg_h0, ssd_h0, with_output, final):
    rg, xbc, z, dt, *rest = _inproj(x, mod, pw, l, rowlen, tm, mod_row, scan_only=not with_output)
    y_rg, rg_st = _rg(rg, pw, l, rg_h0, tt, interleave)
    y_ssd, ssd_st = _ssd(xbc, z, dt, pw, l, ssd_h0)
    if not with_output:
        return None, rg_st, ssd_st
    fn, y_sc = rest
    y_fn = _fourier(fn, blk)
    return _outproj(y_rg, y_sc, y_fn, y_ssd, x, mod, pw, l, final, tm, mod_row), rg_st, ssd_st


def kernel(x, c, ctx, c_ctx, ada_w, ada_b, norm_w, w_in, w_out, rg_conv_w, rg_conv_b, rg_gate_a_w, rg_gate_a_b,
           rg_gate_x_w, rg_gate_x_b, rg_lambda, sc_conv_w, ssd_conv_w, ssd_conv_b, ssd_dt_bias, ssd_a_log, ssd_d,
           ssd_norm_w, final_norm_w):
    bsz, seq, _ = x.shape
    ctx_len = ctx.shape[1]
    depth = w_in.shape[0]
    assert bsz + 1 <= SUBLANES and seq % SUBLANES == 0
    pw = _prepare_weights(norm_w, w_in, w_out, rg_conv_w, rg_conv_b, rg_gate_a_w, rg_gate_a_b, rg_gate_x_w,
                          rg_gate_x_b, rg_lambda, sc_conv_w, ssd_conv_w, ssd_conv_b, ssd_dt_bias, ssd_a_log, ssd_d,
                          ssd_norm_w, final_norm_w)
    c8 = jnp.concatenate([c, c_ctx[None, :], jnp.zeros((SUBLANES - bsz - 1, D_MODEL), F32)], axis=0)
    mod = _ada(c8, ada_w, ada_b.reshape(depth, 1, 3 * D_MODEL))
    zeros_rg = jnp.zeros((bsz, SUBLANES, BRANCH_W), F32)
    zeros_ssd = jnp.zeros((bsz, 2, SSD_HEADS // 2, SSD_STATE, LANES), F32)
    tm_lat = min(1024, seq)
    tt_lat = min(1024, seq)
    tt_ctx = min(256, ctx_len)
    blk_lat = min(512, seq // 2)
    blk_ctx = min(512, ctx_len // 2)
    for l in range(depth):
        last = l == depth - 1
        new_ctx, rg_st, ssd_st = _mix(ctx, mod, pw, l, bsz, ctx_len, ctx_len, tt_ctx, blk_ctx, False,
                                      zeros_rg, zeros_ssd, not last, False)
        x, _, _ = _mix(x, mod, pw, l, None, GRID_W, tm_lat, tt_lat, blk_lat, True, rg_st, ssd_st, True, last)
        if not last:
            ctx = new_ctx
    return x
```

```python
import functools

import numpy as np
import jax
import jax.numpy as jnp
from jax import lax
from jax.experimental import pallas as pl
from jax.experimental.pallas import tpu as pltpu

F32 = jnp.float32
BF16 = jnp.bfloat16
HIGHEST = lax.Precision.HIGHEST

D_MODEL = 1024
D_INNER = 2048
BRANCH_W = 512
GRID_W = 64
RG_HD = 64
RG_C = 8.0
RG_HALF = 256
SSD_HEADS = 8
SSD_HD = 64
SSD_GROUPS = 2
SSD_STATE = 64
SSD_XBC = BRANCH_W + 2 * SSD_GROUPS * SSD_STATE
SSD_CHUNK = 128
SSD_DT = 2 * SSD_HEADS
FN_GD = 128
FN_FLIP = 128
NORM_EPS = 1e-6

LANES = 128
SUBLANES = 8
BF16_ROWS = 16
VMEM_LIMIT_BYTES = 60000 * 1024

C_RGX, C_XBC, C_RGG, C_Z, C_SC, C_FN, C_DT, C_END = 0, 512, 1280, 1792, 2304, 4352, 5376, 5504
REF_DT_LO, REF_DT_HI = 1280, 1296


def _sigmoid(v):
    return 0.5 + 0.5 * jnp.tanh(0.5 * v)


def _silu(v):
    h = 0.5 * v
    return h + h * jnp.tanh(h)


def _softplus(v):
    return jnp.maximum(v, 0.0) + jnp.log1p(jnp.exp(-jnp.abs(v)))


def _dot(a, b):
    return jnp.dot(a, b, preferred_element_type=F32)


def _split3(v):
    hi = v.astype(BF16).astype(F32)
    rest = v - hi
    mid = rest.astype(BF16).astype(F32)
    return hi, mid, rest - mid


def _pack3(v):
    hi, mid, lo = _split3(v)
    period = lax.broadcasted_iota(jnp.int32, v.shape, 1) // SSD_DT
    return jnp.where(period == 0, hi, jnp.where(period == 1, mid, jnp.where(period == 2, lo, 0.0))).astype(BF16)


def _params(*semantics):
    return pltpu.CompilerParams(dimension_semantics=semantics, vmem_limit_bytes=VMEM_LIMIT_BYTES)


def _layer_spec(l, shape, single_buffer=False):
    mode = dict(pipeline_mode=pl.Buffered(1)) if single_buffer else {}
    return pl.BlockSpec((None,) + tuple(shape), lambda *_: (l,) + (0,) * len(shape), **mode)


def _ada_kernel(c_ref, w_ref, b_ref, o_ref):
    o_ref[...] = jnp.dot(_silu(c_ref[...]), w_ref[...], precision=HIGHEST,
                         preferred_element_type=F32) + b_ref[...]


def _ada(c8, w, b):
    depth = w.shape[0]
    tn = 512
    return pl.pallas_call(
        _ada_kernel,
        grid=(depth, 3 * D_MODEL // tn),
        in_specs=[pl.BlockSpec((SUBLANES, D_MODEL), lambda l, j: (0, 0)),
                  pl.BlockSpec((None, D_MODEL, tn), lambda l, j: (l, 0, j)),
                  pl.BlockSpec((None, 1, tn), lambda l, j: (l, 0, j))],
        out_specs=pl.BlockSpec((None, SUBLANES, tn), lambda l, j: (l, 0, j)),
        out_shape=jax.ShapeDtypeStruct((depth, SUBLANES, 3 * D_MODEL), F32),
        compiler_params=_params("arbitrary", "arbitrary"),
        name="ada",
    )(c8, w, b)


def _inproj_kernel(*refs, rowlen, mod_row, halo, scan_only):
    if halo:
        x_ref, xp_ref, xn_ref, *refs = refs
    else:
        x_ref, *refs = refs
    nw_ref, sh_ref, sc_ref, wa_ref, wb_ref, cw_ref, rcw_ref, rcb_ref, scw_ref, scb_ref, *outs = refs
    nb, tm, _ = x_ref.shape
    rows = nb * tm
    i = pl.program_id(1)
    row = pl.program_id(0) if mod_row is None else mod_row
    if halo:
        x = jnp.concatenate([xp_ref[0], x_ref[0], xn_ref[0]], axis=0)
    else:
        x = x_ref[...].reshape(rows, D_MODEL)
    ms = jnp.mean(x * x, axis=-1, keepdims=True)
    gain = nw_ref[...] * (1.0 + sc_ref[pl.ds(row, 1), :])
    hx = ((x * lax.rsqrt(ms + NORM_EPS)) * gain + sh_ref[pl.ds(row, 1), :]).astype(BF16)
    hb = hx[halo:halo + rows]

    def wcols(lo, hi):
        if hi <= REF_DT_LO:
            return wa_ref[:, lo:hi]
        if lo >= C_DT:
            return wa_ref[:, lo - C_DT + REF_DT_LO:hi - C_DT + REF_DT_LO]
        assert lo >= REF_DT_LO and hi <= C_DT
        return wb_ref[:, lo - REF_DT_LO:hi - REF_DT_LO]

    def proj(lo, hi):
        return _dot(hb, wcols(lo, hi))

    def proj_halo(lo, hi):
        return _dot(hx, wcols(lo, hi))

    def conv4(p, cw, cb):
        n = p.shape[0]
        if halo:
            keep_prev = jnp.where(i == 0, 0.0, 1.0)
            keep_next = jnp.where(i == pl.num_programs(1) - 1, 0.0, 1.0)
            p = jnp.concatenate([p[:halo] * keep_prev, p[halo:halo + rows], p[halo + rows:] * keep_next], axis=0)
            mid = slice(halo, halo + rows)
            taps = [pltpu.roll(p, 2, 0)[mid], pltpu.roll(p, 1, 0)[mid], p[mid], pltpu.roll(p, n - 1, 0)[mid]]
        else:
            pos = lax.broadcasted_iota(jnp.int32, p.shape, 0) & (tm - 1)
            taps = [jnp.where(pos >= 2, pltpu.roll(p, 2, 0), 0.0), jnp.where(pos >= 1, pltpu.roll(p, 1, 0), 0.0), p,
                    jnp.where(pos <= tm - 2, pltpu.roll(p, n - 1, 0), 0.0)]
        return sum(cw[k:k + 1, :] * taps[k] for k in range(4)) + cb

    def put(ref, value, lo=0):
        ref[:, :, lo:lo + value.shape[1]] = value.reshape(nb, tm, value.shape[1]).astype(ref.dtype)

    stages = []
    if scan_only:
        rg_ref, xbc_ref, z_ref, dt_ref = outs
    else:
        rg_ref, xbc_ref, z_ref, dt_ref, fn_ref, ysc_ref = outs
        conv3 = []

        def sc_conv(ps):
            v = ps[0] * ps[1]
            pos = lax.broadcasted_iota(jnp.int32, (rows, BRANCH_W), 0) & (rowlen - 1)
            vm1 = jnp.where(pos == 0, 0.0, pltpu.roll(v, 1, 0))
            vp1 = jnp.where(pos == rowlen - 1, 0.0, pltpu.roll(v, rows - 1, 0))
            conv3.append(cw_ref[0:1, :] * vm1 + cw_ref[1:2, :] * v + cw_ref[2:3, :] * vp1)

        stages.append((lambda: (proj(C_SC + 512, C_SC + 1024), proj(C_SC + 1024, C_SC + 1536)), sc_conv))
        stages.append((lambda: (proj(C_SC, C_SC + 512), proj(C_SC + 1536, C_FN)),
                       lambda ps: put(ysc_ref, ps[0] * conv3[0] * _silu(ps[1]))))
    stages.append((lambda: proj_halo(C_XBC, C_RGG),
                   lambda p: put(xbc_ref, _silu(conv4(p, scw_ref[...], scb_ref[...])))))
    stages.append((lambda: proj_halo(C_RGX, C_XBC), lambda p: put(rg_ref, conv4(p, rcw_ref[...], rcb_ref[...]))))
    stages.append((lambda: proj(C_RGG, C_Z), lambda p: put(rg_ref, p, BRANCH_W)))
    stages.append((lambda: proj(C_Z, C_SC), lambda p: put(z_ref, p)))
    if not scan_only:
        stages.append((lambda: proj(C_FN, C_DT), lambda p: put(fn_ref, p)))
    stages.append((lambda: proj(C_DT, C_END), lambda p: put(dt_ref, p)))
    pending = None
    for matmuls, epilogue in stages:
        result = matmuls()
        if pending is not None:
            pending[1](pending[0])
        pending = (result, epilogue)
    pending[1](pending[0])


def _inproj(x, mod, pw, l, rowlen, tm, mod_row, scan_only=False):
    bsz, t, _ = x.shape
    whole = tm == t and mod_row is not None and tm & (tm - 1) == 0
    halo = 0 if whole else BF16_ROWS
    nb = bsz if whole else 1
    assert t % tm == 0 and tm % rowlen == 0 and rowlen & (rowlen - 1) == 0 and tm % BF16_ROWS == 0
    per_tile = tm // BF16_ROWS
    last_halo = t // BF16_ROWS - 1
    tok = lambda width: pl.BlockSpec((nb, tm, width), lambda b, i: (b, i, 0))
    lay = lambda *shape: _layer_spec(l, shape)
    out = lambda width, dt: jax.ShapeDtypeStruct((bsz, t, width), dt)
    x_specs, xs = [tok(D_MODEL)], [x]
    if halo:
        x_specs += [pl.BlockSpec((1, halo, D_MODEL), lambda b, i: (b, jnp.maximum(i * per_tile - 1, 0), 0)),
                    pl.BlockSpec((1, halo, D_MODEL), lambda b, i: (b, jnp.minimum((i + 1) * per_tile, last_halo), 0))]
        xs += [x, x]
    widths = [(1024, BF16), (SSD_XBC, BF16), (512, BF16), (LANES, F32)]
    if not scan_only:
        widths += [(1024, BF16), (512, BF16)]
    return pl.pallas_call(
        functools.partial(_inproj_kernel, rowlen=rowlen, mod_row=mod_row, halo=halo, scan_only=scan_only),
        grid=(bsz // nb, t // tm),
        in_specs=x_specs + [lay(1, D_MODEL),
                            pl.BlockSpec((None, SUBLANES, D_MODEL), lambda b, i: (l, 0, 0)),
                            pl.BlockSpec((None, SUBLANES, D_MODEL), lambda b, i: (l, 0, 1)),
                            _layer_spec(l, pw["w_in_a"].shape[1:], single_buffer=True),
                            _layer_spec(l, pw["w_in_b"].shape[1:], single_buffer=True),
                            lay(3, BRANCH_W), lay(4, BRANCH_W), lay(1, BRANCH_W),
                            lay(4, SSD_XBC), lay(1, SSD_XBC)],
        out_specs=[tok(w) for w, _ in widths],
        out_shape=[out(w, dt) for w, dt in widths],
        compiler_params=_params("arbitrary", "arbitrary"),
        name="inproj",
    )(*xs, pw["norm_w"], mod, mod, pw["w_in_a"], pw["w_in_b"], pw["sc_cw"], pw["rg_cw"], pw["rg_cb"], pw["ssd_cw"],
      pw["ssd_cb"])


def _outproj_kernel(yrg_ref, ysc_ref, yfn_ref, yssd_ref, w_ref, x_ref, g_ref, fw_ref, o_ref, *, final, mod_row):
    row = pl.program_id(0) if mod_row is None else mod_row
    nb, tm, _ = x_ref.shape
    flat = lambda ref: ref[...].reshape(nb * tm, ref.shape[2])
    acc = _dot(flat(yrg_ref), w_ref[0:512, :])
    acc += _dot(flat(ysc_ref), w_ref[512:1024, :])
    acc += _dot(flat(yfn_ref), w_ref[1024:1536, :])
    acc += _dot(flat(yssd_ref), w_ref[1536:2048, :])
    xn = flat(x_ref) + g_ref[pl.ds(row, 1), :] * acc
    if final:
        ms = jnp.mean(xn * xn, axis=-1, keepdims=True)
        xn = (xn * lax.rsqrt(ms + NORM_EPS)) * fw_ref[...]
    o_ref[...] = xn.reshape(nb, tm, D_MODEL)


def _outproj(y_rg, y_sc, y_fn, y_ssd, x, mod, pw, l, final, tm, mod_row):
    bsz, t, _ = x.shape
    nb = bsz if (tm == t and mod_row is not None) else 1
    ytok = pl.BlockSpec((nb, tm, BRANCH_W), lambda b, i: (b, i, 0))
    xtok = pl.BlockSpec((nb, tm, D_MODEL), lambda b, i: (b, i, 0))
    return pl.pallas_call(
        functools.partial(_outproj_kernel, final=final, mod_row=mod_row),
        grid=(bsz // nb, t // tm),
        in_specs=[ytok, ytok, ytok, ytok,
                  _layer_spec(l, (D_INNER, D_MODEL), single_buffer=True),
                  xtok,
                  pl.BlockSpec((None, SUBLANES, D_MODEL), lambda b, i: (l, 0, 2)),
                  pl.BlockSpec((1, D_MODEL), lambda b, i: (0, 0))],
        out_specs=xtok,
        out_shape=jax.ShapeDtypeStruct((bsz, t, D_MODEL), F32),
        compiler_params=_params("arbitrary", "arbitrary"),
        name="outproj",
    )(y_rg, y_sc, y_fn, y_ssd, pw["w_out"], x, mod, pw["final_w"])


MIN_NORMAL_F32 = float(np.finfo(np.float32).tiny)
LOG2E = float(np.log2(np.e))


def _rg_coefficients(uh_b, wg, bg_half, lam):
    uh = uh_b.astype(F32)
    th = jnp.tanh(_dot(uh_b, wg) + bg_half)
    k = (-0.5 * RG_C * np.log2(np.e)) * _softplus(-lam)
    a = jnp.exp2(k + k * th[:, :RG_HALF])
    x = 1.0 - a * a
    root = x * lax.rsqrt(jnp.maximum(x, MIN_NORMAL_F32))
    return a, root * (uh + uh * th[:, RG_HALF:])


def _scan_tile(a, v, h_in, sub, reverse):
    rows = a.shape[0]
    for s in (1, 2, 4):
        if reverse:
            keep = sub < SUBLANES - s
            a_sh = jnp.where(keep, pltpu.roll(a, rows - s, 0), 1.0)
            v_sh = jnp.where(keep, pltpu.roll(v, rows - s, 0), 0.0)
        else:
            keep = sub >= s
            a_sh = jnp.where(keep, pltpu.roll(a, s, 0), 1.0)
            v_sh = jnp.where(keep, pltpu.roll(v, s, 0), 0.0)
        v = v + a * v_sh
        a = a * a_sh
    ngroups = rows // SUBLANES
    out = [None] * ngroups
    h = h_in
    order = range(ngroups - 1, -1, -1) if reverse else range(ngroups)
    for g in order:
        lo = g * SUBLANES
        hg = v[lo:lo + SUBLANES] + a[lo:lo + SUBLANES] * h
        out[g] = hg
        h = hg[0:1] if reverse else hg[SUBLANES - 1:SUBLANES]
    return jnp.concatenate(out, axis=0), h


def _rg_kernel(u_ref, g_ref, wg_ref, bg_ref, lam_ref, h0_ref, y_ref, ht_ref, hb_ref, *, t, tt):
    nt = t // tt
    sub = lax.broadcasted_iota(jnp.int32, (tt, RG_HALF), 0) & (SUBLANES - 1)

    def coeffs(s, d):
        r0 = pl.multiple_of(s * tt, tt)
        return _rg_coefficients(u_ref[0, pl.ds(r0, tt), :], wg_ref[d], bg_ref[d], lam_ref[d:d + 1, :])

    def rev_body(i, h):
        s = nt - 1 - i
        a, v = coeffs(s, 1)
        hh, hn = _scan_tile(a, v, h, sub, True)
        hb_ref[pl.ds(pl.multiple_of(s * tt, tt), tt), :] = hh
        return hn

    h_rev = lax.fori_loop(0, nt, rev_body, h0_ref[0, 1:2, :])

    def fwd_body(s, h):
        a, v = coeffs(s, 0)
        hh, hn = _scan_tile(a, v, h, sub, False)
        r0 = pl.multiple_of(s * tt, tt)
        g = g_ref[0, pl.ds(r0, tt), :].astype(F32)
        y_ref[0, pl.ds(r0, tt), :] = ((hh + hb_ref[pl.ds(r0, tt), :]) * _silu(g)).astype(BF16)
        return hn

    h_fwd = lax.fori_loop(0, nt, fwd_body, h0_ref[0, 0:1, :])
    ht_ref[0] = jnp.concatenate([h_fwd, h_rev, jnp.zeros((SUBLANES - 2, RG_HALF), F32)], axis=0)


RG_GATHER = BF16_ROWS


def _rg_interleaved_kernel(u_ref, g_ref, wg_ref, bg_ref, lam_ref, perm_ref, permt_ref, h0_ref, y_ref,
                           s_ref, af_ref, ab_ref, up_ref, *, t, tt):
    nt = t // tt
    ng = tt // SUBLANES
    chunk = t // SUBLANES
    steps = tt // SUBLANES
    nsub = steps // RG_GATHER
    block = SUBLANES * RG_GATHER
    shape = (SUBLANES, RG_HALF)

    def source_rows(k, m, s):
        return pl.ds(pl.multiple_of(s * chunk + k * steps + m * RG_GATHER, RG_GATHER), RG_GATHER)

    def gather(ref, k):
        blocks = []
        for m in range(nsub):
            rows = jnp.concatenate([ref[0, source_rows(k, m, s), :] for s in range(SUBLANES)], axis=0)
            blocks.append(_dot(perm_ref[...], rows).astype(BF16))
        return jnp.concatenate(blocks, axis=0)

    def sweep(d):
        def body(i, carry):
            h, acc = carry
            s = i if d == 0 else nt - 1 - i
            r0 = pl.multiple_of(s * tt, tt)
            if d == 1:
                ub = gather(u_ref, s)
                up_ref[pl.ds(r0, tt), :] = ub
            else:
                ub = up_ref[pl.ds(r0, tt), :]
            a, v = _rg_coefficients(ub, wg_ref[d], bg_ref[d], lam_ref[d:d + 1, :])
            hs, ps = [None] * ng, [None] * ng
            for g in (range(ng) if d == 0 else range(ng - 1, -1, -1)):
                ag = a[g * SUBLANES:(g + 1) * SUBLANES]
                h = ag * h + v[g * SUBLANES:(g + 1) * SUBLANES]
                acc = ag * acc
                hs[g], ps[g] = h, acc
            local = jnp.concatenate(hs, axis=0)
            if d == 1:
                s_ref[pl.ds(r0, tt), :] = local
                ab_ref[pl.ds(r0, tt), :] = jnp.concatenate(ps, axis=0)
            else:
                s_ref[pl.ds(r0, tt), :] += local
                af_ref[pl.ds(r0, tt), :] = jnp.concatenate(ps, axis=0)
            return h, acc

        return lax.fori_loop(0, nt, body, (jnp.zeros(shape, F32), jnp.ones(shape, F32)), unroll=2)

    hb, pb = sweep(1)
    hf, pf = sweep(0)
    cf = [h0_ref[0, 0:1, :]]
    for s in range(SUBLANES - 1):
        cf.append(hf[s:s + 1] + pf[s:s + 1] * cf[s])
    cb = [None] * SUBLANES
    cb[SUBLANES - 1] = h0_ref[0, 1:2, :]
    for s in range(SUBLANES - 1, 0, -1):
        cb[s - 1] = hb[s:s + 1] + pb[s:s + 1] * cb[s]
    cf = jnp.concatenate(cf, axis=0)[None]
    cb = jnp.concatenate(cb, axis=0)[None]

    def fix(s, carry):
        r0 = pl.multiple_of(s * tt, tt)
        corr = (af_ref[pl.ds(r0, tt), :].reshape(ng, SUBLANES, RG_HALF) * cf
                + ab_ref[pl.ds(r0, tt), :].reshape(ng, SUBLANES, RG_HALF) * cb).reshape(tt, RG_HALF)
        g = gather(g_ref, s).astype(F32)
        y = ((s_ref[pl.ds(r0, tt), :] + corr) * _silu(g)).astype(BF16)
        for m in range(nsub):
            back = _dot(permt_ref[...], y[m * block:(m + 1) * block]).astype(BF16)
            for c in range(SUBLANES):
                y_ref[0, source_rows(s, m, c), :] = back[c * RG_GATHER:(c + 1) * RG_GATHER]
        return carry

    lax.fori_loop(0, nt, fix, 0, unroll=2)


def _rg(rg, pw, l, h0, tt, interleaved):
    bsz, t, _ = rg.shape
    assert t % tt == 0
    nhalf = BRANCH_W // RG_HALF
    seq = lambda col0: pl.BlockSpec((1, t, RG_HALF), lambda b, h: (b, 0, col0 + h))
    in_specs = [seq(0), seq(nhalf),
                pl.BlockSpec((None, 2, None, RG_HALF, 2 * RG_HALF), lambda b, h: (l, 0, h, 0, 0)),
                pl.BlockSpec((None, 2, None, 1, 2 * RG_HALF), lambda b, h: (l, 0, h, 0, 0)),
                pl.BlockSpec((None, 2, RG_HALF), lambda b, h: (l, 0, h)),
                pl.BlockSpec((1, SUBLANES, RG_HALF), lambda b, h: (b, 0, h))]
    args = (rg, rg, pw["rg_wg"], pw["rg_bg"], pw["rg_lam"], h0)
    y_spec = pl.BlockSpec((1, t, RG_HALF), lambda b, h: (b, 0, h))
    y_shape = jax.ShapeDtypeStruct((bsz, t, BRANCH_W), BF16)
    seq_scratch = pltpu.VMEM((t, RG_HALF), F32)
    if interleaved:
        block = SUBLANES * RG_GATHER
        assert tt % block == 0
        pos = np.arange(block)
        perm = np.zeros((block, block))
        perm[pos, RG_GATHER * (pos % SUBLANES) + pos // SUBLANES] = 1.0
        perm = jnp.asarray(perm, dtype=F32).astype(BF16)
        pspec = pl.BlockSpec((block, block), lambda b, h: (0, 0))
        y = pl.pallas_call(
            functools.partial(_rg_interleaved_kernel, t=t, tt=tt),
            grid=(bsz, nhalf), in_specs=in_specs[:5] + [pspec, pspec] + in_specs[5:],
            out_specs=y_spec, out_shape=y_shape,
            scratch_shapes=[seq_scratch, seq_scratch, seq_scratch, pltpu.VMEM((t, RG_HALF), BF16)],
            compiler_params=_params("arbitrary", "arbitrary"), name="rglru_interleaved",
        )(*args[:5], perm, perm.T, args[5])
        return y, None
    return pl.pallas_call(
        functools.partial(_rg_kernel, t=t, tt=tt),
        grid=(bsz, nhalf), in_specs=in_specs,
        out_specs=[y_spec, pl.BlockSpec((1, SUBLANES, RG_HALF), lambda b, h: (b, 0, h))],
        out_shape=[y_shape, jax.ShapeDtypeStruct((bsz, SUBLANES, BRANCH_W), F32)],
        scratch_shapes=[seq_scratch],
        compiler_params=_params("arbitrary", "arbitrary"), name="rglru",
    )(*args)


def _ssd_kernel(xbc_ref, z_ref, dt_ref, dtb_ref, alog_ref, dsk_ref, nw_ref, ecol_ref, ehead_ref,
                hmask_ref, h0_ref, y_ref, ht_ref, yb_ref, cs_ref, adjt_ref, diagt_ref, wgtt_ref, etot_ref,
                loc_ref, st_ref, *, t):
    q = SSD_CHUNK
    nc = t // q
    npairs = SSD_HEADS // 2
    pairs_per_group = npairs // SSD_GROUPS
    lane1 = lax.broadcasted_iota(jnp.int32, (1, LANES), 1)
    a_row = -LOG2E * jnp.exp(alog_ref[...])
    fwd_lane = (lax.broadcasted_iota(jnp.int32, (q, LANES), 1) & (SSD_DT - 1)) < SSD_HEADS
    ri = lax.broadcasted_iota(jnp.int32, (q, q), 0)
    ci = lax.broadcasted_iota(jnp.int32, (q, q), 1)
    lower_b = jnp.where(ri >= ci, 1.0, 0.0).astype(BF16)
    head0_s = lax.broadcasted_iota(jnp.int32, (SSD_STATE, LANES), 1) < SSD_HD

    def pair_vec(row, l0, l1):
        return jnp.where(lane1 < SSD_HD, row[:, l0:l0 + 1], row[:, l1:l1 + 1])

    def small_rows(c):
        return pl.ds(pl.multiple_of(c * SSD_DT, SSD_DT), SSD_DT)

    def pass0(c, carry):
        r0 = pl.multiple_of(c * q, q)
        dtv = _softplus(dt_ref[0, pl.ds(r0, q), :] + dtb_ref[...])
        da = dtv * a_row
        pre = sum(_dot(lower_b, part.astype(BF16)) for part in _split3(da))
        tot = pre[q - 1:q, :]
        cs = jnp.where(fwd_lane, pre, tot - pre + da)
        cs_ref[pl.ds(r0, q), :] = cs
        adjt_ref[small_rows(c), :] = (cs - LOG2E * jnp.log(dtv)).T[:SSD_DT]
        diagt_ref[small_rows(c), :] = (LOG2E * jnp.log(dtv + pltpu.roll(dtv, SSD_HEADS, 1))).T[:SSD_DT]
        wgtt_ref[small_rows(c), :] = (dtv * jnp.exp2(tot - cs)).T[:SSD_DT]
        etot_ref[small_rows(c), :] = jnp.broadcast_to(jnp.exp2(tot), (SSD_DT, LANES))
        return carry

    def pass1(i, carry):
        c = nc - 1 - i
        r0 = pl.multiple_of(c * q, q)
        xs_b = xbc_ref[0, pl.ds(r0, q), 0:BRANCH_W]
        bm_b = xbc_ref[0, pl.ds(r0, q), BRANCH_W:BRANCH_W + LANES]
        cm_b = xbc_ref[0, pl.ds(r0, q), BRANCH_W + LANES:SSD_XBC]
        cs_p = _pack3(cs_ref[pl.ds(r0, q), :])
        cs_col = _dot(cs_p, ecol_ref[...])
        ecs_b = jnp.exp2(_dot(cs_p, ehead_ref[:, BRANCH_W:]))
        adjt = adjt_ref[small_rows(c), :]
        diagt = diagt_ref[small_rows(c), :]
        wgtt = wgtt_ref[small_rows(c), :]
        etot = etot_ref[pl.ds(pl.multiple_of(c * SSD_DT, SSD_DT), 1), :]
        bt = bm_b.astype(F32).T
        xs_h0 = xs_b * hmask_ref[0:1, :]
        xs_h1 = xs_b * hmask_ref[1:2, :]
        ys = []
        for g in range(SSD_GROUPS):
            gs = slice(g * SSD_STATE, (g + 1) * SSD_STATE)
            cbm = lax.dot_general(cm_b[:, gs], bm_b[:, gs], (((1,), (1,)), ((), ())), preferred_element_type=F32)
            btg = bt[gs, :]
            for pp in range(pairs_per_group):
                pair = g * pairs_per_group + pp
                ps = slice(pair * LANES, (pair + 1) * LANES)
                state = st_ref[pair]
                lf0 = 2 * pair
                lb0 = SSD_HEADS + lf0
                ms = []
                for lf in (lf0, lf0 + 1):
                    lb = SSD_HEADS + lf
                    arg = jnp.where(ri > ci, cs_col[:, lf * LANES:(lf + 1) * LANES] - adjt[lf:lf + 1, :],
                                    jnp.where(ri < ci, cs_col[:, lb * LANES:(lb + 1) * LANES] - adjt[lb:lb + 1, :],
                                              diagt[lf:lf + 1, :]))
                    ms.append((cbm * jnp.exp2(arg)).astype(BF16))
                yd = _dot(jnp.concatenate(ms, axis=1), jnp.concatenate([xs_h0[:, ps], xs_h1[:, ps]], axis=0))
                yo = _dot(cm_b[:, gs], state.astype(BF16)) * ecs_b[:, ps]
                ys.append(yd + yo)
                stack = jnp.concatenate([(btg * wgtt[l:l + 1, :]).astype(BF16) for l in (lf0, lf0 + 1, lb0, lb0 + 1)],
                                        axis=0)
                res = _dot(stack, xs_b[:, ps])
                n = SSD_STATE
                loc_ref[c, pair] = jnp.where(head0_s, res[0:n], res[n:2 * n])
                st_ref[pair] = state * pair_vec(etot, lb0, lb0 + 1) + jnp.where(head0_s, res[2 * n:3 * n], res[3 * n:])
        yb_ref[pl.ds(r0, q), :] = jnp.concatenate(ys, axis=1) + dsk_ref[...] * xs_b.astype(F32)
        return carry

    def pass2(c, carry):
        r0 = pl.multiple_of(c * q, q)
        cs = cs_ref[pl.ds(r0, q), :]
        cm_b = xbc_ref[0, pl.ds(r0, q), BRANCH_W + LANES:SSD_XBC]
        ecs_f = jnp.exp2(_dot(_pack3(cs), ehead_ref[:, :BRANCH_W]))
        etot = jnp.exp2(cs[q - 1:q, :])
        ys = []
        for pair in range(npairs):
            g = pair // pairs_per_group
            state = st_ref[pair]
            lf0 = 2 * pair
            ys.append(_dot(cm_b[:, g * SSD_STATE:(g + 1) * SSD_STATE], state.astype(BF16))
                      * ecs_f[:, pair * LANES:(pair + 1) * LANES])
            st_ref[pair] = state * pair_vec(etot, lf0, lf0 + 1) + loc_ref[c, pair]
        yt = yb_ref[pl.ds(r0, q), :] + jnp.concatenate(ys, axis=1)
        yt = yt * _silu(z_ref[0, pl.ds(r0, q), :].astype(F32))
        ms = jnp.mean(yt * yt, axis=-1, keepdims=True)
        y_ref[0, pl.ds(r0, q), :] = ((yt * lax.rsqrt(ms + NORM_EPS)) * nw_ref[...]).astype(BF16)
        return carry

    lax.fori_loop(0, nc, pass0, 0, unroll=4 if nc % 4 == 0 else 1)
    for p in range(npairs):
        st_ref[p] = h0_ref[0, 1, p]
    lax.fori_loop(0, nc, pass1, 0, unroll=4 if nc % 4 == 0 else 2)
    for p in range(npairs):
        ht_ref[0, 1, p] = st_ref[p]
        st_ref[p] = h0_ref[0, 0, p]
    lax.fori_loop(0, nc, pass2, 0, unroll=4)
    for p in range(npairs):
        ht_ref[0, 0, p] = st_ref[p]


@functools.lru_cache(maxsize=None)
def _ssd_tables():
    lanes = np.arange(LANES)[:, None]
    src = np.where(lanes < 3 * SSD_DT, lanes % SSD_DT, -1)
    ecol = (src == np.arange(SSD_DT * LANES)[None, :] // LANES)
    ehead = (src == np.arange(2 * BRANCH_W)[None, :] // SSD_HD)
    even_head = (np.arange(BRANCH_W) // SSD_HD) % 2 == 0
    hmask = np.zeros((BF16_ROWS, BRANCH_W))
    hmask[0] = even_head
    hmask[1] = ~even_head
    as_bf16 = lambda a: jnp.asarray(a, dtype=F32).astype(BF16)
    return as_bf16(ecol), as_bf16(ehead), as_bf16(hmask)


def _ssd(xbc, z, dt, pw, l, h0):
    bsz, t, _ = xbc.shape
    assert t % SSD_CHUNK == 0
    npairs = SSD_HEADS // 2
    ecol, ehead, hmask = _ssd_tables()
    full = lambda shape: pl.BlockSpec(shape, lambda b: (0,) * len(shape))
    lay = lambda *shape: _layer_spec(l, shape)
    seq = lambda width: pl.BlockSpec((1, t, width), lambda b: (b, 0, 0))
    st_spec = pl.BlockSpec((1, 2, npairs, SSD_STATE, LANES), lambda b: (b, 0, 0, 0, 0))
    return pl.pallas_call(
        functools.partial(_ssd_kernel, t=t),
        grid=(bsz,),
        in_specs=[seq(SSD_XBC), seq(BRANCH_W), seq(LANES), lay(1, LANES), lay(1, LANES),
                  lay(1, BRANCH_W), lay(1, BRANCH_W), full(ecol.shape), full(ehead.shape), full(hmask.shape),
                  st_spec],
        out_specs=[seq(BRANCH_W), st_spec],
        out_shape=[jax.ShapeDtypeStruct((bsz, t, BRANCH_W), BF16),
                   jax.ShapeDtypeStruct((bsz, 2, npairs, SSD_STATE, LANES), F32)],
        scratch_shapes=[pltpu.VMEM((t, BRANCH_W), F32), pltpu.VMEM((t, LANES), F32)]
        + [pltpu.VMEM((t // SSD_CHUNK * SSD_DT, LANES), F32)] * 4
        + [pltpu.VMEM((t // SSD_CHUNK, npairs, SSD_STATE, LANES), F32),
                        pltpu.VMEM((npairs, SSD_STATE, LANES), F32)],
        compiler_params=_params("arbitrary"),
        name="ssd",
    )(xbc, z, dt, pw["ssd_dtb"], pw["ssd_alog"], pw["ssd_dskip"], pw["ssd_nw"], ecol, ehead, hmask, h0)


@functools.lru_cache(maxsize=None)
def _fourier_tables(n, blk):
    half = n // 2
    nh = half // blk
    assert nh * blk == half and nh <= SUBLANES
    p = np.arange(FN_GD)
    ang_c = 2.0 * np.pi * ((p[:, None] * p[None, :]) % FN_GD) / FN_GD
    eye = np.eye(BRANCH_W // FN_GD)
    cc = np.kron(eye, np.cos(ang_c))
    sc = np.kron(eye, np.sin(ang_c))
    k = np.arange(half)
    ang_t = 2.0 * np.pi * ((k[:, None] * k[None, :]) % n) / n
    ct = np.cos(ang_t)
    stn = -np.sin(ang_t)
    assert blk % FN_FLIP == 0
    jp = np.zeros((FN_FLIP, FN_FLIP))
    r = np.arange(1, FN_FLIP)
    jp[r, FN_FLIP - r] = 1.0
    tt = np.arange(n)
    ks = blk * (np.arange(nh) + 1)
    ang_k = 2.0 * np.pi * ((ks[:, None] * tt[None, :]) % n) / n
    tc = np.zeros((2 * SUBLANES, n))
    ts = np.zeros((2 * SUBLANES, n))
    tc[:nh] = np.cos(ang_k)
    ts[:nh] = np.sin(ang_k)
    tc[SUBLANES, half] = 1.0
    as_bf16 = lambda a: jnp.asarray(a, dtype=F32).astype(BF16)
    return dict(cc=as_bf16(cc), sc=as_bf16(sc), ct=as_bf16(ct), stn=as_bf16(stn), jp=as_bf16(jp),
                tc=as_bf16(tc), ts=as_bf16(ts))


def _mirror(jp_ref, src, row0):
    n = src.shape[0]
    nsub = n // FN_FLIP
    first = lax.broadcasted_iota(jnp.int32, (FN_FLIP, src.shape[1]), 0) == 0
    out = []
    for a in range(nsub):
        s = nsub - 1 - a
        head = row0 if a == 0 else src[(s + 1) * FN_FLIP:(s + 1) * FN_FLIP + 1].astype(F32)
        out.append(jnp.where(first, head, _dot(jp_ref[...], src[s * FN_FLIP:(s + 1) * FN_FLIP])))
    return jnp.concatenate(out, axis=0)


def _fn_fold_kernel(xj_ref, xm_ref, xr_ref, jp_ref, cc_ref, sc_ref, eo_ref):
    j = pl.program_id(1)
    xj = xj_ref[0]
    row0 = xr_ref[0, 0:1, :].astype(F32) * jnp.where(j == 0, 0.0, 1.0)
    mir = _mirror(jp_ref, xm_ref[0], row0).astype(BF16)
    e = _dot(xj, cc_ref[...]) + _dot(mir, cc_ref[...])
    o = _dot(xj, sc_ref[...]) - _dot(mir, sc_ref[...])
    eo_ref[0] = jnp.concatenate([e, o], axis=1).astype(BF16)


def _fn_aux_kernel(x_ref, tc_ref, ts_ref, cc_ref, sc_ref, o_ref):
    x = x_ref[0]
    xc = _dot(tc_ref[...], x).astype(BF16)
    xs = _dot(ts_ref[...], x).astype(BF16)
    o_ref[0] = _dot(xc, cc_ref[...]) + _dot(xs, sc_ref[...])


def _fn_main_kernel(ct_ref, stn_ref, eo_ref, aux_ref, g_ref, jp_ref, y_ref, *, scale):
    m = pl.program_id(1)
    blk = ct_ref.shape[0]
    nblk = y_ref.shape[1] // blk
    rows = lax.broadcasted_iota(jnp.int32, (blk, BRANCH_W), 0)
    p = _dot(ct_ref[...], eo_ref[0, :, 0:BRANCH_W])
    qn = _dot(stn_ref[...], eo_ref[0, :, BRANCH_W:2 * BRANCH_W])
    sgn = (1 - 2 * ((m * blk + rows) & 1)).astype(F32)
    p = p + sgn * aux_ref[0, SUBLANES:SUBLANES + 1, :]
    lo = pl.multiple_of(m * blk, blk)
    hi = pl.multiple_of((nblk - 1 - m) * blk, blk)
    y_ref[0, pl.ds(lo, blk), :] = ((p + qn) * (_silu(g_ref[0, pl.ds(lo, blk), :].astype(F32)) * scale)).astype(BF16)
    sel = lax.broadcasted_iota(jnp.int32, (2 * SUBLANES, BRANCH_W), 0) == m
    row0 = jnp.sum(jnp.where(sel, aux_ref[0], 0.0), axis=0, keepdims=True)
    y_ref[0, pl.ds(hi, blk), :] = (_mirror(jp_ref, (p - qn).astype(BF16), row0)
                                   * (_silu(g_ref[0, pl.ds(hi, blk), :].astype(F32)) * scale)).astype(BF16)


def _fourier(fn, blk):
    bsz, n, _ = fn.shape
    tb = _fourier_tables(n, blk)
    half = n // 2
    nh = half // blk
    nblk = 2 * nh
    rows16 = n // BF16_ROWS
    full2 = lambda shape: pl.BlockSpec(shape, lambda b, j: (0,) * len(shape))
    eo = pl.pallas_call(
        _fn_fold_kernel,
        grid=(bsz, nh),
        in_specs=[pl.BlockSpec((1, blk, BRANCH_W), lambda b, j: (b, j, 0)),
                  pl.BlockSpec((1, blk, BRANCH_W), lambda b, j: (b, nblk - 1 - j, 0)),
                  pl.BlockSpec((1, BF16_ROWS, BRANCH_W),
                               lambda b, j: (b, jnp.minimum((blk // BF16_ROWS) * (nblk - j), rows16 - 1), 0)),
                  full2((FN_FLIP, FN_FLIP)), full2((BRANCH_W, BRANCH_W)), full2((BRANCH_W, BRANCH_W))],
        out_specs=pl.BlockSpec((1, blk, 2 * BRANCH_W), lambda b, j: (b, j, 0)),
        out_shape=jax.ShapeDtypeStruct((bsz, half, 2 * BRANCH_W), BF16),
        compiler_params=_params("arbitrary", "arbitrary"),
        name="fn_fold",
    )(fn, fn, fn, tb["jp"], tb["cc"], tb["sc"])
    full1 = lambda shape: pl.BlockSpec(shape, lambda b: (0,) * len(shape))
    aux = pl.pallas_call(
        _fn_aux_kernel,
        grid=(bsz,),
        in_specs=[pl.BlockSpec((1, n, BRANCH_W), lambda b: (b, 0, 0)),
                  full1((2 * SUBLANES, n)), full1((2 * SUBLANES, n)),
                  full1((BRANCH_W, BRANCH_W)), full1((BRANCH_W, BRANCH_W))],
        out_specs=pl.BlockSpec((1, 2 * SUBLANES, BRANCH_W), lambda b: (b, 0, 0)),
        out_shape=jax.ShapeDtypeStruct((bsz, 2 * SUBLANES, BRANCH_W), F32),
        compiler_params=_params("arbitrary"),
        name="fn_aux",
    )(fn, tb["tc"], tb["ts"], tb["cc"], tb["sc"])
    return pl.pallas_call(
        functools.partial(_fn_main_kernel, scale=float(1.0 / np.sqrt(n * FN_GD))),
        grid=(bsz, nh),
        in_specs=[pl.BlockSpec((blk, half), lambda b, m: (m, 0)),
                  pl.BlockSpec((blk, half), lambda b, m: (m, 0)),
                  pl.BlockSpec((1, half, 2 * BRANCH_W), lambda b, m: (b, 0, 0)),
                  pl.BlockSpec((1, 2 * SUBLANES, BRANCH_W), lambda b, m: (b, 0, 0)),
                  pl.BlockSpec((1, n, BRANCH_W), lambda b, m: (b, 0, 1)),
                  full2((FN_FLIP, FN_FLIP))],
        out_specs=pl.BlockSpec((1, n, BRANCH_W), lambda b, m: (b, 0, 0)),
        out_shape=jax.ShapeDtypeStruct((bsz, n, BRANCH_W), BF16),
        compiler_params=_params("arbitrary", "arbitrary"),
        name="fn_main",
    )(tb["ct"], tb["stn"], eo, aux, fn, tb["jp"])


def _prepare_weights(norm_w, w_in, w_out, rg_conv_w, rg_conv_b, rg_gate_a_w, rg_gate_a_b, rg_gate_x_w, rg_gate_x_b,
                     rg_lambda, sc_conv_w, ssd_conv_w, ssd_conv_b, ssd_dt_bias, ssd_a_log, ssd_d, ssd_norm_w,
                     final_norm_w):
    depth = w_in.shape[0]
    assert REF_DT_HI - REF_DT_LO == SSD_DT and w_in.shape[2] - SSD_DT + LANES == C_END
    w_a = jnp.concatenate([w_in[:, :, :REF_DT_LO], jnp.tile(w_in[:, :, REF_DT_LO:REF_DT_HI], (1, 1, LANES // SSD_DT))],
                          axis=2).astype(BF16)
    w_b = w_in[:, :, REF_DT_HI:].astype(BF16)
    heads_per_half = RG_HALF // RG_HD
    eye = jnp.eye(heads_per_half, dtype=F32)

    def blockdiag(wg):
        wg = wg.reshape(depth, 2, -1, heads_per_half, RG_HD, RG_HD)
        return jnp.einsum("ldhjio,jk->ldhjiko", wg, eye).reshape(depth, 2, -1, RG_HALF, RG_HALF)

    halves = lambda b: b.reshape(depth, 2, -1, 1, RG_HALF)
    rep = LANES // SSD_DT
    return dict(
        norm_w=norm_w.reshape(depth, 1, D_MODEL), final_w=final_norm_w.reshape(1, D_MODEL),
        w_in_a=w_a, w_in_b=w_b, w_out=w_out.astype(BF16), sc_cw=sc_conv_w,
        rg_cw=0.5 * rg_conv_w, rg_cb=0.5 * rg_conv_b.reshape(depth, 1, BRANCH_W),
        rg_wg=jnp.concatenate([blockdiag(rg_gate_a_w), blockdiag(rg_gate_x_w)], axis=-1).astype(BF16),
        rg_bg=0.5 * jnp.concatenate([halves(rg_gate_a_b), halves(rg_gate_x_b)], axis=-1),
        rg_lam=rg_lambda,
        ssd_cw=ssd_conv_w, ssd_cb=ssd_conv_b.reshape(depth, 1, SSD_XBC),
        ssd_dtb=jnp.tile(ssd_dt_bias.reshape(depth, 1, SSD_DT), (1, 1, rep)),
        ssd_alog=jnp.tile(ssd_a_log.reshape(depth, 1, SSD_DT), (1, 1, rep)),
        ssd_dskip=jnp.repeat(ssd_d, SSD_HD, axis=1).reshape(depth, 1, BRANCH_W),
        ssd_nw=ssd_norm_w.reshape(depth, 1, BRANCH_W),
    )


def _mix(x, mod, pw, l, mod_row, rowlen, tm, tt, blk, interleave, rg_h0, ssd_h0, with_output, final):
    rg, xbc, z, dt, *rest = _inproj(x, mod, pw, l, rowlen, tm, mod_row, scan_only=not with_output)
    y_rg, rg_st = _rg(rg, pw, l, rg_h0, tt, interleave)
    y_ssd, ssd_st = _ssd(xbc, z, dt, pw, l, ssd_h0)
    if not with_output:
        return None, rg_st, ssd_st
    fn, y_sc = rest
    y_fn = _fourier(fn, blk)
    return _outproj(y_rg, y_sc, y_fn, y_ssd, x, mod, pw, l, final, tm, mod_row), rg_st, ssd_st


def kernel(x, c, ctx, c_ctx, ada_w, ada_b, norm_w, w_in, w_out, rg_conv_w, rg_conv_b, rg_gate_a_w, rg_gate_a_b,
           rg_gate_x_w, rg_gate_x_b, rg_lambda, sc_conv_w, ssd_conv_w, ssd_conv_b, ssd_dt_bias, ssd_a_log, ssd_d,
           ssd_norm_w, final_norm_w):
    bsz, seq, _ = x.shape
    ctx_len = ctx.shape[1]
    depth = w_in.shape[0]
    assert bsz + 1 <= SUBLANES and seq % SUBLANES == 0
    pw = _prepare_weights(norm_w, w_in, w_out, rg_conv_w, rg_conv_b, rg_gate_a_w, rg_gate_a_b, rg_gate_x_w,
                          rg_gate_x_b, rg_lambda, sc_conv_w, ssd_conv_w, ssd_conv_b, ssd_dt_bias, ssd_a_log, ssd_d,
                          ssd_norm_w, final_norm_w)
    c8 = jnp.concatenate([c, c_ctx[None, :], jnp.zeros((SUBLANES - bsz - 1, D_MODEL), F32)], axis=0)
    mod = _ada(c8, ada_w, ada_b.reshape(depth, 1, 3 * D_MODEL))
    zeros_rg = jnp.zeros((bsz, SUBLANES, BRANCH_W), F32)
    zeros_ssd = jnp.zeros((bsz, 2, SSD_HEADS // 2, SSD_STATE, LANES), F32)
    tm_lat = min(1024, seq)
    tt_lat = min(1024, seq)
    tt_ctx = min(256, ctx_len)
    blk_lat = min(512, seq // 2)
    blk_ctx = min(512, ctx_len // 2)
    for l in range(depth):
        last = l == depth - 1
        new_ctx, rg_st, ssd_st = _mix(ctx, mod, pw, l, bsz, ctx_len, ctx_len, tt_ctx, blk_ctx, False,
                                      zeros_rg, zeros_ssd, not last, False)
        x, _, _ = _mix(x, mod, pw, l, None, GRID_W, tm_lat, tt_lat, blk_lat, True, rg_st, ssd_st, True, last)
        if not last:
            ctx = new_ctx
    return x
```

```python
import functools

import numpy as np
import jax
import jax.numpy as jnp
from jax import lax
from jax.experimental import pallas as pl
from jax.experimental.pallas import tpu as pltpu

F32 = jnp.float32
BF16 = jnp.bfloat16
HIGHEST = lax.Precision.HIGHEST

D_MODEL = 1024
D_INNER = 2048
BRANCH_W = 512
GRID_W = 64
RG_HD = 64
RG_C = 8.0
RG_HALF = 256
SSD_HEADS = 8
SSD_HD = 64
SSD_GROUPS = 2
SSD_STATE = 64
SSD_XBC = BRANCH_W + 2 * SSD_GROUPS * SSD_STATE
SSD_CHUNK = 128
SSD_DT = 2 * SSD_HEADS
FN_GD = 128
FN_FLIP = 128
NORM_EPS = 1e-6

LANES = 128
SUBLANES = 8
BF16_ROWS = 16
VMEM_LIMIT_BYTES = 60000 * 1024

C_RGX, C_XBC, C_RGG, C_Z, C_SC, C_FN, C_DT, C_END = 0, 512, 1280, 1792, 2304, 4352, 5376, 5504
REF_DT_LO, REF_DT_HI = 1280, 1296


def _sigmoid(v):
    return 0.5 + 0.5 * jnp.tanh(0.5 * v)


def _silu(v):
    h = 0.5 * v
    return h + h * jnp.tanh(h)


def _softplus(v):
    return jnp.maximum(v, 0.0) + jnp.log1p(jnp.exp(-jnp.abs(v)))


def _dot(a, b):
    return jnp.dot(a, b, preferred_element_type=F32)


def _split3(v):
    hi = v.astype(BF16).astype(F32)
    rest = v - hi
    mid = rest.astype(BF16).astype(F32)
    return hi, mid, rest - mid


def _pack3(v):
    hi, mid, lo = _split3(v)
    period = lax.broadcasted_iota(jnp.int32, v.shape, 1) // SSD_DT
    return jnp.where(period == 0, hi, jnp.where(period == 1, mid, jnp.where(period == 2, lo, 0.0))).astype(BF16)


def _params(*semantics):
    return pltpu.CompilerParams(dimension_semantics=semantics, vmem_limit_bytes=VMEM_LIMIT_BYTES)


def _layer_spec(l, shape, single_buffer=False):
    mode = dict(pipeline_mode=pl.Buffered(1)) if single_buffer else {}
    return pl.BlockSpec((None,) + tuple(shape), lambda *_: (l,) + (0,) * len(shape), **mode)


def _ada_kernel(c_ref, w_ref, b_ref, o_ref):
    o_ref[...] = jnp.dot(_silu(c_ref[...]), w_ref[...], precision=HIGHEST,
                         preferred_element_type=F32) + b_ref[...]


def _ada(c8, w, b):
    depth = w.shape[0]
    tn = 512
    return pl.pallas_call(
        _ada_kernel,
        grid=(depth, 3 * D_MODEL // tn),
        in_specs=[pl.BlockSpec((SUBLANES, D_MODEL), lambda l, j: (0, 0)),
                  pl.BlockSpec((None, D_MODEL, tn), lambda l, j: (l, 0, j)),
                  pl.BlockSpec((None, 1, tn), lambda l, j: (l, 0, j))],
        out_specs=pl.BlockSpec((None, SUBLANES, tn), lambda l, j: (l, 0, j)),
        out_shape=jax.ShapeDtypeStruct((depth, SUBLANES, 3 * D_MODEL), F32),
        compiler_params=_params("arbitrary", "arbitrary"),
        name="ada",
    )(c8, w, b)


def _inproj_kernel(*refs, rowlen, mod_row, halo, scan_only):
    if halo:
        x_ref, xp_ref, xn_ref, *refs = refs
    else:
        x_ref, *refs = refs
    nw_ref, sh_ref, sc_ref, wa_ref, wb_ref, cw_ref, rcw_ref, rcb_ref, scw_ref, scb_ref, *outs = refs
    nb, tm, _ = x_ref.shape
    rows = nb * tm
    i = pl.program_id(1)
    row = pl.program_id(0) if mod_row is None else mod_row
    if halo:
        x = jnp.concatenate([xp_ref[0], x_ref[0], xn_ref[0]], axis=0)
    else:
        x = x_ref[...].reshape(rows, D_MODEL)
    ms = jnp.mean(x * x, axis=-1, keepdims=True)
    gain = nw_ref[...] * (1.0 + sc_ref[pl.ds(row, 1), :])
    hx = ((x * lax.rsqrt(ms + NORM_EPS)) * gain + sh_ref[pl.ds(row, 1), :]).astype(BF16)
    hb = hx[halo:halo + rows]

    def wcols(lo, hi):
        if hi <= REF_DT_LO:
            return wa_ref[:, lo:hi]
        if lo >= C_DT:
            return wa_ref[:, lo - C_DT + REF_DT_LO:hi - C_DT + REF_DT_LO]
        assert lo >= REF_DT_LO and hi <= C_DT
        return wb_ref[:, lo - REF_DT_LO:hi - REF_DT_LO]

    def proj(lo, hi):
        return _dot(hb, wcols(lo, hi))

    def proj_halo(lo, hi):
        return _dot(hx, wcols(lo, hi))

    def conv4(p, cw, cb):
        n = p.shape[0]
        if halo:
            keep_prev = jnp.where(i == 0, 0.0, 1.0)
            keep_next = jnp.where(i == pl.num_programs(1) - 1, 0.0, 1.0)
            p = jnp.concatenate([p[:halo] * keep_prev, p[halo:halo + rows], p[halo + rows:] * keep_next], axis=0)
            mid = slice(halo, halo + rows)
            taps = [pltpu.roll(p, 2, 0)[mid], pltpu.roll(p, 1, 0)[mid], p[mid], pltpu.roll(p, n - 1, 0)[mid]]
        else:
            pos = lax.broadcasted_iota(jnp.int32, p.shape, 0) & (tm - 1)
            taps = [jnp.where(pos >= 2, pltpu.roll(p, 2, 0), 0.0), jnp.where(pos >= 1, pltpu.roll(p, 1, 0), 0.0), p,
                    jnp.where(pos <= tm - 2, pltpu.roll(p, n - 1, 0), 0.0)]
        return sum(cw[k:k + 1, :] * taps[k] for k in range(4)) + cb

    def put(ref, value, lo=0):
        ref[:, :, lo:lo + value.shape[1]] = value.reshape(nb, tm, value.shape[1]).astype(ref.dtype)

    stages = []
    if scan_only:
        rg_ref, xbc_ref, z_ref, dt_ref = outs
    else:
        rg_ref, xbc_ref, z_ref, dt_ref, fn_ref, ysc_ref = outs
        conv3 = []

        def sc_conv(ps):
            v = ps[0] * ps[1]
            pos = lax.broadcasted_iota(jnp.int32, (rows, BRANCH_W), 0) & (rowlen - 1)
            vm1 = jnp.where(pos == 0, 0.0, pltpu.roll(v, 1, 0))
            vp1 = jnp.where(pos == rowlen - 1, 0.0, pltpu.roll(v, rows - 1, 0))
            conv3.append(cw_ref[0:1, :] * vm1 + cw_ref[1:2, :] * v + cw_ref[2:3, :] * vp1)

        stages.append((lambda: (proj(C_SC + 512, C_SC + 1024), proj(C_SC + 1024, C_SC + 1536)), sc_conv))
        stages.append((lambda: (proj(C_SC, C_SC + 512), proj(C_SC + 1536, C_FN)),
                       lambda ps: put(ysc_ref, ps[0] * conv3[0] * _silu(ps[1]))))
    stages.append((lambda: proj_halo(C_XBC, C_RGG),
                   lambda p: put(xbc_ref, _silu(conv4(p, scw_ref[...], scb_ref[...])))))
    stages.append((lambda: proj_halo(C_RGX, C_XBC), lambda p: put(rg_ref, conv4(p, rcw_ref[...], rcb_ref[...]))))
    stages.append((lambda: proj(C_RGG, C_Z), lambda p: put(rg_ref, p, BRANCH_W)))
    stages.append((lambda: proj(C_Z, C_SC), lambda p: put(z_ref, p)))
    if not scan_only:
        stages.append((lambda: proj(C_FN, C_DT), lambda p: put(fn_ref, p)))
    stages.append((lambda: proj(C_DT, C_END), lambda p: put(dt_ref, p)))
    pending = None
    for matmuls, epilogue in stages:
        result = matmuls()
        if pending is not None:
            pending[1](pending[0])
        pending = (result, epilogue)
    pending[1](pending[0])


def _inproj(x, mod, pw, l, rowlen, tm, mod_row, scan_only=False):
    bsz, t, _ = x.shape
    whole = tm == t and mod_row is not None and tm & (tm - 1) == 0
    halo = 0 if whole else BF16_ROWS
    nb = bsz if whole else 1
    assert t % tm == 0 and tm % rowlen == 0 and rowlen & (rowlen - 1) == 0 and tm % BF16_ROWS == 0
    per_tile = tm // BF16_ROWS
    last_halo = t // BF16_ROWS - 1
    tok = lambda width: pl.BlockSpec((nb, tm, width), lambda b, i: (b, i, 0))
    lay = lambda *shape: _layer_spec(l, shape)
    out = lambda width, dt: jax.ShapeDtypeStruct((bsz, t, width), dt)
    x_specs, xs = [tok(D_MODEL)], [x]
    if halo:
        x_specs += [pl.BlockSpec((1, halo, D_MODEL), lambda b, i: (b, jnp.maximum(i * per_tile - 1, 0), 0)),
                    pl.BlockSpec((1, halo, D_MODEL), lambda b, i: (b, jnp.minimum((i + 1) * per_tile, last_halo), 0))]
        xs += [x, x]
    widths = [(1024, BF16), (SSD_XBC, BF16), (512, BF16), (LANES, F32)]
    if not scan_only:
        widths += [(1024, BF16), (512, BF16)]
    return pl.pallas_call(
        functools.partial(_inproj_kernel, rowlen=rowlen, mod_row=mod_row, halo=halo, scan_only=scan_only),
        grid=(bsz // nb, t // tm),
        in_specs=x_specs + [lay(1, D_MODEL),
                            pl.BlockSpec((None, SUBLANES, D_MODEL), lambda b, i: (l, 0, 0)),
                            pl.BlockSpec((None, SUBLANES, D_MODEL), lambda b, i: (l, 0, 1)),
                            _layer_spec(l, pw["w_in_a"].shape[1:], single_buffer=True),
                            _layer_spec(l, pw["w_in_b"].shape[1:], single_buffer=True),
                            lay(3, BRANCH_W), lay(4, BRANCH_W), lay(1, BRANCH_W),
                            lay(4, SSD_XBC), lay(1, SSD_XBC)],
        out_specs=[tok(w) for w, _ in widths],
        out_shape=[out(w, dt) for w, dt in widths],
        compiler_params=_params("arbitrary", "arbitrary"),
        name="inproj",
    )(*xs, pw["norm_w"], mod, mod, pw["w_in_a"], pw["w_in_b"], pw["sc_cw"], pw["rg_cw"], pw["rg_cb"], pw["ssd_cw"],
      pw["ssd_cb"])


def _outproj_kernel(yrg_ref, ysc_ref, yfn_ref, yssd_ref, w_ref, x_ref, g_ref, fw_ref, o_ref, *, final, mod_row):
    row = pl.program_id(0) if mod_row is None else mod_row
    nb, tm, _ = x_ref.shape
    flat = lambda ref: ref[...].reshape(nb * tm, ref.shape[2])
    acc = _dot(flat(yrg_ref), w_ref[0:512, :])
    acc += _dot(flat(ysc_ref), w_ref[512:1024, :])
    acc += _dot(flat(yfn_ref), w_ref[1024:1536, :])
    acc += _dot(flat(yssd_ref), w_ref[1536:2048, :])
    xn = flat(x_ref) + g_ref[pl.ds(row, 1), :] * acc
    if final:
        ms = jnp.mean(xn * xn, axis=-1, keepdims=True)
        xn = (xn * lax.rsqrt(ms + NORM_EPS)) * fw_ref[...]
    o_ref[...] = xn.reshape(nb, tm, D_MODEL)


def _outproj(y_rg, y_sc, y_fn, y_ssd, x, mod, pw, l, final, tm, mod_row):
    bsz, t, _ = x.shape
    nb = bsz if (tm == t and mod_row is not None) else 1
    ytok = pl.BlockSpec((nb, tm, BRANCH_W), lambda b, i: (b, i, 0))
    xtok = pl.BlockSpec((nb, tm, D_MODEL), lambda b, i: (b, i, 0))
    return pl.pallas_call(
        functools.partial(_outproj_kernel, final=final, mod_row=mod_row),
        grid=(bsz // nb, t // tm),
        in_specs=[ytok, ytok, ytok, ytok,
                  _layer_spec(l, (D_INNER, D_MODEL), single_buffer=True),
                  xtok,
                  pl.BlockSpec((None, SUBLANES, D_MODEL), lambda b, i: (l, 0, 2)),
                  pl.BlockSpec((1, D_MODEL), lambda b, i: (0, 0))],
        out_specs=xtok,
        out_shape=jax.ShapeDtypeStruct((bsz, t, D_MODEL), F32),
        compiler_params=_params("arbitrary", "arbitrary"),
        name="outproj",
    )(y_rg, y_sc, y_fn, y_ssd, pw["w_out"], x, mod, pw["final_w"])


MIN_NORMAL_F32 = float(np.finfo(np.float32).tiny)
LOG2E = float(np.log2(np.e))


def _rg_coefficients(uh_b, wg, bg_half, lam):
    uh = uh_b.astype(F32)
    th = jnp.tanh(_dot(uh_b, wg) + bg_half)
    k = (-0.5 * RG_C * np.log2(np.e)) * _softplus(-lam)
    a = jnp.exp2(k + k * th[:, :RG_HALF])
    x = 1.0 - a * a
    root = x * lax.rsqrt(jnp.maximum(x, MIN_NORMAL_F32))
    return a, root * (uh + uh * th[:, RG_HALF:])


def _scan_tile(a, v, h_in, sub, reverse):
    rows = a.shape[0]
    for s in (1, 2, 4):
        if reverse:
            keep = sub < SUBLANES - s
            a_sh = jnp.where(keep, pltpu.roll(a, rows - s, 0), 1.0)
            v_sh = jnp.where(keep, pltpu.roll(v, rows - s, 0), 0.0)
        else:
            keep = sub >= s
            a_sh = jnp.where(keep, pltpu.roll(a, s, 0), 1.0)
            v_sh = jnp.where(keep, pltpu.roll(v, s, 0), 0.0)
        v = v + a * v_sh
        a = a * a_sh
    ngroups = rows // SUBLANES
    out = [None] * ngroups
    h = h_in
    order = range(ngroups - 1, -1, -1) if reverse else range(ngroups)
    for g in order:
        lo = g * SUBLANES
        hg = v[lo:lo + SUBLANES] + a[lo:lo + SUBLANES] * h
        out[g] = hg
        h = hg[0:1] if reverse else hg[SUBLANES - 1:SUBLANES]
    return jnp.concatenate(out, axis=0), h


def _rg_kernel(u_ref, g_ref, wg_ref, bg_ref, lam_ref, h0_ref, y_ref, ht_ref, hb_ref, *, t, tt):
    nt = t // tt
    sub = lax.broadcasted_iota(jnp.int32, (tt, RG_HALF), 0) & (SUBLANES - 1)

    def coeffs(s, d):
        r0 = pl.multiple_of(s * tt, tt)
        return _rg_coefficients(u_ref[0, pl.ds(r0, tt), :], wg_ref[d], bg_ref[d], lam_ref[d:d + 1, :])

    def rev_body(i, h):
        s = nt - 1 - i
        a, v = coeffs(s, 1)
        hh, hn = _scan_tile(a, v, h, sub, True)
        hb_ref[pl.ds(pl.multiple_of(s * tt, tt), tt), :] = hh
        return hn

    h_rev = lax.fori_loop(0, nt, rev_body, h0_ref[0, 1:2, :])

    def fwd_body(s, h):
        a, v = coeffs(s, 0)
        hh, hn = _scan_tile(a, v, h, sub, False)
        r0 = pl.multiple_of(s * tt, tt)
        g = g_ref[0, pl.ds(r0, tt), :].astype(F32)
        y_ref[0, pl.ds(r0, tt), :] = ((hh + hb_ref[pl.ds(r0, tt), :]) * _silu(g)).astype(BF16)
        return hn

    h_fwd = lax.fori_loop(0, nt, fwd_body, h0_ref[0, 0:1, :])
    ht_ref[0] = jnp.concatenate([h_fwd, h_rev, jnp.zeros((SUBLANES - 2, RG_HALF), F32)], axis=0)


RG_GATHER = BF16_ROWS


def _rg_interleaved_kernel(u_ref, g_ref, wg_ref, bg_ref, lam_ref, perm_ref, permt_ref, h0_ref, y_ref,
                           s_ref, af_ref, ab_ref, up_ref, *, t, tt):
    nt = t // tt
    ng = tt // SUBLANES
    chunk = t // SUBLANES
    steps = tt // SUBLANES
    nsub = steps // RG_GATHER
    block = SUBLANES * RG_GATHER
    shape = (SUBLANES, RG_HALF)

    def source_rows(k, m, s):
        return pl.ds(pl.multiple_of(s * chunk + k * steps + m * RG_GATHER, RG_GATHER), RG_GATHER)

    def gather(ref, k):
        blocks = []
        for m in range(nsub):
            rows = jnp.concatenate([ref[0, source_rows(k, m, s), :] for s in range(SUBLANES)], axis=0)
            blocks.append(_dot(perm_ref[...], rows).astype(BF16))
        return jnp.concatenate(blocks, axis=0)

    def sweep(d):
        def body(i, carry):
            h, acc = carry
            s = i if d == 0 else nt - 1 - i
            r0 = pl.multiple_of(s * tt, tt)
            if d == 1:
                ub = gather(u_ref, s)
                up_ref[pl.ds(r0, tt), :] = ub
            else:
                ub = up_ref[pl.ds(r0, tt), :]
            a, v = _rg_coefficients(ub, wg_ref[d], bg_ref[d], lam_ref[d:d + 1, :])
            hs, ps = [None] * ng, [None] * ng
            for g in (range(ng) if d == 0 else range(ng - 1, -1, -1)):
                ag = a[g * SUBLANES:(g + 1) * SUBLANES]
                h = ag * h + v[g * SUBLANES:(g + 1) * SUBLANES]
                acc = ag * acc
                hs[g], ps[g] = h, acc
            local = jnp.concatenate(hs, axis=0)
            if d == 1:
                s_ref[pl.ds(r0, tt), :] = local
                ab_ref[pl.ds(r0, tt), :] = jnp.concatenate(ps, axis=0)
            else:
                s_ref[pl.ds(r0, tt), :] += local
                af_ref[pl.ds(r0, tt), :] = jnp.concatenate(ps, axis=0)
            return h, acc

        return lax.fori_loop(0, nt, body, (jnp.zeros(shape, F32), jnp.ones(shape, F32)), unroll=2)

    hb, pb = sweep(1)
    hf, pf = sweep(0)
    cf = [h0_ref[0, 0:1, :]]
    for s in range(SUBLANES - 1):
        cf.append(hf[s:s + 1] + pf[s:s + 1] * cf[s])
    cb = [None] * SUBLANES
    cb[SUBLANES - 1] = h0_ref[0, 1:2, :]
    for s in range(SUBLANES - 1, 0, -1):
        cb[s - 1] = hb[s:s + 1] + pb[s:s + 1] * cb[s]
    cf = jnp.concatenate(cf, axis=0)[None]
    cb = jnp.concatenate(cb, axis=0)[None]

    def fix(s, carry):
        r0 = pl.multiple_of(s * tt, tt)
        corr = (af_ref[pl.ds(r0, tt), :].reshape(ng, SUBLANES, RG_HALF) * cf
                + ab_ref[pl.ds(r0, tt), :].reshape(ng, SUBLANES, RG_HALF) * cb).reshape(tt, RG_HALF)
        g = gather(g_ref, s).astype(F32)
        y = ((s_ref[pl.ds(r0, tt), :] + corr) * _silu(g)).astype(BF16)
        for m in range(nsub):
            back = _dot(permt_ref[...], y[m * block:(m + 1) * block]).astype(BF16)
            for c in range(SUBLANES):
                y_ref[0, source_rows(s, m, c), :] = back[c * RG_GATHER:(c + 1) * RG_GATHER]
        return carry

    lax.fori_loop(0, nt, fix, 0, unroll=4 if nt % 4 == 0 else 1)


def _rg(rg, pw, l, h0, tt, interleaved):
    bsz, t, _ = rg.shape
    assert t % tt == 0
    nhalf = BRANCH_W // RG_HALF
    seq = lambda col0: pl.BlockSpec((1, t, RG_HALF), lambda b, h: (b, 0, col0 + h))
    in_specs = [seq(0), seq(nhalf),
                pl.BlockSpec((None, 2, None, RG_HALF, 2 * RG_HALF), lambda b, h: (l, 0, h, 0, 0)),
                pl.BlockSpec((None, 2, None, 1, 2 * RG_HALF), lambda b, h: (l, 0, h, 0, 0)),
                pl.BlockSpec((None, 2, RG_HALF), lambda b, h: (l, 0, h)),
                pl.BlockSpec((1, SUBLANES, RG_HALF), lambda b, h: (b, 0, h))]
    args = (rg, rg, pw["rg_wg"], pw["rg_bg"], pw["rg_lam"], h0)
    y_spec = pl.BlockSpec((1, t, RG_HALF), lambda b, h: (b, 0, h))
    y_shape = jax.ShapeDtypeStruct((bsz, t, BRANCH_W), BF16)
    seq_scratch = pltpu.VMEM((t, RG_HALF), F32)
    if interleaved:
        block = SUBLANES * RG_GATHER
        assert tt % block == 0
        pos = np.arange(block)
        perm = np.zeros((block, block))
        perm[pos, RG_GATHER * (pos % SUBLANES) + pos // SUBLANES] = 1.0
        perm = jnp.asarray(perm, dtype=F32).astype(BF16)
        pspec = pl.BlockSpec((block, block), lambda b, h: (0, 0))
        y = pl.pallas_call(
            functools.partial(_rg_interleaved_kernel, t=t, tt=tt),
            grid=(bsz, nhalf), in_specs=in_specs[:5] + [pspec, pspec] + in_specs[5:],
            out_specs=y_spec, out_shape=y_shape,
            scratch_shapes=[seq_scratch, seq_scratch, seq_scratch, pltpu.VMEM((t, RG_HALF), BF16)],
            compiler_params=_params("arbitrary", "arbitrary"), name="rglru_interleaved",
        )(*args[:5], perm, perm.T, args[5])
        return y, None
    return pl.pallas_call(
        functools.partial(_rg_kernel, t=t, tt=tt),
        grid=(bsz, nhalf), in_specs=in_specs,
        out_specs=[y_spec, pl.BlockSpec((1, SUBLANES, RG_HALF), lambda b, h: (b, 0, h))],
        out_shape=[y_shape, jax.ShapeDtypeStruct((bsz, SUBLANES, BRANCH_W), F32)],
        scratch_shapes=[seq_scratch],
        compiler_params=_params("arbitrary", "arbitrary"), name="rglru",
    )(*args)


def _ssd_kernel(xbc_ref, z_ref, dt_ref, dtb_ref, alog_ref, dsk_ref, nw_ref, ecol_ref, ehead_ref,
                hmask_ref, h0_ref, y_ref, ht_ref, yb_ref, cs_ref, adjt_ref, diagt_ref, wgtt_ref, etot_ref,
                loc_ref, st_ref, *, t):
    q = SSD_CHUNK
    nc = t // q
    npairs = SSD_HEADS // 2
    pairs_per_group = npairs // SSD_GROUPS
    lane1 = lax.broadcasted_iota(jnp.int32, (1, LANES), 1)
    a_row = -LOG2E * jnp.exp(alog_ref[...])
    fwd_lane = (lax.broadcasted_iota(jnp.int32, (q, LANES), 1) & (SSD_DT - 1)) < SSD_HEADS
    ri = lax.broadcasted_iota(jnp.int32, (q, q), 0)
    ci = lax.broadcasted_iota(jnp.int32, (q, q), 1)
    lower_b = jnp.where(ri >= ci, 1.0, 0.0).astype(BF16)
    head0_s = lax.broadcasted_iota(jnp.int32, (SSD_STATE, LANES), 1) < SSD_HD

    def pair_vec(row, l0, l1):
        return jnp.where(lane1 < SSD_HD, row[:, l0:l0 + 1], row[:, l1:l1 + 1])

    def small_rows(c):
        return pl.ds(pl.multiple_of(c * SSD_DT, SSD_DT), SSD_DT)

    def pass0(c, carry):
        r0 = pl.multiple_of(c * q, q)
        dtv = _softplus(dt_ref[0, pl.ds(r0, q), :] + dtb_ref[...])
        da = dtv * a_row
        pre = sum(_dot(lower_b, part.astype(BF16)) for part in _split3(da))
        tot = pre[q - 1:q, :]
        cs = jnp.where(fwd_lane, pre, tot - pre + da)
        cs_ref[pl.ds(r0, q), :] = cs
        adjt_ref[small_rows(c), :] = (cs - LOG2E * jnp.log(dtv)).T[:SSD_DT]
        diagt_ref[small_rows(c), :] = (LOG2E * jnp.log(dtv + pltpu.roll(dtv, SSD_HEADS, 1))).T[:SSD_DT]
        wgtt_ref[small_rows(c), :] = (dtv * jnp.exp2(tot - cs)).T[:SSD_DT]
        etot_ref[small_rows(c), :] = jnp.broadcast_to(jnp.exp2(tot), (SSD_DT, LANES))
        return carry

    def pass1(i, carry):
        c = nc - 1 - i
        r0 = pl.multiple_of(c * q, q)
        xs_b = xbc_ref[0, pl.ds(r0, q), 0:BRANCH_W]
        bm_b = xbc_ref[0, pl.ds(r0, q), BRANCH_W:BRANCH_W + LANES]
        cm_b = xbc_ref[0, pl.ds(r0, q), BRANCH_W + LANES:SSD_XBC]
        cs_p = _pack3(cs_ref[pl.ds(r0, q), :])
        cs_col = _dot(cs_p, ecol_ref[...])
        ecs_b = jnp.exp2(_dot(cs_p, ehead_ref[:, BRANCH_W:]))
        adjt = adjt_ref[small_rows(c), :]
        diagt = diagt_ref[small_rows(c), :]
        wgtt = wgtt_ref[small_rows(c), :]
        etot = etot_ref[pl.ds(pl.multiple_of(c * SSD_DT, SSD_DT), 1), :]
        bt = bm_b.astype(F32).T
        xs_h0 = xs_b * hmask_ref[0:1, :]
        xs_h1 = xs_b * hmask_ref[1:2, :]
        ys = []
        for g in range(SSD_GROUPS):
            gs = slice(g * SSD_STATE, (g + 1) * SSD_STATE)
            cbm = lax.dot_general(cm_b[:, gs], bm_b[:, gs], (((1,), (1,)), ((), ())), preferred_element_type=F32)
            btg = bt[gs, :]
            for pp in range(pairs_per_group):
                pair = g * pairs_per_group + pp
                ps = slice(pair * LANES, (pair + 1) * LANES)
                state = st_ref[pair]
                lf0 = 2 * pair
                lb0 = SSD_HEADS + lf0
                ms = []
                for lf in (lf0, lf0 + 1):
                    lb = SSD_HEADS + lf
                    arg = jnp.where(ri > ci, cs_col[:, lf * LANES:(lf + 1) * LANES] - adjt[lf:lf + 1, :],
                                    jnp.where(ri < ci, cs_col[:, lb * LANES:(lb + 1) * LANES] - adjt[lb:lb + 1, :],
                                              diagt[lf:lf + 1, :]))
                    ms.append((cbm * jnp.exp2(arg)).astype(BF16))
                yd = _dot(jnp.concatenate(ms, axis=1), jnp.concatenate([xs_h0[:, ps], xs_h1[:, ps]], axis=0))
                yo = _dot(cm_b[:, gs], state.astype(BF16)) * ecs_b[:, ps]
                ys.append(yd + yo)
                stack = jnp.concatenate([(btg * wgtt[l:l + 1, :]).astype(BF16) for l in (lf0, lf0 + 1, lb0, lb0 + 1)],
                                        axis=0)
                res = _dot(stack, xs_b[:, ps])
                n = SSD_STATE
                loc_ref[c, pair] = jnp.where(head0_s, res[0:n], res[n:2 * n])
                st_ref[pair] = state * pair_vec(etot, lb0, lb0 + 1) + jnp.where(head0_s, res[2 * n:3 * n], res[3 * n:])
        yb_ref[pl.ds(r0, q), :] = jnp.concatenate(ys, axis=1) + dsk_ref[...] * xs_b.astype(F32)
        return carry

    def pass2(c, carry):
        r0 = pl.multiple_of(c * q, q)
        cs = cs_ref[pl.ds(r0, q), :]
        cm_b = xbc_ref[0, pl.ds(r0, q), BRANCH_W + LANES:SSD_XBC]
        ecs_f = jnp.exp2(_dot(_pack3(cs), ehead_ref[:, :BRANCH_W]))
        etot = jnp.exp2(cs[q - 1:q, :])
        ys = []
        for pair in range(npairs):
            g = pair // pairs_per_group
            state = st_ref[pair]
            lf0 = 2 * pair
            ys.append(_dot(cm_b[:, g * SSD_STATE:(g + 1) * SSD_STATE], state.astype(BF16))
                      * ecs_f[:, pair * LANES:(pair + 1) * LANES])
            st_ref[pair] = state * pair_vec(etot, lf0, lf0 + 1) + loc_ref[c, pair]
        yt = yb_ref[pl.ds(r0, q), :] + jnp.concatenate(ys, axis=1)
        yt = yt * _silu(z_ref[0, pl.ds(r0, q), :].astype(F32))
        ms = jnp.mean(yt * yt, axis=-1, keepdims=True)
        y_ref[0, pl.ds(r0, q), :] = ((yt * lax.rsqrt(ms + NORM_EPS)) * nw_ref[...]).astype(BF16)
        return carry

    lax.fori_loop(0, nc, pass0, 0, unroll=8 if nc % 8 == 0 else 1)
    for p in range(npairs):
        st_ref[p] = h0_ref[0, 1, p]
    lax.fori_loop(0, nc, pass1, 0, unroll=4 if nc % 4 == 0 else 2)
    for p in range(npairs):
        ht_ref[0, 1, p] = st_ref[p]
        st_ref[p] = h0_ref[0, 0, p]
    lax.fori_loop(0, nc, pass2, 0, unroll=8 if nc % 8 == 0 else 2)
    for p in range(npairs):
        ht_ref[0, 0, p] = st_ref[p]


@functools.lru_cache(maxsize=None)
def _ssd_tables():
    lanes = np.arange(LANES)[:, None]
    src = np.where(lanes < 3 * SSD_DT, lanes % SSD_DT, -1)
    ecol = (src == np.arange(SSD_DT * LANES)[None, :] // LANES)
    ehead = (src == np.arange(2 * BRANCH_W)[None, :] // SSD_HD)
    even_head = (np.arange(BRANCH_W) // SSD_HD) % 2 == 0
    hmask = np.zeros((BF16_ROWS, BRANCH_W))
    hmask[0] = even_head
    hmask[1] = ~even_head
    as_bf16 = lambda a: jnp.asarray(a, dtype=F32).astype(BF16)
    return as_bf16(ecol), as_bf16(ehead), as_bf16(hmask)


def _ssd(xbc, z, dt, pw, l, h0):
    bsz, t, _ = xbc.shape
    assert t % SSD_CHUNK == 0
    npairs = SSD_HEADS // 2
    ecol, ehead, hmask = _ssd_tables()
    full = lambda shape: pl.BlockSpec(shape, lambda b: (0,) * len(shape))
    lay = lambda *shape: _layer_spec(l, shape)
    seq = lambda width: pl.BlockSpec((1, t, width), lambda b: (b, 0, 0))
    st_spec = pl.BlockSpec((1, 2, npairs, SSD_STATE, LANES), lambda b: (b, 0, 0, 0, 0))
    return pl.pallas_call(
        functools.partial(_ssd_kernel, t=t),
        grid=(bsz,),
        in_specs=[seq(SSD_XBC), seq(BRANCH_W), seq(LANES), lay(1, LANES), lay(1, LANES),
                  lay(1, BRANCH_W), lay(1, BRANCH_W), full(ecol.shape), full(ehead.shape), full(hmask.shape),
                  st_spec],
        out_specs=[seq(BRANCH_W), st_spec],
        out_shape=[jax.ShapeDtypeStruct((bsz, t, BRANCH_W), BF16),
                   jax.ShapeDtypeStruct((bsz, 2, npairs, SSD_STATE, LANES), F32)],
        scratch_shapes=[pltpu.VMEM((t, BRANCH_W), F32), pltpu.VMEM((t, LANES), F32)]
        + [pltpu.VMEM((t // SSD_CHUNK * SSD_DT, LANES), F32)] * 4
        + [pltpu.VMEM((t // SSD_CHUNK, npairs, SSD_STATE, LANES), F32),
                        pltpu.VMEM((npairs, SSD_STATE, LANES), F32)],
        compiler_params=_params("arbitrary"),
        name="ssd",
    )(xbc, z, dt, pw["ssd_dtb"], pw["ssd_alog"], pw["ssd_dskip"], pw["ssd_nw"], ecol, ehead, hmask, h0)


@functools.lru_cache(maxsize=None)
def _fourier_tables(n, blk):
    half = n // 2
    nh = half // blk
    assert nh * blk == half and nh <= SUBLANES
    p = np.arange(FN_GD)
    ang_c = 2.0 * np.pi * ((p[:, None] * p[None, :]) % FN_GD) / FN_GD
    eye = np.eye(BRANCH_W // FN_GD)
    cc = np.kron(eye, np.cos(ang_c))
    sc = np.kron(eye, np.sin(ang_c))
    k = np.arange(half)
    ang_t = 2.0 * np.pi * ((k[:, None] * k[None, :]) % n) / n
    ct = np.cos(ang_t)
    stn = -np.sin(ang_t)
    assert blk % FN_FLIP == 0
    jp = np.zeros((FN_FLIP, FN_FLIP))
    r = np.arange(1, FN_FLIP)
    jp[r, FN_FLIP - r] = 1.0
    tt = np.arange(n)
    ks = blk * (np.arange(nh) + 1)
    ang_k = 2.0 * np.pi * ((ks[:, None] * tt[None, :]) % n) / n
    tc = np.zeros((2 * SUBLANES, n))
    ts = np.zeros((2 * SUBLANES, n))
    tc[:nh] = np.cos(ang_k)
    ts[:nh] = np.sin(ang_k)
    tc[SUBLANES, half] = 1.0
    as_bf16 = lambda a: jnp.asarray(a, dtype=F32).astype(BF16)
    return dict(cc=as_bf16(cc), sc=as_bf16(sc), ct=as_bf16(ct), stn=as_bf16(stn), jp=as_bf16(jp),
                tc=as_bf16(tc), ts=as_bf16(ts))


def _mirror(jp_ref, src, row0):
    n = src.shape[0]
    nsub = n // FN_FLIP
    first = lax.broadcasted_iota(jnp.int32, (FN_FLIP, src.shape[1]), 0) == 0
    out = []
    for a in range(nsub):
        s = nsub - 1 - a
        head = row0 if a == 0 else src[(s + 1) * FN_FLIP:(s + 1) * FN_FLIP + 1].astype(F32)
        out.append(jnp.where(first, head, _dot(jp_ref[...], src[s * FN_FLIP:(s + 1) * FN_FLIP])))
    return jnp.concatenate(out, axis=0)


def _fn_fold_kernel(xj_ref, xm_ref, xr_ref, jp_ref, cc_ref, sc_ref, eo_ref):
    j = pl.program_id(1)
    xj = xj_ref[0]
    row0 = xr_ref[0, 0:1, :].astype(F32) * jnp.where(j == 0, 0.0, 1.0)
    mir = _mirror(jp_ref, xm_ref[0], row0).astype(BF16)
    e = _dot(xj, cc_ref[...]) + _dot(mir, cc_ref[...])
    o = _dot(xj, sc_ref[...]) - _dot(mir, sc_ref[...])
    eo_ref[0] = jnp.concatenate([e, o], axis=1).astype(BF16)


def _fn_aux_kernel(x_ref, tc_ref, ts_ref, cc_ref, sc_ref, o_ref):
    x = x_ref[0]
    xc = _dot(tc_ref[...], x).astype(BF16)
    xs = _dot(ts_ref[...], x).astype(BF16)
    o_ref[0] = _dot(xc, cc_ref[...]) + _dot(xs, sc_ref[...])


def _fn_main_kernel(ct_ref, stn_ref, eo_ref, aux_ref, g_ref, jp_ref, y_ref, *, scale):
    m = pl.program_id(1)
    blk = ct_ref.shape[0]
    nblk = y_ref.shape[1] // blk
    rows = lax.broadcasted_iota(jnp.int32, (blk, BRANCH_W), 0)
    p = _dot(ct_ref[...], eo_ref[0, :, 0:BRANCH_W])
    qn = _dot(stn_ref[...], eo_ref[0, :, BRANCH_W:2 * BRANCH_W])
    sgn = (1 - 2 * ((m * blk + rows) & 1)).astype(F32)
    p = p + sgn * aux_ref[0, SUBLANES:SUBLANES + 1, :]
    lo = pl.multiple_of(m * blk, blk)
    hi = pl.multiple_of((nblk - 1 - m) * blk, blk)
    y_ref[0, pl.ds(lo, blk), :] = ((p + qn) * (_silu(g_ref[0, pl.ds(lo, blk), :].astype(F32)) * scale)).astype(BF16)
    sel = lax.broadcasted_iota(jnp.int32, (2 * SUBLANES, BRANCH_W), 0) == m
    row0 = jnp.sum(jnp.where(sel, aux_ref[0], 0.0), axis=0, keepdims=True)
    y_ref[0, pl.ds(hi, blk), :] = (_mirror(jp_ref, (p - qn).astype(BF16), row0)
                                   * (_silu(g_ref[0, pl.ds(hi, blk), :].astype(F32)) * scale)).astype(BF16)


def _fourier(fn, blk):
    bsz, n, _ = fn.shape
    tb = _fourier_tables(n, blk)
    half = n // 2
    nh = half // blk
    nblk = 2 * nh
    rows16 = n // BF16_ROWS
    full2 = lambda shape: pl.BlockSpec(shape, lambda b, j: (0,) * len(shape))
    eo = pl.pallas_call(
        _fn_fold_kernel,
        grid=(bsz, nh),
        in_specs=[pl.BlockSpec((1, blk, BRANCH_W), lambda b, j: (b, j, 0)),
                  pl.BlockSpec((1, blk, BRANCH_W), lambda b, j: (b, nblk - 1 - j, 0)),
                  pl.BlockSpec((1, BF16_ROWS, BRANCH_W),
                               lambda b, j: (b, jnp.minimum((blk // BF16_ROWS) * (nblk - j), rows16 - 1), 0)),
                  full2((FN_FLIP, FN_FLIP)), full2((BRANCH_W, BRANCH_W)), full2((BRANCH_W, BRANCH_W))],
        out_specs=pl.BlockSpec((1, blk, 2 * BRANCH_W), lambda b, j: (b, j, 0)),
        out_shape=jax.ShapeDtypeStruct((bsz, half, 2 * BRANCH_W), BF16),
        compiler_params=_params("arbitrary", "arbitrary"),
        name="fn_fold",
    )(fn, fn, fn, tb["jp"], tb["cc"], tb["sc"])
    full1 = lambda shape: pl.BlockSpec(shape, lambda b: (0,) * len(shape))
    aux = pl.pallas_call(
        _fn_aux_kernel,
        grid=(bsz,),
        in_specs=[pl.BlockSpec((1, n, BRANCH_W), lambda b: (b, 0, 0)),
                  full1((2 * SUBLANES, n)), full1((2 * SUBLANES, n)),
                  full1((BRANCH_W, BRANCH_W)), full1((BRANCH_W, BRANCH_W))],
        out_specs=pl.BlockSpec((1, 2 * SUBLANES, BRANCH_W), lambda b: (b, 0, 0)),
        out_shape=jax.ShapeDtypeStruct((bsz, 2 * SUBLANES, BRANCH_W), F32),
        compiler_params=_params("arbitrary"),
        name="fn_aux",
    )(fn, tb["tc"], tb["ts"], tb["cc"], tb["sc"])
    return pl.pallas_call(
        functools.partial(_fn_main_kernel, scale=float(1.0 / np.sqrt(n * FN_GD))),
        grid=(bsz, nh),
        in_specs=[pl.BlockSpec((blk, half), lambda b, m: (m, 0)),
                  pl.BlockSpec((blk, half), lambda b, m: (m, 0)),
                  pl.BlockSpec((1, half, 2 * BRANCH_W), lambda b, m: (b, 0, 0)),
                  pl.BlockSpec((1, 2 * SUBLANES, BRANCH_W), lambda b, m: (b, 0, 0)),
                  pl.BlockSpec((1, n, BRANCH_W), lambda b, m: (b, 0, 1)),
                  full2((FN_FLIP, FN_FLIP))],
        out_specs=pl.BlockSpec((1, n, BRANCH_W), lambda b, m: (b, 0, 0)),
        out_shape=jax.ShapeDtypeStruct((bsz, n, BRANCH_W), BF16),
        compiler_params=_params("arbitrary", "arbitrary"),
        name="fn_main",
    )(tb["ct"], tb["stn"], eo, aux, fn, tb["jp"])


def _prepare_weights(norm_w, w_in, w_out, rg_conv_w, rg_conv_b, rg_gate_a_w, rg_gate_a_b, rg_gate_x_w, rg_gate_x_b,
                     rg_lambda, sc_conv_w, ssd_conv_w, ssd_conv_b, ssd_dt_bias, ssd_a_log, ssd_d, ssd_norm_w,
                     final_norm_w):
    depth = w_in.shape[0]
    assert REF_DT_HI - REF_DT_LO == SSD_DT and w_in.shape[2] - SSD_DT + LANES == C_END
    w_a = jnp.concatenate([w_in[:, :, :REF_DT_LO], jnp.tile(w_in[:, :, REF_DT_LO:REF_DT_HI], (1, 1, LANES // SSD_DT))],
                          axis=2).astype(BF16)
    w_b = w_in[:, :, REF_DT_HI:].astype(BF16)
    heads_per_half = RG_HALF // RG_HD
    eye = jnp.eye(heads_per_half, dtype=F32)

    def blockdiag(wg):
        wg = wg.reshape(depth, 2, -1, heads_per_half, RG_HD, RG_HD)
        return jnp.einsum("ldhjio,jk->ldhjiko", wg, eye).reshape(depth, 2, -1, RG_HALF, RG_HALF)

    halves = lambda b: b.reshape(depth, 2, -1, 1, RG_HALF)
    rep = LANES // SSD_DT
    return dict(
        norm_w=norm_w.reshape(depth, 1, D_MODEL), final_w=final_norm_w.reshape(1, D_MODEL),
        w_in_a=w_a, w_in_b=w_b, w_out=w_out.astype(BF16), sc_cw=sc_conv_w,
        rg_cw=0.5 * rg_conv_w, rg_cb=0.5 * rg_conv_b.reshape(depth, 1, BRANCH_W),
        rg_wg=jnp.concatenate([blockdiag(rg_gate_a_w), blockdiag(rg_gate_x_w)], axis=-1).astype(BF16),
        rg_bg=0.5 * jnp.concatenate([halves(rg_gate_a_b), halves(rg_gate_x_b)], axis=-1),
        rg_lam=rg_lambda,
        ssd_cw=ssd_conv_w, ssd_cb=ssd_conv_b.reshape(depth, 1, SSD_XBC),
        ssd_dtb=jnp.tile(ssd_dt_bias.reshape(depth, 1, SSD_DT), (1, 1, rep)),
        ssd_alog=jnp.tile(ssd_a_log.reshape(depth, 1, SSD_DT), (1, 1, rep)),
        ssd_dskip=jnp.repeat(ssd_d, SSD_HD, axis=1).reshape(depth, 1, BRANCH_W),
        ssd_nw=ssd_norm_w.reshape(depth, 1, BRANCH_W),
    )


def _mix(x, mod, pw, l, mod_row, rowlen, tm, tt, blk, interleave, rg_h0, ssd_h0, with_output, final):
    rg, xbc, z, dt, *rest = _inproj(x, mod, pw, l, rowlen, tm, mod_row, scan_only=not with_output)
    y_rg, rg_st = _rg(rg, pw, l, rg_h0, tt, interleave)
    y_ssd, ssd_st = _ssd(xbc, z, dt, pw, l, ssd_h0)
    if not with_output:
        return None, rg_st, ssd_st
    fn, y_sc = rest
    y_fn = _fourier(fn, blk)
    return _outproj(y_rg, y_sc, y_fn, y_ssd, x, mod, pw, l, final, tm, mod_row), rg_st, ssd_st


def kernel(x, c, ctx, c_ctx, ada_w, ada_b, norm_w, w_in, w_out, rg_conv_w, rg_conv_b, rg_gate_a_w, rg_gate_a_b,
           rg_gate_x_w, rg_gate_x_b, rg_lambda, sc_conv_w, ssd_conv_w, ssd_conv_b, ssd_dt_bias, ssd_a_log, ssd_d,
           ssd_norm_w, final_norm_w):
    bsz, seq, _ = x.shape
    ctx_len = ctx.shape[1]
    depth = w_in.shape[0]
    assert bsz + 1 <= SUBLANES and seq % SUBLANES == 0
    pw = _prepare_weights(norm_w, w_in, w_out, rg_conv_w, rg_conv_b, rg_gate_a_w, rg_gate_a_b, rg_gate_x_w,
                          rg_gate_x_b, rg_lambda, sc_conv_w, ssd_conv_w, ssd_conv_b, ssd_dt_bias, ssd_a_log, ssd_d,
                          ssd_norm_w, final_norm_w)
    c8 = jnp.concatenate([c, c_ctx[None, :], jnp.zeros((SUBLANES - bsz - 1, D_MODEL), F32)], axis=0)
    mod = _ada(c8, ada_w, ada_b.reshape(depth, 1, 3 * D_MODEL))
    zeros_rg = jnp.zeros((bsz, SUBLANES, BRANCH_W), F32)
    zeros_ssd = jnp.zeros((bsz, 2, SSD_HEADS // 2, SSD_STATE, LANES), F32)
    tm_lat = min(1024, seq)
    tt_lat = min(1024, seq)
    tt_ctx = min(256, ctx_len)
    blk_lat = min(512, seq // 2)
    blk_ctx = min(512, ctx_len // 2)
    for l in range(depth):
        last = l == depth - 1
        new_ctx, rg_st, ssd_st = _mix(ctx, mod, pw, l, bsz, ctx_len, ctx_len, tt_ctx, blk_ctx, False,
                                      zeros_rg, zeros_ssd, not last, False)
        x, _, _ = _mix(x, mod, pw, l, None, GRID_W, tm_lat, tt_lat, blk_lat, True, rg_st, ssd_st, True, last)
        if not last:
            ctx = new_ctx
    return x
```

```python
import functools

import numpy as np
import jax
import jax.numpy as jnp
from jax import lax
from jax.experimental import pallas as pl
from jax.experimental.pallas import tpu as pltpu

F32 = jnp.float32
BF16 = jnp.bfloat16
HIGHEST = lax.Precision.HIGHEST

D_MODEL = 1024
D_INNER = 2048
BRANCH_W = 512
GRID_W = 64
RG_HD = 64
RG_C = 8.0
RG_HALF = 256
SSD_HEADS = 8
SSD_HD = 64
SSD_GROUPS = 2
SSD_STATE = 64
SSD_XBC = BRANCH_W + 2 * SSD_GROUPS * SSD_STATE
SSD_CHUNK = 128
SSD_DT = 2 * SSD_HEADS
FN_GD = 128
FN_FLIP = 128
NORM_EPS = 1e-6

LANES = 128
SUBLANES = 8
BF16_ROWS = 16
VMEM_LIMIT_BYTES = 60000 * 1024

C_RGX, C_XBC, C_RGG, C_Z, C_SC, C_FN, C_DT, C_END = 0, 512, 1280, 1792, 2304, 4352, 5376, 5504
REF_DT_LO, REF_DT_HI = 1280, 1296


def _sigmoid(v):
    return 0.5 + 0.5 * jnp.tanh(0.5 * v)


def _silu(v):
    h = 0.5 * v
    return h + h * jnp.tanh(h)


def _softplus(v):
    return jnp.maximum(v, 0.0) + jnp.log1p(jnp.exp(-jnp.abs(v)))


def _dot(a, b):
    return jnp.dot(a, b, preferred_element_type=F32)


def _split3(v):
    hi = v.astype(BF16).astype(F32)
    rest = v - hi
    mid = rest.astype(BF16).astype(F32)
    return hi, mid, rest - mid


def _pack3(v):
    hi, mid, lo = _split3(v)
    period = lax.broadcasted_iota(jnp.int32, v.shape, 1) // SSD_DT
    return jnp.where(period == 0, hi, jnp.where(period == 1, mid, jnp.where(period == 2, lo, 0.0))).astype(BF16)


def _params(*semantics):
    return pltpu.CompilerParams(dimension_semantics=semantics, vmem_limit_bytes=VMEM_LIMIT_BYTES)


def _layer_spec(l, shape, single_buffer=False):
    mode = dict(pipeline_mode=pl.Buffered(1)) if single_buffer else {}
    return pl.BlockSpec((None,) + tuple(shape), lambda *_: (l,) + (0,) * len(shape), **mode)


def _ada_kernel(c_ref, w_ref, b_ref, o_ref):
    o_ref[...] = jnp.dot(_silu(c_ref[...]), w_ref[...], precision=HIGHEST,
                         preferred_element_type=F32) + b_ref[...]


def _ada(c8, w, b):
    depth = w.shape[0]
    tn = 512
    return pl.pallas_call(
        _ada_kernel,
        grid=(depth, 3 * D_MODEL // tn),
        in_specs=[pl.BlockSpec((SUBLANES, D_MODEL), lambda l, j: (0, 0)),
                  pl.BlockSpec((None, D_MODEL, tn), lambda l, j: (l, 0, j)),
                  pl.BlockSpec((None, 1, tn), lambda l, j: (l, 0, j))],
        out_specs=pl.BlockSpec((None, SUBLANES, tn), lambda l, j: (l, 0, j)),
        out_shape=jax.ShapeDtypeStruct((depth, SUBLANES, 3 * D_MODEL), F32),
        compiler_params=_params("arbitrary", "arbitrary"),
        name="ada",
    )(c8, w, b)


def _inproj_kernel(*refs, rowlen, mod_row, halo, scan_only):
    if halo:
        x_ref, xp_ref, xn_ref, *refs = refs
    else:
        x_ref, *refs = refs
    nw_ref, sh_ref, sc_ref, wa_ref, wb_ref, cw_ref, rcw_ref, rcb_ref, scw_ref, scb_ref, *outs, tap_ref = refs
    nb, tm, _ = x_ref.shape
    rows = nb * tm
    i = pl.program_id(1)
    row = pl.program_id(0) if mod_row is None else mod_row
    if halo:
        x = jnp.concatenate([xp_ref[0], x_ref[0], xn_ref[0]], axis=0)
    else:
        x = x_ref[...].reshape(rows, D_MODEL)
    ms = jnp.mean(x * x, axis=-1, keepdims=True)
    gain = nw_ref[...] * (1.0 + sc_ref[pl.ds(row, 1), :])
    hx = ((x * lax.rsqrt(ms + NORM_EPS)) * gain + sh_ref[pl.ds(row, 1), :]).astype(BF16)
    hb = hx[halo:halo + rows]

    def wcols(lo, hi):
        if hi <= REF_DT_LO:
            return wa_ref[:, lo:hi]
        if lo >= C_DT:
            return wa_ref[:, lo - C_DT + REF_DT_LO:hi - C_DT + REF_DT_LO]
        assert lo >= REF_DT_LO and hi <= C_DT
        return wb_ref[:, lo - REF_DT_LO:hi - REF_DT_LO]

    def proj(lo, hi):
        return _dot(hb, wcols(lo, hi))

    def proj_halo(lo, hi):
        return _dot(hx, wcols(lo, hi))

    def conv4(p, cw, cb):
        n = p.shape[0]
        if halo:
            keep_prev = jnp.where(i == 0, 0.0, 1.0)
            keep_next = jnp.where(i == pl.num_programs(1) - 1, 0.0, 1.0)
            w = p.shape[1]
            tap_ref[0:halo, 0:w] = p[:halo] * keep_prev
            tap_ref[halo:halo + rows, 0:w] = p[halo:halo + rows]
            tap_ref[halo + rows:n, 0:w] = p[halo + rows:] * keep_next
            taps = [tap_ref[halo + off:halo + off + rows, 0:w] for off in (-2, -1)]
            taps += [p[halo:halo + rows], tap_ref[halo + 1:halo + 1 + rows, 0:w]]
        else:
            pos = lax.broadcasted_iota(jnp.int32, p.shape, 0) & (tm - 1)
            taps = [jnp.where(pos >= 2, pltpu.roll(p, 2, 0), 0.0), jnp.where(pos >= 1, pltpu.roll(p, 1, 0), 0.0), p,
                    jnp.where(pos <= tm - 2, pltpu.roll(p, n - 1, 0), 0.0)]
        return sum(cw[k:k + 1, :] * taps[k] for k in range(4)) + cb

    def put(ref, value, lo=0):
        ref[:, :, lo:lo + value.shape[1]] = value.reshape(nb, tm, value.shape[1]).astype(ref.dtype)

    stages = []
    if scan_only:
        rg_ref, xbc_ref, z_ref, dt_ref = outs
    else:
        rg_ref, xbc_ref, z_ref, dt_ref, fn_ref, ysc_ref = outs
        conv3 = []

        def sc_conv(ps):
            v = ps[0] * ps[1]
            pos = lax.broadcasted_iota(jnp.int32, (rows, BRANCH_W), 0) & (rowlen - 1)
            pad = SUBLANES
            edge = jnp.zeros((pad, BRANCH_W), F32)
            tap_ref[0:pad, 0:BRANCH_W] = edge
            tap_ref[pad:pad + rows, 0:BRANCH_W] = v
            tap_ref[pad + rows:2 * pad + rows, 0:BRANCH_W] = edge
            vm1 = jnp.where(pos == 0, 0.0, tap_ref[pad - 1:pad - 1 + rows, 0:BRANCH_W])
            vp1 = jnp.where(pos == rowlen - 1, 0.0, tap_ref[pad + 1:pad + 1 + rows, 0:BRANCH_W])
            conv3.append(cw_ref[0:1, :] * vm1 + cw_ref[1:2, :] * v + cw_ref[2:3, :] * vp1)

        stages.append((lambda: (proj(C_SC + 512, C_SC + 1024), proj(C_SC + 1024, C_SC + 1536)), sc_conv))
        stages.append((lambda: (proj(C_SC, C_SC + 512), proj(C_SC + 1536, C_FN)),
                       lambda ps: put(ysc_ref, ps[0] * conv3[0] * _silu(ps[1]))))
    stages.append((lambda: proj_halo(C_XBC, C_RGG),
                   lambda p: put(xbc_ref, _silu(conv4(p, scw_ref[...], scb_ref[...])))))
    stages.append((lambda: proj_halo(C_RGX, C_XBC), lambda p: put(rg_ref, conv4(p, rcw_ref[...], rcb_ref[...]))))
    stages.append((lambda: proj(C_RGG, C_Z), lambda p: put(rg_ref, p, BRANCH_W)))
    stages.append((lambda: proj(C_Z, C_SC), lambda p: put(z_ref, p)))
    if not scan_only:
        stages.append((lambda: proj(C_FN, C_DT), lambda p: put(fn_ref, p)))
    stages.append((lambda: proj(C_DT, C_END), lambda p: put(dt_ref, p)))
    pending = None
    for matmuls, epilogue in stages:
        result = matmuls()
        if pending is not None:
            pending[1](pending[0])
        pending = (result, epilogue)
    pending[1](pending[0])


def _inproj(x, mod, pw, l, rowlen, tm, mod_row, scan_only=False):
    bsz, t, _ = x.shape
    whole = tm == t and mod_row is not None and tm & (tm - 1) == 0
    halo = 0 if whole else BF16_ROWS
    nb = bsz if whole else 1
    assert t % tm == 0 and tm % rowlen == 0 and rowlen & (rowlen - 1) == 0 and tm % BF16_ROWS == 0
    per_tile = tm // BF16_ROWS
    last_halo = t // BF16_ROWS - 1
    tok = lambda width: pl.BlockSpec((nb, tm, width), lambda b, i: (b, i, 0))
    lay = lambda *shape: _layer_spec(l, shape)
    out = lambda width, dt: jax.ShapeDtypeStruct((bsz, t, width), dt)
    x_specs, xs = [tok(D_MODEL)], [x]
    if halo:
        x_specs += [pl.BlockSpec((1, halo, D_MODEL), lambda b, i: (b, jnp.maximum(i * per_tile - 1, 0), 0)),
                    pl.BlockSpec((1, halo, D_MODEL), lambda b, i: (b, jnp.minimum((i + 1) * per_tile, last_halo), 0))]
        xs += [x, x]
    widths = [(1024, BF16), (SSD_XBC, BF16), (512, BF16), (LANES, F32)]
    if not scan_only:
        widths += [(1024, BF16), (512, BF16)]
    return pl.pallas_call(
        functools.partial(_inproj_kernel, rowlen=rowlen, mod_row=mod_row, halo=halo, scan_only=scan_only),
        grid=(bsz // nb, t // tm),
        in_specs=x_specs + [lay(1, D_MODEL),
                            pl.BlockSpec((None, SUBLANES, D_MODEL), lambda b, i: (l, 0, 0)),
                            pl.BlockSpec((None, SUBLANES, D_MODEL), lambda b, i: (l, 0, 1)),
                            _layer_spec(l, pw["w_in_a"].shape[1:], single_buffer=True),
                            _layer_spec(l, pw["w_in_b"].shape[1:], single_buffer=True),
                            lay(3, BRANCH_W), lay(4, BRANCH_W), lay(1, BRANCH_W),
                            lay(4, SSD_XBC), lay(1, SSD_XBC)],
        out_specs=[tok(w) for w, _ in widths],
        out_shape=[out(w, dt) for w, dt in widths],
        scratch_shapes=[pltpu.VMEM((nb * tm + 2 * BF16_ROWS, SSD_XBC), F32)],
        compiler_params=_params("arbitrary", "arbitrary"),
        name="inproj",
    )(*xs, pw["norm_w"], mod, mod, pw["w_in_a"], pw["w_in_b"], pw["sc_cw"], pw["rg_cw"], pw["rg_cb"], pw["ssd_cw"],
      pw["ssd_cb"])


def _outproj_kernel(yrg_ref, ysc_ref, yfn_ref, yssd_ref, w_ref, x_ref, g_ref, fw_ref, o_ref, *, final, mod_row):
    row = pl.program_id(0) if mod_row is None else mod_row
    nb, tm, _ = x_ref.shape
    flat = lambda ref: ref[...].reshape(nb * tm, ref.shape[2])
    acc = _dot(flat(yrg_ref), w_ref[0:512, :])
    acc += _dot(flat(ysc_ref), w_ref[512:1024, :])
    acc += _dot(flat(yfn_ref), w_ref[1024:1536, :])
    acc += _dot(flat(yssd_ref), w_ref[1536:2048, :])
    xn = flat(x_ref) + g_ref[pl.ds(row, 1), :] * acc
    if final:
        ms = jnp.mean(xn * xn, axis=-1, keepdims=True)
        xn = (xn * lax.rsqrt(ms + NORM_EPS)) * fw_ref[...]
    o_ref[...] = xn.reshape(nb, tm, D_MODEL)


def _outproj(y_rg, y_sc, y_fn, y_ssd, x, mod, pw, l, final, tm, mod_row):
    bsz, t, _ = x.shape
    nb = bsz if (tm == t and mod_row is not None) else 1
    ytok = pl.BlockSpec((nb, tm, BRANCH_W), lambda b, i: (b, i, 0))
    xtok = pl.BlockSpec((nb, tm, D_MODEL), lambda b, i: (b, i, 0))
    return pl.pallas_call(
        functools.partial(_outproj_kernel, final=final, mod_row=mod_row),
        grid=(bsz // nb, t // tm),
        in_specs=[ytok, ytok, ytok, ytok,
                  _layer_spec(l, (D_INNER, D_MODEL), single_buffer=True),
                  xtok,
                  pl.BlockSpec((None, SUBLANES, D_MODEL), lambda b, i: (l, 0, 2)),
                  pl.BlockSpec((1, D_MODEL), lambda b, i: (0, 0))],
        out_specs=xtok,
        out_shape=jax.ShapeDtypeStruct((bsz, t, D_MODEL), F32),
        compiler_params=_params("arbitrary", "arbitrary"),
        name="outproj",
    )(y_rg, y_sc, y_fn, y_ssd, pw["w_out"], x, mod, pw["final_w"])


MIN_NORMAL_F32 = float(np.finfo(np.float32).tiny)
LOG2E = float(np.log2(np.e))


def _rg_coefficients(uh_b, wg, bg_half, lam):
    uh = uh_b.astype(F32)
    th = jnp.tanh(_dot(uh_b, wg) + bg_half)
    k = (-0.5 * RG_C * np.log2(np.e)) * _softplus(-lam)
    a = jnp.exp2(k + k * th[:, :RG_HALF])
    x = 1.0 - a * a
    root = x * lax.rsqrt(jnp.maximum(x, MIN_NORMAL_F32))
    return a, root * (uh + uh * th[:, RG_HALF:])


def _scan_tile(a, v, h_in, sub, reverse):
    rows = a.shape[0]
    for s in (1, 2, 4):
        if reverse:
            keep = sub < SUBLANES - s
            a_sh = jnp.where(keep, pltpu.roll(a, rows - s, 0), 1.0)
            v_sh = jnp.where(keep, pltpu.roll(v, rows - s, 0), 0.0)
        else:
            keep = sub >= s
            a_sh = jnp.where(keep, pltpu.roll(a, s, 0), 1.0)
            v_sh = jnp.where(keep, pltpu.roll(v, s, 0), 0.0)
        v = v + a * v_sh
        a = a * a_sh
    ngroups = rows // SUBLANES
    out = [None] * ngroups
    h = h_in
    order = range(ngroups - 1, -1, -1) if reverse else range(ngroups)
    for g in order:
        lo = g * SUBLANES
        hg = v[lo:lo + SUBLANES] + a[lo:lo + SUBLANES] * h
        out[g] = hg
        h = hg[0:1] if reverse else hg[SUBLANES - 1:SUBLANES]
    return jnp.concatenate(out, axis=0), h


def _rg_kernel(u_ref, g_ref, wg_ref, bg_ref, lam_ref, h0_ref, y_ref, ht_ref, hb_ref, *, t, tt):
    nt = t // tt
    sub = lax.broadcasted_iota(jnp.int32, (tt, RG_HALF), 0) & (SUBLANES - 1)

    def coeffs(s, d):
        r0 = pl.multiple_of(s * tt, tt)
        return _rg_coefficients(u_ref[0, pl.ds(r0, tt), :], wg_ref[d], bg_ref[d], lam_ref[d:d + 1, :])

    def rev_body(i, h):
        s = nt - 1 - i
        a, v = coeffs(s, 1)
        hh, hn = _scan_tile(a, v, h, sub, True)
        hb_ref[pl.ds(pl.multiple_of(s * tt, tt), tt), :] = hh
        return hn

    h_rev = lax.fori_loop(0, nt, rev_body, h0_ref[0, 1:2, :])

    def fwd_body(s, h):
        a, v = coeffs(s, 0)
        hh, hn = _scan_tile(a, v, h, sub, False)
        r0 = pl.multiple_of(s * tt, tt)
        g = g_ref[0, pl.ds(r0, tt), :].astype(F32)
        y_ref[0, pl.ds(r0, tt), :] = ((hh + hb_ref[pl.ds(r0, tt), :]) * _silu(g)).astype(BF16)
        return hn

    h_fwd = lax.fori_loop(0, nt, fwd_body, h0_ref[0, 0:1, :])
    ht_ref[0] = jnp.concatenate([h_fwd, h_rev, jnp.zeros((SUBLANES - 2, RG_HALF), F32)], axis=0)


RG_GATHER = BF16_ROWS


def _rg_interleaved_kernel(u_ref, g_ref, wg_ref, bg_ref, lam_ref, perm_ref, permt_ref, h0_ref, y_ref,
                           s_ref, af_ref, ab_ref, up_ref, *, t, tt):
    nt = t // tt
    ng = tt // SUBLANES
    chunk = t // SUBLANES
    steps = tt // SUBLANES
    nsub = steps // RG_GATHER
    block = SUBLANES * RG_GATHER
    shape = (SUBLANES, RG_HALF)

    def source_rows(k, m, s):
        return pl.ds(pl.multiple_of(s * chunk + k * steps + m * RG_GATHER, RG_GATHER), RG_GATHER)

    def gather(ref, k):
        blocks = []
        for m in range(nsub):
            rows = jnp.concatenate([ref[0, source_rows(k, m, s), :] for s in range(SUBLANES)], axis=0)
            blocks.append(_dot(perm_ref[...], rows).astype(BF16))
        return jnp.concatenate(blocks, axis=0)

    def sweep(d):
        def body(i, carry):
            h, acc = carry
            s = i if d == 0 else nt - 1 - i
            r0 = pl.multiple_of(s * tt, tt)
            if d == 1:
                ub = gather(u_ref, s)
                up_ref[pl.ds(r0, tt), :] = ub
            else:
                ub = up_ref[pl.ds(r0, tt), :]
            a, v = _rg_coefficients(ub, wg_ref[d], bg_ref[d], lam_ref[d:d + 1, :])
            hs, ps = [None] * ng, [None] * ng
            for g in (range(ng) if d == 0 else range(ng - 1, -1, -1)):
                ag = a[g * SUBLANES:(g + 1) * SUBLANES]
                h = ag * h + v[g * SUBLANES:(g + 1) * SUBLANES]
                acc = ag * acc
                hs[g], ps[g] = h, acc
            local = jnp.concatenate(hs, axis=0)
            if d == 1:
                s_ref[pl.ds(r0, tt), :] = local
                ab_ref[pl.ds(r0, tt), :] = jnp.concatenate(ps, axis=0)
            else:
                s_ref[pl.ds(r0, tt), :] += local
                af_ref[pl.ds(r0, tt), :] = jnp.concatenate(ps, axis=0)
            return h, acc

        return lax.fori_loop(0, nt, body, (jnp.zeros(shape, F32), jnp.ones(shape, F32)), unroll=2)

    hb, pb = sweep(1)
    hf, pf = sweep(0)
    cf = [h0_ref[0, 0:1, :]]
    for s in range(SUBLANES - 1):
        cf.append(hf[s:s + 1] + pf[s:s + 1] * cf[s])
    cb = [None] * SUBLANES
    cb[SUBLANES - 1] = h0_ref[0, 1:2, :]
    for s in range(SUBLANES - 1, 0, -1):
        cb[s - 1] = hb[s:s + 1] + pb[s:s + 1] * cb[s]
    cf = jnp.concatenate(cf, axis=0)[None]
    cb = jnp.concatenate(cb, axis=0)[None]

    def fix(s, carry):
        r0 = pl.multiple_of(s * tt, tt)
        corr = (af_ref[pl.ds(r0, tt), :].reshape(ng, SUBLANES, RG_HALF) * cf
                + ab_ref[pl.ds(r0, tt), :].reshape(ng, SUBLANES, RG_HALF) * cb).reshape(tt, RG_HALF)
        g = gather(g_ref, s).astype(F32)
        y = ((s_ref[pl.ds(r0, tt), :] + corr) * _silu(g)).astype(BF16)
        for m in range(nsub):
            back = _dot(permt_ref[...], y[m * block:(m + 1) * block]).astype(BF16)
            for c in range(SUBLANES):
                y_ref[0, source_rows(s, m, c), :] = back[c * RG_GATHER:(c + 1) * RG_GATHER]
        return carry

    lax.fori_loop(0, nt, fix, 0, unroll=4 if nt % 4 == 0 else 1)


def _rg(rg, pw, l, h0, tt, interleaved):
    bsz, t, _ = rg.shape
    assert t % tt == 0
    nhalf = BRANCH_W // RG_HALF
    seq = lambda col0: pl.BlockSpec((1, t, RG_HALF), lambda b, h: (b, 0, col0 + h))
    in_specs = [seq(0), seq(nhalf),
                pl.BlockSpec((None, 2, None, RG_HALF, 2 * RG_HALF), lambda b, h: (l, 0, h, 0, 0)),
                pl.BlockSpec((None, 2, None, 1, 2 * RG_HALF), lambda b, h: (l, 0, h, 0, 0)),
                pl.BlockSpec((None, 2, RG_HALF), lambda b, h: (l, 0, h)),
                pl.BlockSpec((1, SUBLANES, RG_HALF), lambda b, h: (b, 0, h))]
    args = (rg, rg, pw["rg_wg"], pw["rg_bg"], pw["rg_lam"], h0)
    y_spec = pl.BlockSpec((1, t, RG_HALF), lambda b, h: (b, 0, h))
    y_shape = jax.ShapeDtypeStruct((bsz, t, BRANCH_W), BF16)
    seq_scratch = pltpu.VMEM((t, RG_HALF), F32)
    if interleaved:
        block = SUBLANES * RG_GATHER
        assert tt % block == 0
        pos = np.arange(block)
        perm = np.zeros((block, block))
        perm[pos, RG_GATHER * (pos % SUBLANES) + pos // SUBLANES] = 1.0
        perm = jnp.asarray(perm, dtype=F32).astype(BF16)
        pspec = pl.BlockSpec((block, block), lambda b, h: (0, 0))
        y = pl.pallas_call(
            functools.partial(_rg_interleaved_kernel, t=t, tt=tt),
            grid=(bsz, nhalf), in_specs=in_specs[:5] + [pspec, pspec] + in_specs[5:],
            out_specs=y_spec, out_shape=y_shape,
            scratch_shapes=[seq_scratch, seq_scratch, seq_scratch, pltpu.VMEM((t, RG_HALF), BF16)],
            compiler_params=_params("arbitrary", "arbitrary"), name="rglru_interleaved",
        )(*args[:5], perm, perm.T, args[5])
        return y, None
    return pl.pallas_call(
        functools.partial(_rg_kernel, t=t, tt=tt),
        grid=(bsz, nhalf), in_specs=in_specs,
        out_specs=[y_spec, pl.BlockSpec((1, SUBLANES, RG_HALF), lambda b, h: (b, 0, h))],
        out_shape=[y_shape, jax.ShapeDtypeStruct((bsz, SUBLANES, BRANCH_W), F32)],
        scratch_shapes=[seq_scratch],
        compiler_params=_params("arbitrary", "arbitrary"), name="rglru",
    )(*args)


def _ssd_kernel(xbc_ref, z_ref, dt_ref, dtb_ref, alog_ref, dsk_ref, nw_ref, ecol_ref, ehead_ref,
                hmask_ref, h0_ref, y_ref, ht_ref, yb_ref, cs_ref, adjt_ref, diagt_ref, wgtt_ref, etot_ref,
                loc_ref, st_ref, *, t):
    q = SSD_CHUNK
    nc = t // q
    npairs = SSD_HEADS // 2
    pairs_per_group = npairs // SSD_GROUPS
    lane1 = lax.broadcasted_iota(jnp.int32, (1, LANES), 1)
    a_row = -LOG2E * jnp.exp(alog_ref[...])
    fwd_lane = (lax.broadcasted_iota(jnp.int32, (q, LANES), 1) & (SSD_DT - 1)) < SSD_HEADS
    ri = lax.broadcasted_iota(jnp.int32, (q, q), 0)
    ci = lax.broadcasted_iota(jnp.int32, (q, q), 1)
    lower_b = jnp.where(ri >= ci, 1.0, 0.0).astype(BF16)
    head0_s = lax.broadcasted_iota(jnp.int32, (SSD_STATE, LANES), 1) < SSD_HD

    def pair_vec(row, l0, l1):
        return jnp.where(lane1 < SSD_HD, row[:, l0:l0 + 1], row[:, l1:l1 + 1])

    def small_rows(c):
        return pl.ds(pl.multiple_of(c * SSD_DT, SSD_DT), SSD_DT)

    def pass0(c, carry):
        r0 = pl.multiple_of(c * q, q)
        dtv = _softplus(dt_ref[0, pl.ds(r0, q), :] + dtb_ref[...])
        da = dtv * a_row
        pre = sum(_dot(lower_b, part.astype(BF16)) for part in _split3(da))
        tot = pre[q - 1:q, :]
        cs = jnp.where(fwd_lane, pre, tot - pre + da)
        cs_ref[pl.ds(r0, q), :] = cs
        adjt_ref[small_rows(c), :] = (cs - LOG2E * jnp.log(dtv)).T[:SSD_DT]
        diagt_ref[small_rows(c), :] = (LOG2E * jnp.log(dtv + pltpu.roll(dtv, SSD_HEADS, 1))).T[:SSD_DT]
        wgtt_ref[small_rows(c), :] = (dtv * jnp.exp2(tot - cs)).T[:SSD_DT]
        etot_ref[small_rows(c), :] = jnp.broadcast_to(jnp.exp2(tot), (SSD_DT, LANES))
        return carry

    def pass1(i, carry):
        c = nc - 1 - i
        r0 = pl.multiple_of(c * q, q)
        xs_b = xbc_ref[0, pl.ds(r0, q), 0:BRANCH_W]
        bm_b = xbc_ref[0, pl.ds(r0, q), BRANCH_W:BRANCH_W + LANES]
        cm_b = xbc_ref[0, pl.ds(r0, q), BRANCH_W + LANES:SSD_XBC]
        cs_p = _pack3(cs_ref[pl.ds(r0, q), :])
        cs_col = _dot(cs_p, ecol_ref[...])
        ecs_b = jnp.exp2(_dot(cs_p, ehead_ref[:, BRANCH_W:]))
        adjt = adjt_ref[small_rows(c), :]
        diagt = diagt_ref[small_rows(c), :]
        wgtt = wgtt_ref[small_rows(c), :]
        etot = etot_ref[pl.ds(pl.multiple_of(c * SSD_DT, SSD_DT), 1), :]
        bt = bm_b.astype(F32).T
        xs_h0 = xs_b * hmask_ref[0:1, :]
        xs_h1 = xs_b * hmask_ref[1:2, :]
        ys = []
        for g in range(SSD_GROUPS):
            gs = slice(g * SSD_STATE, (g + 1) * SSD_STATE)
            cbm = lax.dot_general(cm_b[:, gs], bm_b[:, gs], (((1,), (1,)), ((), ())), preferred_element_type=F32)
            btg = bt[gs, :]
            for pp in range(pairs_per_group):
                pair = g * pairs_per_group + pp
                ps = slice(pair * LANES, (pair + 1) * LANES)
                state = st_ref[pair]
                lf0 = 2 * pair
                lb0 = SSD_HEADS + lf0
                ms = []
                for lf in (lf0, lf0 + 1):
                    lb = SSD_HEADS + lf
                    arg = jnp.where(ri > ci, cs_col[:, lf * LANES:(lf + 1) * LANES] - adjt[lf:lf + 1, :],
                                    jnp.where(ri < ci, cs_col[:, lb * LANES:(lb + 1) * LANES] - adjt[lb:lb + 1, :],
                                              diagt[lf:lf + 1, :]))
                    ms.append((cbm * jnp.exp2(arg)).astype(BF16))
                yd = _dot(jnp.concatenate(ms, axis=1), jnp.concatenate([xs_h0[:, ps], xs_h1[:, ps]], axis=0))
                yo = _dot(cm_b[:, gs], state.astype(BF16)) * ecs_b[:, ps]
                ys.append(yd + yo)
                stack = jnp.concatenate([(btg * wgtt[l:l + 1, :]).astype(BF16) for l in (lf0, lf0 + 1, lb0, lb0 + 1)],
                                        axis=0)
                res = _dot(stack, xs_b[:, ps])
                n = SSD_STATE
                loc_ref[c, pair] = jnp.where(head0_s, res[0:n], res[n:2 * n])
                st_ref[pair] = state * pair_vec(etot, lb0, lb0 + 1) + jnp.where(head0_s, res[2 * n:3 * n], res[3 * n:])
        yb_ref[pl.ds(r0, q), :] = jnp.concatenate(ys, axis=1) + dsk_ref[...] * xs_b.astype(F32)
        return carry

    def pass2(c, carry):
        r0 = pl.multiple_of(c * q, q)
        cs = cs_ref[pl.ds(r0, q), :]
        cm_b = xbc_ref[0, pl.ds(r0, q), BRANCH_W + LANES:SSD_XBC]
        ecs_f = jnp.exp2(_dot(_pack3(cs), ehead_ref[:, :BRANCH_W]))
        etot = jnp.exp2(cs[q - 1:q, :])
        ys = []
        for pair in range(npairs):
            g = pair // pairs_per_group
            state = st_ref[pair]
            lf0 = 2 * pair
            ys.append(_dot(cm_b[:, g * SSD_STATE:(g + 1) * SSD_STATE], state.astype(BF16))
                      * ecs_f[:, pair * LANES:(pair + 1) * LANES])
            st_ref[pair] = state * pair_vec(etot, lf0, lf0 + 1) + loc_ref[c, pair]
        yt = yb_ref[pl.ds(r0, q), :] + jnp.concatenate(ys, axis=1)
        yt = yt * _silu(z_ref[0, pl.ds(r0, q), :].astype(F32))
        ms = jnp.mean(yt * yt, axis=-1, keepdims=True)
        y_ref[0, pl.ds(r0, q), :] = ((yt * lax.rsqrt(ms + NORM_EPS)) * nw_ref[...]).astype(BF16)
        return carry

    lax.fori_loop(0, nc, pass0, 0, unroll=8 if nc % 8 == 0 else 1)
    for p in range(npairs):
        st_ref[p] = h0_ref[0, 1, p]
    lax.fori_loop(0, nc, pass1, 0, unroll=4 if nc % 4 == 0 else 2)
    for p in range(npairs):
        ht_ref[0, 1, p] = st_ref[p]
        st_ref[p] = h0_ref[0, 0, p]
    lax.fori_loop(0, nc, pass2, 0, unroll=8 if nc % 8 == 0 else 2)
    for p in range(npairs):
        ht_ref[0, 0, p] = st_ref[p]


@functools.lru_cache(maxsize=None)
def _ssd_tables():
    lanes = np.arange(LANES)[:, None]
    src = np.where(lanes < 3 * SSD_DT, lanes % SSD_DT, -1)
    ecol = (src == np.arange(SSD_DT * LANES)[None, :] // LANES)
    ehead = (src == np.arange(2 * BRANCH_W)[None, :] // SSD_HD)
    even_head = (np.arange(BRANCH_W) // SSD_HD) % 2 == 0
    hmask = np.zeros((BF16_ROWS, BRANCH_W))
    hmask[0] = even_head
    hmask[1] = ~even_head
    as_bf16 = lambda a: jnp.asarray(a, dtype=F32).astype(BF16)
    return as_bf16(ecol), as_bf16(ehead), as_bf16(hmask)


def _ssd(xbc, z, dt, pw, l, h0):
    bsz, t, _ = xbc.shape
    assert t % SSD_CHUNK == 0
    npairs = SSD_HEADS // 2
    ecol, ehead, hmask = _ssd_tables()
    full = lambda shape: pl.BlockSpec(shape, lambda b: (0,) * len(shape))
    lay = lambda *shape: _layer_spec(l, shape)
    seq = lambda width: pl.BlockSpec((1, t, width), lambda b: (b, 0, 0))
    st_spec = pl.BlockSpec((1, 2, npairs, SSD_STATE, LANES), lambda b: (b, 0, 0, 0, 0))
    return pl.pallas_call(
        functools.partial(_ssd_kernel, t=t),
        grid=(bsz,),
        in_specs=[seq(SSD_XBC), seq(BRANCH_W), seq(LANES), lay(1, LANES), lay(1, LANES),
                  lay(1, BRANCH_W), lay(1, BRANCH_W), full(ecol.shape), full(ehead.shape), full(hmask.shape),
                  st_spec],
        out_specs=[seq(BRANCH_W), st_spec],
        out_shape=[jax.ShapeDtypeStruct((bsz, t, BRANCH_W), BF16),
                   jax.ShapeDtypeStruct((bsz, 2, npairs, SSD_STATE, LANES), F32)],
        scratch_shapes=[pltpu.VMEM((t, BRANCH_W), F32), pltpu.VMEM((t, LANES), F32)]
        + [pltpu.VMEM((t // SSD_CHUNK * SSD_DT, LANES), F32)] * 4
        + [pltpu.VMEM((t // SSD_CHUNK, npairs, SSD_STATE, LANES), F32),
                        pltpu.VMEM((npairs, SSD_STATE, LANES), F32)],
        compiler_params=_params("arbitrary"),
        name="ssd",
    )(xbc, z, dt, pw["ssd_dtb"], pw["ssd_alog"], pw["ssd_dskip"], pw["ssd_nw"], ecol, ehead, hmask, h0)


@functools.lru_cache(maxsize=None)
def _fourier_tables(n, blk):
    half = n // 2
    nh = half // blk
    assert nh * blk == half and nh <= SUBLANES
    p = np.arange(FN_GD)
    ang_c = 2.0 * np.pi * ((p[:, None] * p[None, :]) % FN_GD) / FN_GD
    eye = np.eye(BRANCH_W // FN_GD)
    cc = np.kron(eye, np.cos(ang_c))
    sc = np.kron(eye, np.sin(ang_c))
    k = np.arange(half)
    ang_t = 2.0 * np.pi * ((k[:, None] * k[None, :]) % n) / n
    ct = np.cos(ang_t)
    stn = -np.sin(ang_t)
    assert blk % FN_FLIP == 0
    jp = np.zeros((FN_FLIP, FN_FLIP))
    r = np.arange(1, FN_FLIP)
    jp[r, FN_FLIP - r] = 1.0
    tt = np.arange(n)
    ks = blk * (np.arange(nh) + 1)
    ang_k = 2.0 * np.pi * ((ks[:, None] * tt[None, :]) % n) / n
    tc = np.zeros((2 * SUBLANES, n))
    ts = np.zeros((2 * SUBLANES, n))
    tc[:nh] = np.cos(ang_k)
    ts[:nh] = np.sin(ang_k)
    tc[SUBLANES, half] = 1.0
    as_bf16 = lambda a: jnp.asarray(a, dtype=F32).astype(BF16)
    return dict(cc=as_bf16(cc), sc=as_bf16(sc), ct=as_bf16(ct), stn=as_bf16(stn), jp=as_bf16(jp),
                tc=as_bf16(tc), ts=as_bf16(ts))


def _mirror(jp_ref, src, row0):
    n = src.shape[0]
    nsub = n // FN_FLIP
    first = lax.broadcasted_iota(jnp.int32, (FN_FLIP, src.shape[1]), 0) == 0
    out = []
    for a in range(nsub):
        s = nsub - 1 - a
        head = row0 if a == 0 else src[(s + 1) * FN_FLIP:(s + 1) * FN_FLIP + 1].astype(F32)
        out.append(jnp.where(first, head, _dot(jp_ref[...], src[s * FN_FLIP:(s + 1) * FN_FLIP])))
    return jnp.concatenate(out, axis=0)


def _fn_fold_kernel(xj_ref, xm_ref, xr_ref, jp_ref, cc_ref, sc_ref, eo_ref):
    j = pl.program_id(1)
    xj = xj_ref[0]
    row0 = xr_ref[0, 0:1, :].astype(F32) * jnp.where(j == 0, 0.0, 1.0)
    mir = _mirror(jp_ref, xm_ref[0], row0).astype(BF16)
    e = _dot(xj, cc_ref[...]) + _dot(mir, cc_ref[...])
    o = _dot(xj, sc_ref[...]) - _dot(mir, sc_ref[...])
    eo_ref[0] = jnp.concatenate([e, o], axis=1).astype(BF16)


def _fn_aux_kernel(x_ref, tc_ref, ts_ref, cc_ref, sc_ref, o_ref):
    x = x_ref[0]
    xc = _dot(tc_ref[...], x).astype(BF16)
    xs = _dot(ts_ref[...], x).astype(BF16)
    o_ref[0] = _dot(xc, cc_ref[...]) + _dot(xs, sc_ref[...])


def _fn_main_kernel(ct_ref, stn_ref, eo_ref, aux_ref, g_ref, jp_ref, y_ref, *, scale):
    m = pl.program_id(1)
    blk = ct_ref.shape[0]
    nblk = y_ref.shape[1] // blk
    rows = lax.broadcasted_iota(jnp.int32, (blk, BRANCH_W), 0)
    p = _dot(ct_ref[...], eo_ref[0, :, 0:BRANCH_W])
    qn = _dot(stn_ref[...], eo_ref[0, :, BRANCH_W:2 * BRANCH_W])
    sgn = (1 - 2 * ((m * blk + rows) & 1)).astype(F32)
    p = p + sgn * aux_ref[0, SUBLANES:SUBLANES + 1, :]
    lo = pl.multiple_of(m * blk, blk)
    hi = pl.multiple_of((nblk - 1 - m) * blk, blk)
    y_ref[0, pl.ds(lo, blk), :] = ((p + qn) * (_silu(g_ref[0, pl.ds(lo, blk), :].astype(F32)) * scale)).astype(BF16)
    sel = lax.broadcasted_iota(jnp.int32, (2 * SUBLANES, BRANCH_W), 0) == m
    row0 = jnp.sum(jnp.where(sel, aux_ref[0], 0.0), axis=0, keepdims=True)
    y_ref[0, pl.ds(hi, blk), :] = (_mirror(jp_ref, (p - qn).astype(BF16), row0)
                                   * (_silu(g_ref[0, pl.ds(hi, blk), :].astype(F32)) * scale)).astype(BF16)


def _fourier(fn, blk):
    bsz, n, _ = fn.shape
    tb = _fourier_tables(n, blk)
    half = n // 2
    nh = half // blk
    nblk = 2 * nh
    rows16 = n // BF16_ROWS
    full2 = lambda shape: pl.BlockSpec(shape, lambda b, j: (0,) * len(shape))
    eo = pl.pallas_call(
        _fn_fold_kernel,
        grid=(bsz, nh),
        in_specs=[pl.BlockSpec((1, blk, BRANCH_W), lambda b, j: (b, j, 0)),
                  pl.BlockSpec((1, blk, BRANCH_W), lambda b, j: (b, nblk - 1 - j, 0)),
                  pl.BlockSpec((1, BF16_ROWS, BRANCH_W),
                               lambda b, j: (b, jnp.minimum((blk // BF16_ROWS) * (nblk - j), rows16 - 1), 0)),
                  full2((FN_FLIP, FN_FLIP)), full2((BRANCH_W, BRANCH_W)), full2((BRANCH_W, BRANCH_W))],
        out_specs=pl.BlockSpec((1, blk, 2 * BRANCH_W), lambda b, j: (b, j, 0)),
        out_shape=jax.ShapeDtypeStruct((bsz, half, 2 * BRANCH_W), BF16),
        compiler_params=_params("arbitrary", "arbitrary"),
        name="fn_fold",
    )(fn, fn, fn, tb["jp"], tb["cc"], tb["sc"])
    full1 = lambda shape: pl.BlockSpec(shape, lambda b: (0,) * len(shape))
    aux = pl.pallas_call(
        _fn_aux_kernel,
        grid=(bsz,),
        in_specs=[pl.BlockSpec((1, n, BRANCH_W), lambda b: (b, 0, 0)),
                  full1((2 * SUBLANES, n)), full1((2 * SUBLANES, n)),
                  full1((BRANCH_W, BRANCH_W)), full1((BRANCH_W, BRANCH_W))],
        out_specs=pl.BlockSpec((1, 2 * SUBLANES, BRANCH_W), lambda b: (b, 0, 0)),
        out_shape=jax.ShapeDtypeStruct((bsz, 2 * SUBLANES, BRANCH_W), F32),
        compiler_params=_params("arbitrary"),
        name="fn_aux",
    )(fn, tb["tc"], tb["ts"], tb["cc"], tb["sc"])
    return pl.pallas_call(
        functools.partial(_fn_main_kernel, scale=float(1.0 / np.sqrt(n * FN_GD))),
        grid=(bsz, nh),
        in_specs=[pl.BlockSpec((blk, half), lambda b, m: (m, 0)),
                  pl.BlockSpec((blk, half), lambda b, m: (m, 0)),
                  pl.BlockSpec((1, half, 2 * BRANCH_W), lambda b, m: (b, 0, 0)),
                  pl.BlockSpec((1, 2 * SUBLANES, BRANCH_W), lambda b, m: (b, 0, 0)),
                  pl.BlockSpec((1, n, BRANCH_W), lambda b, m: (b, 0, 1)),
                  full2((FN_FLIP, FN_FLIP))],
        out_specs=pl.BlockSpec((1, n, BRANCH_W), lambda b, m: (b, 0, 0)),
        out_shape=jax.ShapeDtypeStruct((bsz, n, BRANCH_W), BF16),
        compiler_params=_params("arbitrary", "arbitrary"),
        name="fn_main",
    )(tb["ct"], tb["stn"], eo, aux, fn, tb["jp"])


def _prepare_weights(norm_w, w_in, w_out, rg_conv_w, rg_conv_b, rg_gate_a_w, rg_gate_a_b, rg_gate_x_w, rg_gate_x_b,
                     rg_lambda, sc_conv_w, ssd_conv_w, ssd_conv_b, ssd_dt_bias, ssd_a_log, ssd_d, ssd_norm_w,
                     final_norm_w):
    depth = w_in.shape[0]
    assert REF_DT_HI - REF_DT_LO == SSD_DT and w_in.shape[2] - SSD_DT + LANES == C_END
    w_a = jnp.concatenate([w_in[:, :, :REF_DT_LO], jnp.tile(w_in[:, :, REF_DT_LO:REF_DT_HI], (1, 1, LANES // SSD_DT))],
                          axis=2).astype(BF16)
    w_b = w_in[:, :, REF_DT_HI:].astype(BF16)
    heads_per_half = RG_HALF // RG_HD
    eye = jnp.eye(heads_per_half, dtype=F32)

    def blockdiag(wg):
        wg = wg.reshape(depth, 2, -1, heads_per_half, RG_HD, RG_HD)
        return jnp.einsum("ldhjio,jk->ldhjiko", wg, eye).reshape(depth, 2, -1, RG_HALF, RG_HALF)

    halves = lambda b: b.reshape(depth, 2, -1, 1, RG_HALF)
    rep = LANES // SSD_DT
    return dict(
        norm_w=norm_w.reshape(depth, 1, D_MODEL), final_w=final_norm_w.reshape(1, D_MODEL),
        w_in_a=w_a, w_in_b=w_b, w_out=w_out.astype(BF16), sc_cw=sc_conv_w,
        rg_cw=0.5 * rg_conv_w, rg_cb=0.5 * rg_conv_b.reshape(depth, 1, BRANCH_W),
        rg_wg=jnp.concatenate([blockdiag(rg_gate_a_w), blockdiag(rg_gate_x_w)], axis=-1).astype(BF16),
        rg_bg=0.5 * jnp.concatenate([halves(rg_gate_a_b), halves(rg_gate_x_b)], axis=-1),
        rg_lam=rg_lambda,
        ssd_cw=ssd_conv_w, ssd_cb=ssd_conv_b.reshape(depth, 1, SSD_XBC),
        ssd_dtb=jnp.tile(ssd_dt_bias.reshape(depth, 1, SSD_DT), (1, 1, rep)),
        ssd_alog=jnp.tile(ssd_a_log.reshape(depth, 1, SSD_DT), (1, 1, rep)),
        ssd_dskip=jnp.repeat(ssd_d, SSD_HD, axis=1).reshape(depth, 1, BRANCH_W),
        ssd_nw=ssd_norm_w.reshape(depth, 1, BRANCH_W),
    )


def _mix(x, mod, pw, l, mod_row, rowlen, tm, tt, blk, interleave, rg_h0, ssd_h0, with_output, final):
    rg, xbc, z, dt, *rest = _inproj(x, mod, pw, l, rowlen, tm, mod_row, scan_only=not with_output)
    y_rg, rg_st = _rg(rg, pw, l, rg_h0, tt, interleave)
    y_ssd, ssd_st = _ssd(xbc, z, dt, pw, l, ssd_h0)
    if not with_output:
        return None, rg_st, ssd_st
    fn, y_sc = rest
    y_fn = _fourier(fn, blk)
    return _outproj(y_rg, y_sc, y_fn, y_ssd, x, mod, pw, l, final, tm, mod_row), rg_st, ssd_st


def kernel(x, c, ctx, c_ctx, ada_w, ada_b, norm_w, w_in, w_out, rg_conv_w, rg_conv_b, rg_gate_a_w, rg_gate_a_b,
           rg_gate_x_w, rg_gate_x_b, rg_lambda, sc_conv_w, ssd_conv_w, ssd_conv_b, ssd_dt_bias, ssd_a_log, ssd_d,
           ssd_norm_w, final_norm_w):
    bsz, seq, _ = x.shape
    ctx_len = ctx.shape[1]
    depth = w_in.shape[0]
    assert bsz + 1 <= SUBLANES and seq % SUBLANES == 0
    pw = _prepare_weights(norm_w, w_in, w_out, rg_conv_w, rg_conv_b, rg_gate_a_w, rg_gate_a_b, rg_gate_x_w,
                          rg_gate_x_b, rg_lambda, sc_conv_w, ssd_conv_w, ssd_conv_b, ssd_dt_bias, ssd_a_log, ssd_d,
                          ssd_norm_w, final_norm_w)
    c8 = jnp.concatenate([c, c_ctx[None, :], jnp.zeros((SUBLANES - bsz - 1, D_MODEL), F32)], axis=0)
    mod = _ada(c8, ada_w, ada_b.reshape(depth, 1, 3 * D_MODEL))
    zeros_rg = jnp.zeros((bsz, SUBLANES, BRANCH_W), F32)
    zeros_ssd = jnp.zeros((bsz, 2, SSD_HEADS // 2, SSD_STATE, LANES), F32)
    tm_lat = min(1024, seq)
    tt_lat = min(1024, seq)
    tt_ctx = min(256, ctx_len)
    blk_lat = min(512, seq // 2)
    blk_ctx = min(512, ctx_len // 2)
    for l in range(depth):
        last = l == depth - 1
        new_ctx, rg_st, ssd_st = _mix(ctx, mod, pw, l, bsz, ctx_len, ctx_len, tt_ctx, blk_ctx, False,
                                      zeros_rg, zeros_ssd, not last, False)
        x, _, _ = _mix(x, mod, pw, l, None, GRID_W, tm_lat, tt_lat, blk_lat, True, rg_st, ssd_st, True, last)
        if not last:
            ctx = new_ctx
    return x
```

```python
import functools

import numpy as np
import jax
import jax.numpy as jnp
from jax import lax
from jax.experimental import pallas as pl
from jax.experimental.pallas import tpu as pltpu

F32 = jnp.float32
BF16 = jnp.bfloat16
HIGHEST = lax.Precision.HIGHEST

D_MODEL = 1024
D_INNER = 2048
BRANCH_W = 512
GRID_W = 64
RG_HD = 64
RG_C = 8.0
RG_HALF = 256
SSD_HEADS = 8
SSD_HD = 64
SSD_GROUPS = 2
SSD_STATE = 64
SSD_XBC = BRANCH_W + 2 * SSD_GROUPS * SSD_STATE
SSD_CHUNK = 128
SSD_DT = 2 * SSD_HEADS
FN_GD = 128
FN_FLIP = 128
NORM_EPS = 1e-6

LANES = 128
SUBLANES = 8
BF16_ROWS = 16
VMEM_LIMIT_BYTES = 60000 * 1024

C_RGX, C_XBC, C_RGG, C_Z, C_SC, C_FN, C_DT, C_END = 0, 512, 1280, 1792, 2304, 4352, 5376, 5504
REF_DT_LO, REF_DT_HI = 1280, 1296


def _sigmoid(v):
    return 0.5 + 0.5 * jnp.tanh(0.5 * v)


def _silu(v):
    h = 0.5 * v
    return h + h * jnp.tanh(h)


def _softplus(v):
    return jnp.maximum(v, 0.0) + jnp.log1p(jnp.exp(-jnp.abs(v)))


def _dot(a, b):
    return jnp.dot(a, b, preferred_element_type=F32)


def _split3(v):
    hi = v.astype(BF16).astype(F32)
    rest = v - hi
    mid = rest.astype(BF16).astype(F32)
    return hi, mid, rest - mid


def _pack3(v):
    hi, mid, lo = _split3(v)
    period = lax.broadcasted_iota(jnp.int32, v.shape, 1) // SSD_DT
    return jnp.where(period == 0, hi, jnp.where(period == 1, mid, jnp.where(period == 2, lo, 0.0))).astype(BF16)


def _params(*semantics):
    return pltpu.CompilerParams(dimension_semantics=semantics, vmem_limit_bytes=VMEM_LIMIT_BYTES)


def _layer_spec(l, shape, single_buffer=False):
    mode = dict(pipeline_mode=pl.Buffered(1)) if single_buffer else {}
    return pl.BlockSpec((None,) + tuple(shape), lambda *_: (l,) + (0,) * len(shape), **mode)


def _ada_kernel(c_ref, w_ref, b_ref, o_ref):
    o_ref[...] = jnp.dot(_silu(c_ref[...]), w_ref[...], precision=HIGHEST,
                         preferred_element_type=F32) + b_ref[...]


def _ada(c8, w, b):
    depth = w.shape[0]
    tn = 512
    return pl.pallas_call(
        _ada_kernel,
        grid=(depth, 3 * D_MODEL // tn),
        in_specs=[pl.BlockSpec((SUBLANES, D_MODEL), lambda l, j: (0, 0)),
                  pl.BlockSpec((None, D_MODEL, tn), lambda l, j: (l, 0, j)),
                  pl.BlockSpec((None, 1, tn), lambda l, j: (l, 0, j))],
        out_specs=pl.BlockSpec((None, SUBLANES, tn), lambda l, j: (l, 0, j)),
        out_shape=jax.ShapeDtypeStruct((depth, SUBLANES, 3 * D_MODEL), F32),
        compiler_params=_params("arbitrary", "arbitrary"),
        name="ada",
    )(c8, w, b)


def _inproj_kernel(*refs, rowlen, mod_row, halo, scan_only):
    if halo:
        x_ref, xp_ref, xn_ref, *refs = refs
    else:
        x_ref, *refs = refs
    nw_ref, sh_ref, sc_ref, wa_ref, wb_ref, cw_ref, rcw_ref, rcb_ref, scw_ref, scb_ref, *outs, tap_ref = refs
    nb, tm, _ = x_ref.shape
    rows = nb * tm
    i = pl.program_id(1)
    row = pl.program_id(0) if mod_row is None else mod_row
    if halo:
        x = jnp.concatenate([xp_ref[0], x_ref[0], xn_ref[0]], axis=0)
    else:
        x = x_ref[...].reshape(rows, D_MODEL)
    ms = jnp.mean(x * x, axis=-1, keepdims=True)
    gain = nw_ref[...] * (1.0 + sc_ref[pl.ds(row, 1), :])
    hx = ((x * lax.rsqrt(ms + NORM_EPS)) * gain + sh_ref[pl.ds(row, 1), :]).astype(BF16)
    hb = hx[halo:halo + rows]

    def wcols(lo, hi):
        if hi <= REF_DT_LO:
            return wa_ref[:, lo:hi]
        if lo >= C_DT:
            return wa_ref[:, lo - C_DT + REF_DT_LO:hi - C_DT + REF_DT_LO]
        assert lo >= REF_DT_LO and hi <= C_DT
        return wb_ref[:, lo - REF_DT_LO:hi - REF_DT_LO]

    def proj(lo, hi):
        return _dot(hb, wcols(lo, hi))

    def proj_halo(lo, hi):
        return _dot(hx, wcols(lo, hi))

    def conv4(p, cw, cb):
        n = p.shape[0]
        if halo:
            keep_prev = jnp.where(i == 0, 0.0, 1.0)
            keep_next = jnp.where(i == pl.num_programs(1) - 1, 0.0, 1.0)
            w = p.shape[1]
            tap_ref[0:halo, 0:w] = p[:halo] * keep_prev
            tap_ref[halo:halo + rows, 0:w] = p[halo:halo + rows]
            tap_ref[halo + rows:n, 0:w] = p[halo + rows:] * keep_next
            taps = [tap_ref[halo + off:halo + off + rows, 0:w] for off in (-2, -1)]
            taps += [p[halo:halo + rows], tap_ref[halo + 1:halo + 1 + rows, 0:w]]
        else:
            pos = lax.broadcasted_iota(jnp.int32, p.shape, 0) & (tm - 1)
            taps = [jnp.where(pos >= 2, pltpu.roll(p, 2, 0), 0.0), jnp.where(pos >= 1, pltpu.roll(p, 1, 0), 0.0), p,
                    jnp.where(pos <= tm - 2, pltpu.roll(p, n - 1, 0), 0.0)]
        return sum(cw[k:k + 1, :] * taps[k] for k in range(4)) + cb

    def put(ref, value, lo=0):
        ref[:, :, lo:lo + value.shape[1]] = value.reshape(nb, tm, value.shape[1]).astype(ref.dtype)

    stages = []
    if scan_only:
        rg_ref, xbc_ref, z_ref, dt_ref = outs
    else:
        rg_ref, xbc_ref, z_ref, dt_ref, fn_ref, ysc_ref = outs
        conv3 = []

        def sc_conv(ps):
            v = ps[0] * ps[1]
            pos = lax.broadcasted_iota(jnp.int32, (rows, BRANCH_W), 0) & (rowlen - 1)
            pad = SUBLANES
            edge = jnp.zeros((pad, BRANCH_W), F32)
            tap_ref[0:pad, 0:BRANCH_W] = edge
            tap_ref[pad:pad + rows, 0:BRANCH_W] = v
            tap_ref[pad + rows:2 * pad + rows, 0:BRANCH_W] = edge
            vm1 = jnp.where(pos == 0, 0.0, tap_ref[pad - 1:pad - 1 + rows, 0:BRANCH_W])
            vp1 = jnp.where(pos == rowlen - 1, 0.0, tap_ref[pad + 1:pad + 1 + rows, 0:BRANCH_W])
            conv3.append(cw_ref[0:1, :] * vm1 + cw_ref[1:2, :] * v + cw_ref[2:3, :] * vp1)

        stages.append((lambda: (proj(C_SC + 512, C_SC + 1024), proj(C_SC + 1024, C_SC + 1536)), sc_conv))
        stages.append((lambda: (proj(C_SC, C_SC + 512), proj(C_SC + 1536, C_FN)),
                       lambda ps: put(ysc_ref, ps[0] * conv3[0] * _silu(ps[1]))))
    stages.append((lambda: proj_halo(C_XBC, C_RGG),
                   lambda p: put(xbc_ref, _silu(conv4(p, scw_ref[...], scb_ref[...])))))
    stages.append((lambda: proj_halo(C_RGX, C_XBC), lambda p: put(rg_ref, conv4(p, rcw_ref[...], rcb_ref[...]))))
    stages.append((lambda: proj(C_RGG, C_Z), lambda p: put(rg_ref, p, BRANCH_W)))
    stages.append((lambda: proj(C_Z, C_SC), lambda p: put(z_ref, p)))
    if not scan_only:
        stages.append((lambda: proj(C_FN, C_DT), lambda p: put(fn_ref, p)))
    stages.append((lambda: proj(C_DT, C_END), lambda p: put(dt_ref, p)))
    pending = None
    for matmuls, epilogue in stages:
        result = matmuls()
        if pending is not None:
            pending[1](pending[0])
        pending = (result, epilogue)
    pending[1](pending[0])


def _inproj(x, mod, pw, l, rowlen, tm, mod_row, scan_only=False):
    bsz, t, _ = x.shape
    whole = tm == t and mod_row is not None and tm & (tm - 1) == 0
    halo = 0 if whole else BF16_ROWS
    nb = bsz if whole else 1
    assert t % tm == 0 and tm % rowlen == 0 and rowlen & (rowlen - 1) == 0 and tm % BF16_ROWS == 0
    per_tile = tm // BF16_ROWS
    last_halo = t // BF16_ROWS - 1
    tok = lambda width: pl.BlockSpec((nb, tm, width), lambda b, i: (b, i, 0))
    lay = lambda *shape: _layer_spec(l, shape)
    out = lambda width, dt: jax.ShapeDtypeStruct((bsz, t, width), dt)
    x_specs, xs = [tok(D_MODEL)], [x]
    if halo:
        x_specs += [pl.BlockSpec((1, halo, D_MODEL), lambda b, i: (b, jnp.maximum(i * per_tile - 1, 0), 0)),
                    pl.BlockSpec((1, halo, D_MODEL), lambda b, i: (b, jnp.minimum((i + 1) * per_tile, last_halo), 0))]
        xs += [x, x]
    widths = [(1024, BF16), (SSD_XBC, BF16), (512, BF16), (LANES, F32)]
    if not scan_only:
        widths += [(1024, BF16), (512, BF16)]
    return pl.pallas_call(
        functools.partial(_inproj_kernel, rowlen=rowlen, mod_row=mod_row, halo=halo, scan_only=scan_only),
        grid=(bsz // nb, t // tm),
        in_specs=x_specs + [lay(1, D_MODEL),
                            pl.BlockSpec((None, SUBLANES, D_MODEL), lambda b, i: (l, 0, 0)),
                            pl.BlockSpec((None, SUBLANES, D_MODEL), lambda b, i: (l, 0, 1)),
                            _layer_spec(l, pw["w_in_a"].shape[1:], single_buffer=True),
                            _layer_spec(l, pw["w_in_b"].shape[1:], single_buffer=True),
                            lay(3, BRANCH_W), lay(4, BRANCH_W), lay(1, BRANCH_W),
                            lay(4, SSD_XBC), lay(1, SSD_XBC)],
        out_specs=[tok(w) for w, _ in widths],
        out_shape=[out(w, dt) for w, dt in widths],
        scratch_shapes=[pltpu.VMEM((nb * tm + 2 * BF16_ROWS, SSD_XBC), F32)],
        compiler_params=_params("arbitrary", "arbitrary"),
        name="inproj",
    )(*xs, pw["norm_w"], mod, mod, pw["w_in_a"], pw["w_in_b"], pw["sc_cw"], pw["rg_cw"], pw["rg_cb"], pw["ssd_cw"],
      pw["ssd_cb"])


def _outproj_kernel(yrg_ref, ysc_ref, yfn_ref, yssd_ref, w_ref, x_ref, g_ref, fw_ref, o_ref, *, final, mod_row):
    row = pl.program_id(0) if mod_row is None else mod_row
    nb, tm, _ = x_ref.shape
    flat = lambda ref: ref[...].reshape(nb * tm, ref.shape[2])
    acc = _dot(flat(yrg_ref), w_ref[0:512, :])
    acc += _dot(flat(ysc_ref), w_ref[512:1024, :])
    acc += _dot(flat(yfn_ref), w_ref[1024:1536, :])
    acc += _dot(flat(yssd_ref), w_ref[1536:2048, :])
    xn = flat(x_ref) + g_ref[pl.ds(row, 1), :] * acc
    if final:
        ms = jnp.mean(xn * xn, axis=-1, keepdims=True)
        xn = (xn * lax.rsqrt(ms + NORM_EPS)) * fw_ref[...]
    o_ref[...] = xn.reshape(nb, tm, D_MODEL)


def _outproj(y_rg, y_sc, y_fn, y_ssd, x, mod, pw, l, final, tm, mod_row):
    bsz, t, _ = x.shape
    nb = bsz if (tm == t and mod_row is not None) else 1
    ytok = pl.BlockSpec((nb, tm, BRANCH_W), lambda b, i: (b, i, 0))
    xtok = pl.BlockSpec((nb, tm, D_MODEL), lambda b, i: (b, i, 0))
    return pl.pallas_call(
        functools.partial(_outproj_kernel, final=final, mod_row=mod_row),
        grid=(bsz // nb, t // tm),
        in_specs=[ytok, ytok, ytok, ytok,
                  _layer_spec(l, (D_INNER, D_MODEL), single_buffer=True),
                  xtok,
                  pl.BlockSpec((None, SUBLANES, D_MODEL), lambda b, i: (l, 0, 2)),
                  pl.BlockSpec((1, D_MODEL), lambda b, i: (0, 0))],
        out_specs=xtok,
        out_shape=jax.ShapeDtypeStruct((bsz, t, D_MODEL), F32),
        compiler_params=_params("arbitrary", "arbitrary"),
        name="outproj",
    )(y_rg, y_sc, y_fn, y_ssd, pw["w_out"], x, mod, pw["final_w"])


MIN_NORMAL_F32 = float(np.finfo(np.float32).tiny)
LOG2E = float(np.log2(np.e))


def _rg_coefficients(uh_b, wg, bg_half, lam):
    uh = uh_b.astype(F32)
    th = jnp.tanh(_dot(uh_b, wg) + bg_half)
    k = (-0.5 * RG_C * np.log2(np.e)) * _softplus(-lam)
    a = jnp.exp2(k + k * th[:, :RG_HALF])
    x = 1.0 - a * a
    root = x * lax.rsqrt(jnp.maximum(x, MIN_NORMAL_F32))
    return a, root * (uh + uh * th[:, RG_HALF:])


def _scan_tile(a, v, h_in, sub, reverse):
    rows = a.shape[0]
    for s in (1, 2, 4):
        if reverse:
            keep = sub < SUBLANES - s
            a_sh = jnp.where(keep, pltpu.roll(a, rows - s, 0), 1.0)
            v_sh = jnp.where(keep, pltpu.roll(v, rows - s, 0), 0.0)
        else:
            keep = sub >= s
            a_sh = jnp.where(keep, pltpu.roll(a, s, 0), 1.0)
            v_sh = jnp.where(keep, pltpu.roll(v, s, 0), 0.0)
        v = v + a * v_sh
        a = a * a_sh
    ngroups = rows // SUBLANES
    out = [None] * ngroups
    h = h_in
    order = range(ngroups - 1, -1, -1) if reverse else range(ngroups)
    for g in order:
        lo = g * SUBLANES
        hg = v[lo:lo + SUBLANES] + a[lo:lo + SUBLANES] * h
        out[g] = hg
        h = hg[0:1] if reverse else hg[SUBLANES - 1:SUBLANES]
    return jnp.concatenate(out, axis=0), h


def _rg_kernel(u_ref, g_ref, wg_ref, bg_ref, lam_ref, h0_ref, y_ref, ht_ref, hb_ref, *, t, tt):
    nt = t // tt
    sub = lax.broadcasted_iota(jnp.int32, (tt, RG_HALF), 0) & (SUBLANES - 1)

    def coeffs(s, d):
        r0 = pl.multiple_of(s * tt, tt)
        return _rg_coefficients(u_ref[0, pl.ds(r0, tt), :], wg_ref[d], bg_ref[d], lam_ref[d:d + 1, :])

    def rev_body(i, h):
        s = nt - 1 - i
        a, v = coeffs(s, 1)
        hh, hn = _scan_tile(a, v, h, sub, True)
        hb_ref[pl.ds(pl.multiple_of(s * tt, tt), tt), :] = hh
        return hn

    h_rev = lax.fori_loop(0, nt, rev_body, h0_ref[0, 1:2, :])

    def fwd_body(s, h):
        a, v = coeffs(s, 0)
        hh, hn = _scan_tile(a, v, h, sub, False)
        r0 = pl.multiple_of(s * tt, tt)
        g = g_ref[0, pl.ds(r0, tt), :].astype(F32)
        y_ref[0, pl.ds(r0, tt), :] = ((hh + hb_ref[pl.ds(r0, tt), :]) * _silu(g)).astype(BF16)
        return hn

    h_fwd = lax.fori_loop(0, nt, fwd_body, h0_ref[0, 0:1, :])
    ht_ref[0] = jnp.concatenate([h_fwd, h_rev, jnp.zeros((SUBLANES - 2, RG_HALF), F32)], axis=0)


RG_GATHER = BF16_ROWS


def _rg_interleaved_kernel(u_ref, g_ref, wg_ref, bg_ref, lam_ref, perm_ref, permt_ref, h0_ref, y_ref,
                           s_ref, af_ref, ab_ref, up_ref, *, t, tt):
    nt = t // tt
    ng = tt // SUBLANES
    chunk = t // SUBLANES
    steps = tt // SUBLANES
    nsub = steps // RG_GATHER
    block = SUBLANES * RG_GATHER
    shape = (SUBLANES, RG_HALF)

    def source_rows(k, m, s):
        return pl.ds(pl.multiple_of(s * chunk + k * steps + m * RG_GATHER, RG_GATHER), RG_GATHER)

    def gather(ref, k):
        blocks = []
        for m in range(nsub):
            rows = jnp.concatenate([ref[0, source_rows(k, m, s), :] for s in range(SUBLANES)], axis=0)
            blocks.append(_dot(perm_ref[...], rows).astype(BF16))
        return jnp.concatenate(blocks, axis=0)

    def sweep(d):
        def body(i, carry):
            h, acc = carry
            s = i if d == 0 else nt - 1 - i
            r0 = pl.multiple_of(s * tt, tt)
            if d == 1:
                ub = gather(u_ref, s)
                up_ref[pl.ds(r0, tt), :] = ub
            else:
                ub = up_ref[pl.ds(r0, tt), :]
            a, v = _rg_coefficients(ub, wg_ref[d], bg_ref[d], lam_ref[d:d + 1, :])
            hs, ps = [None] * ng, [None] * ng
            for g in (range(ng) if d == 0 else range(ng - 1, -1, -1)):
                ag = a[g * SUBLANES:(g + 1) * SUBLANES]
                h = ag * h + v[g * SUBLANES:(g + 1) * SUBLANES]
                acc = ag * acc
                hs[g], ps[g] = h, acc
            local = jnp.concatenate(hs, axis=0)
            if d == 1:
                s_ref[pl.ds(r0, tt), :] = local
                ab_ref[pl.ds(r0, tt), :] = jnp.concatenate(ps, axis=0)
            else:
                s_ref[pl.ds(r0, tt), :] += local
                af_ref[pl.ds(r0, tt), :] = jnp.concatenate(ps, axis=0)
            return h, acc

        return lax.fori_loop(0, nt, body, (jnp.zeros(shape, F32), jnp.ones(shape, F32)), unroll=2)

    hb, pb = sweep(1)
    hf, pf = sweep(0)
    cf = [h0_ref[0, 0:1, :]]
    for s in range(SUBLANES - 1):
        cf.append(hf[s:s + 1] + pf[s:s + 1] * cf[s])
    cb = [None] * SUBLANES
    cb[SUBLANES - 1] = h0_ref[0, 1:2, :]
    for s in range(SUBLANES - 1, 0, -1):
        cb[s - 1] = hb[s:s + 1] + pb[s:s + 1] * cb[s]
    cf = jnp.concatenate(cf, axis=0)[None]
    cb = jnp.concatenate(cb, axis=0)[None]

    def fix(s, carry):
        r0 = pl.multiple_of(s * tt, tt)
        corr = (af_ref[pl.ds(r0, tt), :].reshape(ng, SUBLANES, RG_HALF) * cf
                + ab_ref[pl.ds(r0, tt), :].reshape(ng, SUBLANES, RG_HALF) * cb).reshape(tt, RG_HALF)
        g = gather(g_ref, s).astype(F32)
        y = ((s_ref[pl.ds(r0, tt), :] + corr) * _silu(g)).astype(BF16)
        for m in range(nsub):
            back = _dot(permt_ref[...], y[m * block:(m + 1) * block]).astype(BF16)
            for c in range(SUBLANES):
                y_ref[0, source_rows(s, m, c), :] = back[c * RG_GATHER:(c + 1) * RG_GATHER]
        return carry

    lax.fori_loop(0, nt, fix, 0, unroll=4 if nt % 4 == 0 else 1)


def _rg(rg, pw, l, h0, tt, interleaved):
    bsz, t, _ = rg.shape
    assert t % tt == 0
    nhalf = BRANCH_W // RG_HALF
    seq = lambda col0: pl.BlockSpec((1, t, RG_HALF), lambda b, h: (b, 0, col0 + h))
    in_specs = [seq(0), seq(nhalf),
                pl.BlockSpec((None, 2, None, RG_HALF, 2 * RG_HALF), lambda b, h: (l, 0, h, 0, 0)),
                pl.BlockSpec((None, 2, None, 1, 2 * RG_HALF), lambda b, h: (l, 0, h, 0, 0)),
                pl.BlockSpec((None, 2, RG_HALF), lambda b, h: (l, 0, h)),
                pl.BlockSpec((1, SUBLANES, RG_HALF), lambda b, h: (b, 0, h))]
    args = (rg, rg, pw["rg_wg"], pw["rg_bg"], pw["rg_lam"], h0)
    y_spec = pl.BlockSpec((1, t, RG_HALF), lambda b, h: (b, 0, h))
    y_shape = jax.ShapeDtypeStruct((bsz, t, BRANCH_W), BF16)
    seq_scratch = pltpu.VMEM((t, RG_HALF), F32)
    if interleaved:
        block = SUBLANES * RG_GATHER
        assert tt % block == 0
        pos = np.arange(block)
        perm = np.zeros((block, block))
        perm[pos, RG_GATHER * (pos % SUBLANES) + pos // SUBLANES] = 1.0
        perm = jnp.asarray(perm, dtype=F32).astype(BF16)
        pspec = pl.BlockSpec((block, block), lambda b, h: (0, 0))
        y = pl.pallas_call(
            functools.partial(_rg_interleaved_kernel, t=t, tt=tt),
            grid=(bsz, nhalf), in_specs=in_specs[:5] + [pspec, pspec] + in_specs[5:],
            out_specs=y_spec, out_shape=y_shape,
            scratch_shapes=[seq_scratch, seq_scratch, seq_scratch, pltpu.VMEM((t, RG_HALF), BF16)],
            compiler_params=_params("arbitrary", "arbitrary"), name="rglru_interleaved",
        )(*args[:5], perm, perm.T, args[5])
        return y, None
    return pl.pallas_call(
        functools.partial(_rg_kernel, t=t, tt=tt),
        grid=(bsz, nhalf), in_specs=in_specs,
        out_specs=[y_spec, pl.BlockSpec((1, SUBLANES, RG_HALF), lambda b, h: (b, 0, h))],
        out_shape=[y_shape, jax.ShapeDtypeStruct((bsz, SUBLANES, BRANCH_W), F32)],
        scratch_shapes=[seq_scratch],
        compiler_params=_params("arbitrary", "arbitrary"), name="rglru",
    )(*args)


def _ssd_kernel(xbc_ref, z_ref, dt_ref, dtb_ref, alog_ref, dsk_ref, nw_ref, ecol_ref, ehead_ref,
                hmask_ref, h0_ref, y_ref, ht_ref, yb_ref, csp_ref, adjt_ref, diagt_ref, wgtt_ref, etot_ref,
                loc_ref, st_ref, *, t):
    q = SSD_CHUNK
    nc = t // q
    npairs = SSD_HEADS // 2
    pairs_per_group = npairs // SSD_GROUPS
    lane1 = lax.broadcasted_iota(jnp.int32, (1, LANES), 1)
    a_row = -LOG2E * jnp.exp(alog_ref[...])
    fwd_lane = (lax.broadcasted_iota(jnp.int32, (q, LANES), 1) & (SSD_DT - 1)) < SSD_HEADS
    ri = lax.broadcasted_iota(jnp.int32, (q, q), 0)
    ci = lax.broadcasted_iota(jnp.int32, (q, q), 1)
    lower_b = jnp.where(ri >= ci, 1.0, 0.0).astype(BF16)
    head0_s = lax.broadcasted_iota(jnp.int32, (SSD_STATE, LANES), 1) < SSD_HD

    def pair_vec(row, l0, l1):
        return jnp.where(lane1 < SSD_HD, row[:, l0:l0 + 1], row[:, l1:l1 + 1])

    def small_rows(c):
        return pl.ds(pl.multiple_of(c * SSD_DT, SSD_DT), SSD_DT)

    def pass0(c, carry):
        r0 = pl.multiple_of(c * q, q)
        dtv = _softplus(dt_ref[0, pl.ds(r0, q), :] + dtb_ref[...])
        da = dtv * a_row
        pre = sum(_dot(lower_b, part.astype(BF16)) for part in _split3(da))
        tot = pre[q - 1:q, :]
        cs = jnp.where(fwd_lane, pre, tot - pre + da)
        csp_ref[pl.ds(r0, q), :] = _pack3(cs)
        adjt_ref[small_rows(c), :] = (cs - LOG2E * jnp.log(dtv)).T[:SSD_DT]
        diagt_ref[small_rows(c), :] = (LOG2E * jnp.log(dtv + pltpu.roll(dtv, SSD_HEADS, 1))).T[:SSD_DT]
        wgtt_ref[small_rows(c), :] = (dtv * jnp.exp2(tot - cs)).T[:SSD_DT]
        etot_ref[small_rows(c), :] = jnp.broadcast_to(jnp.exp2(tot), (SSD_DT, LANES))
        return carry

    def pass1(i, carry):
        c = nc - 1 - i
        r0 = pl.multiple_of(c * q, q)
        xs_b = xbc_ref[0, pl.ds(r0, q), 0:BRANCH_W]
        bm_b = xbc_ref[0, pl.ds(r0, q), BRANCH_W:BRANCH_W + LANES]
        cm_b = xbc_ref[0, pl.ds(r0, q), BRANCH_W + LANES:SSD_XBC]
        cs_p = csp_ref[pl.ds(r0, q), :]
        cs_col = _dot(cs_p, ecol_ref[...])
        ecs_b = jnp.exp2(_dot(cs_p, ehead_ref[:, BRANCH_W:]))
        adjt = adjt_ref[small_rows(c), :]
        diagt = diagt_ref[small_rows(c), :]
        wgtt = wgtt_ref[small_rows(c), :]
        etot = etot_ref[pl.ds(pl.multiple_of(c * SSD_DT, SSD_DT), 1), :]
        bt = bm_b.astype(F32).T
        xs_h0 = xs_b * hmask_ref[0:1, :]
        xs_h1 = xs_b * hmask_ref[1:2, :]
        ys = []
        for g in range(SSD_GROUPS):
            gs = slice(g * SSD_STATE, (g + 1) * SSD_STATE)
            cbm = lax.dot_general(cm_b[:, gs], bm_b[:, gs], (((1,), (1,)), ((), ())), preferred_element_type=F32)
            btg = bt[gs, :]
            for pp in range(pairs_per_group):
                pair = g * pairs_per_group + pp
                ps = slice(pair * LANES, (pair + 1) * LANES)
                state = st_ref[pair]
                lf0 = 2 * pair
                lb0 = SSD_HEADS + lf0
                ms = []
                for lf in (lf0, lf0 + 1):
                    lb = SSD_HEADS + lf
                    arg = jnp.where(ri > ci, cs_col[:, lf * LANES:(lf + 1) * LANES] - adjt[lf:lf + 1, :],
                                    jnp.where(ri < ci, cs_col[:, lb * LANES:(lb + 1) * LANES] - adjt[lb:lb + 1, :],
                                              diagt[lf:lf + 1, :]))
                    ms.append((cbm * jnp.exp2(arg)).astype(BF16))
                yd = _dot(jnp.concatenate(ms, axis=1), jnp.concatenate([xs_h0[:, ps], xs_h1[:, ps]], axis=0))
                yo = _dot(cm_b[:, gs], state.astype(BF16)) * ecs_b[:, ps]
                ys.append(yd + yo)
                stack = jnp.concatenate([(btg * wgtt[l:l + 1, :]).astype(BF16) for l in (lf0, lf0 + 1, lb0, lb0 + 1)],
                                        axis=0)
                res = _dot(stack, xs_b[:, ps])
                n = SSD_STATE
                loc_ref[c, pair] = jnp.where(head0_s, res[0:n], res[n:2 * n])
                st_ref[pair] = state * pair_vec(etot, lb0, lb0 + 1) + jnp.where(head0_s, res[2 * n:3 * n], res[3 * n:])
        yb_ref[pl.ds(r0, q), :] = jnp.concatenate(ys, axis=1) + dsk_ref[...] * xs_b.astype(F32)
        return carry

    def pass2(c, carry):
        r0 = pl.multiple_of(c * q, q)
        cm_b = xbc_ref[0, pl.ds(r0, q), BRANCH_W + LANES:SSD_XBC]
        ecs_f = jnp.exp2(_dot(csp_ref[pl.ds(r0, q), :], ehead_ref[:, :BRANCH_W]))
        etot = etot_ref[pl.ds(pl.multiple_of(c * SSD_DT, SSD_DT), 1), :]
        ys = []
        for pair in range(npairs):
            g = pair // pairs_per_group
            state = st_ref[pair]
            lf0 = 2 * pair
            ys.append(_dot(cm_b[:, g * SSD_STATE:(g + 1) * SSD_STATE], state.astype(BF16))
                      * ecs_f[:, pair * LANES:(pair + 1) * LANES])
            st_ref[pair] = state * pair_vec(etot, lf0, lf0 + 1) + loc_ref[c, pair]
        yt = yb_ref[pl.ds(r0, q), :] + jnp.concatenate(ys, axis=1)
        yt = yt * _silu(z_ref[0, pl.ds(r0, q), :].astype(F32))
        ms = jnp.mean(yt * yt, axis=-1, keepdims=True)
        y_ref[0, pl.ds(r0, q), :] = ((yt * lax.rsqrt(ms + NORM_EPS)) * nw_ref[...]).astype(BF16)
        return carry

    lax.fori_loop(0, nc, pass0, 0, unroll=8 if nc % 8 == 0 else 1)
    for p in range(npairs):
        st_ref[p] = h0_ref[0, 1, p]
    lax.fori_loop(0, nc, pass1, 0, unroll=4 if nc % 4 == 0 else 2)
    for p in range(npairs):
        ht_ref[0, 1, p] = st_ref[p]
        st_ref[p] = h0_ref[0, 0, p]
    lax.fori_loop(0, nc, pass2, 0, unroll=8 if nc % 8 == 0 else 2)
    for p in range(npairs):
        ht_ref[0, 0, p] = st_ref[p]


@functools.lru_cache(maxsize=None)
def _ssd_tables():
    lanes = np.arange(LANES)[:, None]
    src = np.where(lanes < 3 * SSD_DT, lanes % SSD_DT, -1)
    ecol = (src == np.arange(SSD_DT * LANES)[None, :] // LANES)
    ehead = (src == np.arange(2 * BRANCH_W)[None, :] // SSD_HD)
    even_head = (np.arange(BRANCH_W) // SSD_HD) % 2 == 0
    hmask = np.zeros((BF16_ROWS, BRANCH_W))
    hmask[0] = even_head
    hmask[1] = ~even_head
    as_bf16 = lambda a: jnp.asarray(a, dtype=F32).astype(BF16)
    return as_bf16(ecol), as_bf16(ehead), as_bf16(hmask)


def _ssd(xbc, z, dt, pw, l, h0):
    bsz, t, _ = xbc.shape
    assert t % SSD_CHUNK == 0
    npairs = SSD_HEADS // 2
    ecol, ehead, hmask = _ssd_tables()
    full = lambda shape: pl.BlockSpec(shape, lambda b: (0,) * len(shape))
    lay = lambda *shape: _layer_spec(l, shape)
    seq = lambda width: pl.BlockSpec((1, t, width), lambda b: (b, 0, 0))
    st_spec = pl.BlockSpec((1, 2, npairs, SSD_STATE, LANES), lambda b: (b, 0, 0, 0, 0))
    return pl.pallas_call(
        functools.partial(_ssd_kernel, t=t),
        grid=(bsz,),
        in_specs=[seq(SSD_XBC), seq(BRANCH_W), seq(LANES), lay(1, LANES), lay(1, LANES),
                  lay(1, BRANCH_W), lay(1, BRANCH_W), full(ecol.shape), full(ehead.shape), full(hmask.shape),
                  st_spec],
        out_specs=[seq(BRANCH_W), st_spec],
        out_shape=[jax.ShapeDtypeStruct((bsz, t, BRANCH_W), BF16),
                   jax.ShapeDtypeStruct((bsz, 2, npairs, SSD_STATE, LANES), F32)],
        scratch_shapes=[pltpu.VMEM((t, BRANCH_W), F32), pltpu.VMEM((t, LANES), BF16)]
        + [pltpu.VMEM((t // SSD_CHUNK * SSD_DT, LANES), F32)] * 4
        + [pltpu.VMEM((t // SSD_CHUNK, npairs, SSD_STATE, LANES), F32),
                        pltpu.VMEM((npairs, SSD_STATE, LANES), F32)],
        compiler_params=_params("arbitrary"),
        name="ssd",
    )(xbc, z, dt, pw["ssd_dtb"], pw["ssd_alog"], pw["ssd_dskip"], pw["ssd_nw"], ecol, ehead, hmask, h0)


@functools.lru_cache(maxsize=None)
def _fourier_tables(n, blk):
    half = n // 2
    nh = half // blk
    assert nh * blk == half and nh <= SUBLANES
    p = np.arange(FN_GD)
    ang_c = 2.0 * np.pi * ((p[:, None] * p[None, :]) % FN_GD) / FN_GD
    eye = np.eye(BRANCH_W // FN_GD)
    cc = np.kron(eye, np.cos(ang_c))
    sc = np.kron(eye, np.sin(ang_c))
    k = np.arange(half)
    ang_t = 2.0 * np.pi * ((k[:, None] * k[None, :]) % n) / n
    ct = np.cos(ang_t)
    stn = -np.sin(ang_t)
    assert blk % FN_FLIP == 0
    jp = np.zeros((FN_FLIP, FN_FLIP))
    r = np.arange(1, FN_FLIP)
    jp[r, FN_FLIP - r] = 1.0
    tt = np.arange(n)
    ks = blk * (np.arange(nh) + 1)
    ang_k = 2.0 * np.pi * ((ks[:, None] * tt[None, :]) % n) / n
    tc = np.zeros((2 * SUBLANES, n))
    ts = np.zeros((2 * SUBLANES, n))
    tc[:nh] = np.cos(ang_k)
    ts[:nh] = np.sin(ang_k)
    tc[SUBLANES, half] = 1.0
    as_bf16 = lambda a: jnp.asarray(a, dtype=F32).astype(BF16)
    return dict(cc=as_bf16(cc), sc=as_bf16(sc), ct=as_bf16(ct), stn=as_bf16(stn), jp=as_bf16(jp),
                tc=as_bf16(tc), ts=as_bf16(ts))


def _mirror(jp_ref, src, row0):
    n = src.shape[0]
    nsub = n // FN_FLIP
    first = lax.broadcasted_iota(jnp.int32, (FN_FLIP, src.shape[1]), 0) == 0
    out = []
    for a in range(nsub):
        s = nsub - 1 - a
        head = row0 if a == 0 else src[(s + 1) * FN_FLIP:(s + 1) * FN_FLIP + 1].astype(F32)
        out.append(jnp.where(first, head, _dot(jp_ref[...], src[s * FN_FLIP:(s + 1) * FN_FLIP])))
    return jnp.concatenate(out, axis=0)


def _fn_fold_kernel(xj_ref, xm_ref, xr_ref, jp_ref, cc_ref, sc_ref, eo_ref):
    j = pl.program_id(1)
    xj = xj_ref[0]
    row0 = xr_ref[0, 0:1, :].astype(F32) * jnp.where(j == 0, 0.0, 1.0)
    mir = _mirror(jp_ref, xm_ref[0], row0).astype(BF16)
    e = _dot(xj, cc_ref[...]) + _dot(mir, cc_ref[...])
    o = _dot(xj, sc_ref[...]) - _dot(mir, sc_ref[...])
    eo_ref[0] = jnp.concatenate([e, o], axis=1).astype(BF16)


def _fn_aux_kernel(x_ref, tc_ref, ts_ref, cc_ref, sc_ref, o_ref):
    x = x_ref[0]
    xc = _dot(tc_ref[...], x).astype(BF16)
    xs = _dot(ts_ref[...], x).astype(BF16)
    o_ref[0] = _dot(xc, cc_ref[...]) + _dot(xs, sc_ref[...])


def _fn_main_kernel(ct_ref, stn_ref, eo_ref, aux_ref, g_ref, jp_ref, y_ref, *, scale):
    m = pl.program_id(1)
    blk = ct_ref.shape[0]
    nblk = y_ref.shape[1] // blk
    rows = lax.broadcasted_iota(jnp.int32, (blk, BRANCH_W), 0)
    p = _dot(ct_ref[...], eo_ref[0, :, 0:BRANCH_W])
    qn = _dot(stn_ref[...], eo_ref[0, :, BRANCH_W:2 * BRANCH_W])
    sgn = (1 - 2 * ((m * blk + rows) & 1)).astype(F32)
    p = p + sgn * aux_ref[0, SUBLANES:SUBLANES + 1, :]
    lo = pl.multiple_of(m * blk, blk)
    hi = pl.multiple_of((nblk - 1 - m) * blk, blk)
    y_ref[0, pl.ds(lo, blk), :] = ((p + qn) * (_silu(g_ref[0, pl.ds(lo, blk), :].astype(F32)) * scale)).astype(BF16)
    sel = lax.broadcasted_iota(jnp.int32, (2 * SUBLANES, BRANCH_W), 0) == m
    row0 = jnp.sum(jnp.where(sel, aux_ref[0], 0.0), axis=0, keepdims=True)
    y_ref[0, pl.ds(hi, blk), :] = (_mirror(jp_ref, (p - qn).astype(BF16), row0)
                                   * (_silu(g_ref[0, pl.ds(hi, blk), :].astype(F32)) * scale)).astype(BF16)


def _fourier(fn, blk):
    bsz, n, _ = fn.shape
    tb = _fourier_tables(n, blk)
    half = n // 2
    nh = half // blk
    nblk = 2 * nh
    rows16 = n // BF16_ROWS
    full2 = lambda shape: pl.BlockSpec(shape, lambda b, j: (0,) * len(shape))
    eo = pl.pallas_call(
        _fn_fold_kernel,
        grid=(bsz, nh),
        in_specs=[pl.BlockSpec((1, blk, BRANCH_W), lambda b, j: (b, j, 0)),
                  pl.BlockSpec((1, blk, BRANCH_W), lambda b, j: (b, nblk - 1 - j, 0)),
                  pl.BlockSpec((1, BF16_ROWS, BRANCH_W),
                               lambda b, j: (b, jnp.minimum((blk // BF16_ROWS) * (nblk - j), rows16 - 1), 0)),
                  full2((FN_FLIP, FN_FLIP)), full2((BRANCH_W, BRANCH_W)), full2((BRANCH_W, BRANCH_W))],
        out_specs=pl.BlockSpec((1, blk, 2 * BRANCH_W), lambda b, j: (b, j, 0)),
        out_shape=jax.ShapeDtypeStruct((bsz, half, 2 * BRANCH_W), BF16),
        compiler_params=_params("arbitrary", "arbitrary"),
        name="fn_fold",
    )(fn, fn, fn, tb["jp"], tb["cc"], tb["sc"])
    full1 = lambda shape: pl.BlockSpec(shape, lambda b: (0,) * len(shape))
    aux = pl.pallas_call(
        _fn_aux_kernel,
        grid=(bsz,),
        in_specs=[pl.BlockSpec((1, n, BRANCH_W), lambda b: (b, 0, 0)),
                  full1((2 * SUBLANES, n)), full1((2 * SUBLANES, n)),
                  full1((BRANCH_W, BRANCH_W)), full1((BRANCH_W, BRANCH_W))],
        out_specs=pl.BlockSpec((1, 2 * SUBLANES, BRANCH_W), lambda b: (b, 0, 0)),
        out_shape=jax.ShapeDtypeStruct((bsz, 2 * SUBLANES, BRANCH_W), F32),
        compiler_params=_params("arbitrary"),
        name="fn_aux",
    )(fn, tb["tc"], tb["ts"], tb["cc"], tb["sc"])
    return pl.pallas_call(
        functools.partial(_fn_main_kernel, scale=float(1.0 / np.sqrt(n * FN_GD))),
        grid=(bsz, nh),
        in_specs=[pl.BlockSpec((blk, half), lambda b, m: (m, 0)),
                  pl.BlockSpec((blk, half), lambda b, m: (m, 0)),
                  pl.BlockSpec((1, half, 2 * BRANCH_W), lambda b, m: (b, 0, 0)),
                  pl.BlockSpec((1, 2 * SUBLANES, BRANCH_W), lambda b, m: (b, 0, 0)),
                  pl.BlockSpec((1, n, BRANCH_W), lambda b, m: (b, 0, 1)),
                  full2((FN_FLIP, FN_FLIP))],
        out_specs=pl.BlockSpec((1, n, BRANCH_W), lambda b, m: (b, 0, 0)),
        out_shape=jax.ShapeDtypeStruct((bsz, n, BRANCH_W), BF16),
        compiler_params=_params("arbitrary", "arbitrary"),
        name="fn_main",
    )(tb["ct"], tb["stn"], eo, aux, fn, tb["jp"])


def _prepare_weights(norm_w, w_in, w_out, rg_conv_w, rg_conv_b, rg_gate_a_w, rg_gate_a_b, rg_gate_x_w, rg_gate_x_b,
                     rg_lambda, sc_conv_w, ssd_conv_w, ssd_conv_b, ssd_dt_bias, ssd_a_log, ssd_d, ssd_norm_w,
                     final_norm_w):
    depth = w_in.shape[0]
    assert REF_DT_HI - REF_DT_LO == SSD_DT and w_in.shape[2] - SSD_DT + LANES == C_END
    w_a = jnp.concatenate([w_in[:, :, :REF_DT_LO], jnp.tile(w_in[:, :, REF_DT_LO:REF_DT_HI], (1, 1, LANES // SSD_DT))],
                          axis=2).astype(BF16)
    w_b = w_in[:, :, REF_DT_HI:].astype(BF16)
    w_a, w_b = lax.optimization_barrier((w_a, w_b))
    heads_per_half = RG_HALF // RG_HD
    eye = jnp.eye(heads_per_half, dtype=F32)

    def blockdiag(wg):
        wg = wg.reshape(depth, 2, -1, heads_per_half, RG_HD, RG_HD)
        return jnp.einsum("ldhjio,jk->ldhjiko", wg, eye).reshape(depth, 2, -1, RG_HALF, RG_HALF)

    halves = lambda b: b.reshape(depth, 2, -1, 1, RG_HALF)
    rep = LANES // SSD_DT
    return dict(
        norm_w=norm_w.reshape(depth, 1, D_MODEL), final_w=final_norm_w.reshape(1, D_MODEL),
        w_in_a=w_a, w_in_b=w_b, w_out=w_out.astype(BF16), sc_cw=sc_conv_w,
        rg_cw=0.5 * rg_conv_w, rg_cb=0.5 * rg_conv_b.reshape(depth, 1, BRANCH_W),
        rg_wg=jnp.concatenate([blockdiag(rg_gate_a_w), blockdiag(rg_gate_x_w)], axis=-1).astype(BF16),
        rg_bg=0.5 * jnp.concatenate([halves(rg_gate_a_b), halves(rg_gate_x_b)], axis=-1),
        rg_lam=rg_lambda,
        ssd_cw=ssd_conv_w, ssd_cb=ssd_conv_b.reshape(depth, 1, SSD_XBC),
        ssd_dtb=jnp.tile(ssd_dt_bias.reshape(depth, 1, SSD_DT), (1, 1, rep)),
        ssd_alog=jnp.tile(ssd_a_log.reshape(depth, 1, SSD_DT), (1, 1, rep)),
        ssd_dskip=jnp.repeat(ssd_d, SSD_HD, axis=1).reshape(depth, 1, BRANCH_W),
        ssd_nw=ssd_norm_w.reshape(depth, 1, BRANCH_W),
    )


def _mix(x, mod, pw, l, mod_row, rowlen, tm, tt, blk, interleave, rg_h0, ssd_h0, with_output, final):
    rg, xbc, z, dt, *rest = _inproj(x, mod, pw, l, rowlen, tm, mod_row, scan_only=not with_output)
    y_rg, rg_st = _rg(rg, pw, l, rg_h0, tt, interleave)
    y_ssd, ssd_st = _ssd(xbc, z, dt, pw, l, ssd_h0)
    if not with_output:
        return None, rg_st, ssd_st
    fn, y_sc = rest
    y_fn = _fourier(fn, blk)
    return _outproj(y_rg, y_sc, y_fn, y_ssd, x, mod, pw, l, final, tm, mod_row), rg_st, ssd_st


def kernel(x, c, ctx, c_ctx, ada_w, ada_b, norm_w, w_in, w_out, rg_conv_w, rg_conv_b, rg_gate_a_w, rg_gate_a_b,
           rg_gate_x_w, rg_gate_x_b, rg_lambda, sc_conv_w, ssd_conv_w, ssd_conv_b, ssd_dt_bias, ssd_a_log, ssd_d,
           ssd_norm_w, final_norm_w):
    bsz, seq, _ = x.shape
    ctx_len = ctx.shape[1]
    depth = w_in.shape[0]
    assert bsz + 1 <= SUBLANES and seq % SUBLANES == 0
    pw = _prepare_weights(norm_w, w_in, w_out, rg_conv_w, rg_conv_b, rg_gate_a_w, rg_gate_a_b, rg_gate_x_w,
                          rg_gate_x_b, rg_lambda, sc_conv_w, ssd_conv_w, ssd_conv_b, ssd_dt_bias, ssd_a_log, ssd_d,
                          ssd_norm_w, final_norm_w)
    c8 = jnp.concatenate([c, c_ctx[None, :], jnp.zeros((SUBLANES - bsz - 1, D_MODEL), F32)], axis=0)
    mod = _ada(c8, ada_w, ada_b.reshape(depth, 1, 3 * D_MODEL))
    zeros_rg = jnp.zeros((bsz, SUBLANES, BRANCH_W), F32)
    zeros_ssd = jnp.zeros((bsz, 2, SSD_HEADS // 2, SSD_STATE, LANES), F32)
    tm_lat = min(1024, seq)
    tt_lat = min(1024, seq)
    tt_ctx = min(256, ctx_len)
    blk_lat = min(512, seq // 2)
    blk_ctx = min(512, ctx_len // 2)
    for l in range(depth):
        last = l == depth - 1
        new_ctx, rg_st, ssd_st = _mix(ctx, mod, pw, l, bsz, ctx_len, ctx_len, tt_ctx, blk_ctx, False,
                                      zeros_rg, zeros_ssd, not last, False)
        x, _, _ = _mix(x, mod, pw, l, None, GRID_W, tm_lat, tt_lat, blk_lat, True, rg_st, ssd_st, True, last)
        if not last:
            ctx = new_ctx
    return x
```

```python
import functools

import numpy as np
import jax
import jax.numpy as jnp
from jax import lax
from jax.experimental import pallas as pl
from jax.experimental.pallas import tpu as pltpu

F32 = jnp.float32
BF16 = jnp.bfloat16

D_MODEL = 1024
D_INNER = 2048
BRANCH_W = 512
GRID_W = 64
RG_HD = 64
RG_C = 8.0
RG_HALF = 256
SSD_HEADS = 8
SSD_HD = 64
SSD_GROUPS = 2
SSD_STATE = 64
SSD_XBC = BRANCH_W + 2 * SSD_GROUPS * SSD_STATE
SSD_CHUNK = 128
SSD_DT = 2 * SSD_HEADS
FN_GD = 128
FN_FLIP = 128
NORM_EPS = 1e-6

LANES = 128
SUBLANES = 8
BF16_ROWS = 16
VMEM_LIMIT_BYTES = 60000 * 1024

C_RGX, C_XBC, C_RGG, C_Z, C_SC, C_FN, C_DT, C_END = 0, 512, 1280, 1792, 2304, 4352, 5376, 5504
REF_DT_LO, REF_DT_HI = 1280, 1296


def _silu(v):
    h = 0.5 * v
    return h + h * jnp.tanh(h)


def _softplus(v):
    return jnp.maximum(v, 0.0) + jnp.log1p(jnp.exp(-jnp.abs(v)))


def _dot(a, b):
    return jnp.dot(a, b, preferred_element_type=F32)


def _split3(v):
    hi = v.astype(BF16).astype(F32)
    rest = v - hi
    mid = rest.astype(BF16).astype(F32)
    return hi, mid, rest - mid


def _pack3(v):
    hi, mid, lo = _split3(v)
    period = lax.broadcasted_iota(jnp.int32, v.shape, 1) // SSD_DT
    return jnp.where(period == 0, hi, jnp.where(period == 1, mid, jnp.where(period == 2, lo, 0.0))).astype(BF16)


def _params(*semantics):
    return pltpu.CompilerParams(dimension_semantics=semantics, vmem_limit_bytes=VMEM_LIMIT_BYTES)


def _layer_spec(l, shape, single_buffer=False):
    mode = dict(pipeline_mode=pl.Buffered(1)) if single_buffer else {}
    return pl.BlockSpec((None,) + tuple(shape), lambda *_: (l,) + (0,) * len(shape), **mode)


def _ada_kernel(c_ref, w_ref, b_ref, o_ref):
    s = _silu(c_ref[...])
    hi = s.astype(BF16)
    lo = (s - hi.astype(F32)).astype(BF16)
    w = w_ref[...].astype(BF16)
    o_ref[...] = _dot(hi, w) + _dot(lo, w) + b_ref[...]


def _ada(c8, w, b):
    depth = w.shape[0]
    tn = 1024
    return pl.pallas_call(
        _ada_kernel,
        grid=(depth, 3 * D_MODEL // tn),
        in_specs=[pl.BlockSpec((SUBLANES, D_MODEL), lambda l, j: (0, 0)),
                  pl.BlockSpec((None, D_MODEL, tn), lambda l, j: (l, 0, j)),
                  pl.BlockSpec((None, 1, tn), lambda l, j: (l, 0, j))],
        out_specs=pl.BlockSpec((None, SUBLANES, tn), lambda l, j: (l, 0, j)),
        out_shape=jax.ShapeDtypeStruct((depth, SUBLANES, 3 * D_MODEL), F32),
        compiler_params=_params("arbitrary", "arbitrary"),
        name="ada",
    )(c8, w, b)


def _inproj_kernel(*refs, rowlen, mod_row, halo, scan_only):
    if halo:
        x_ref, xp_ref, xn_ref, *refs = refs
    else:
        x_ref, *refs = refs
    nw_ref, sh_ref, sc_ref, wa_ref, wb_ref, rep_ref, cw_ref, rcw_ref, rcb_ref, scw_ref, scb_ref, *outs, tap_ref = refs
    nb, tm, _ = x_ref.shape
    rows = nb * tm
    i = pl.program_id(1)
    row = pl.program_id(0) if mod_row is None else mod_row
    if halo:
        x = jnp.concatenate([xp_ref[0], x_ref[0], xn_ref[0]], axis=0)
    else:
        x = x_ref[...].reshape(rows, D_MODEL)
    ms = jnp.mean(x * x, axis=-1, keepdims=True)
    gain = nw_ref[...] * (1.0 + sc_ref[pl.ds(row, 1), :])
    hx = ((x * lax.rsqrt(ms + NORM_EPS)) * gain + sh_ref[pl.ds(row, 1), :]).astype(BF16)
    hb = hx[halo:halo + rows]

    def wcols(lo, hi):
        if hi <= REF_DT_LO:
            return wa_ref[:, lo:hi]
        if lo >= C_DT:
            assert (lo, hi) == (C_DT, C_END)
            return _dot(wa_ref[:, REF_DT_LO:REF_DT_LO + LANES], rep_ref[...]).astype(BF16)
        assert lo >= REF_DT_LO and hi <= C_DT
        return wb_ref[:, lo - REF_DT_LO:hi - REF_DT_LO]

    def proj(lo, hi):
        return _dot(hb, wcols(lo, hi))

    def proj_halo(lo, hi):
        return _dot(hx, wcols(lo, hi))

    def conv4(p, cw, cb):
        n = p.shape[0]
        if halo:
            keep_prev = jnp.where(i == 0, 0.0, 1.0)
            keep_next = jnp.where(i == pl.num_programs(1) - 1, 0.0, 1.0)
            w = p.shape[1]
            tap_ref[0:halo, 0:w] = p[:halo] * keep_prev
            tap_ref[halo:halo + rows, 0:w] = p[halo:halo + rows]
            tap_ref[halo + rows:n, 0:w] = p[halo + rows:] * keep_next
            taps = [tap_ref[halo + off:halo + off + rows, 0:w] for off in (-2, -1)]
            taps += [p[halo:halo + rows], tap_ref[halo + 1:halo + 1 + rows, 0:w]]
        else:
            pos = lax.broadcasted_iota(jnp.int32, p.shape, 0) & (tm - 1)
            taps = [jnp.where(pos >= 2, pltpu.roll(p, 2, 0), 0.0), jnp.where(pos >= 1, pltpu.roll(p, 1, 0), 0.0), p,
                    jnp.where(pos <= tm - 2, pltpu.roll(p, n - 1, 0), 0.0)]
        return sum(cw[k:k + 1, :] * taps[k] for k in range(4)) + cb

    def put(ref, value, lo=0):
        ref[:, :, lo:lo + value.shape[1]] = value.reshape(nb, tm, value.shape[1]).astype(ref.dtype)

    stages = []
    if scan_only:
        rg_ref, xbc_ref, z_ref, dt_ref = outs
    else:
        rg_ref, xbc_ref, z_ref, dt_ref, fn_ref, ysc_ref = outs
        conv3 = []

        def sc_conv(ps):
            v = ps[0] * ps[1]
            pos = lax.broadcasted_iota(jnp.int32, (rows, BRANCH_W), 0) & (rowlen - 1)
            pad = SUBLANES
            edge = jnp.zeros((pad, BRANCH_W), F32)
            tap_ref[0:pad, 0:BRANCH_W] = edge
            tap_ref[pad:pad + rows, 0:BRANCH_W] = v
            tap_ref[pad + rows:2 * pad + rows, 0:BRANCH_W] = edge
            vm1 = jnp.where(pos == 0, 0.0, tap_ref[pad - 1:pad - 1 + rows, 0:BRANCH_W])
            vp1 = jnp.where(pos == rowlen - 1, 0.0, tap_ref[pad + 1:pad + 1 + rows, 0:BRANCH_W])
            conv3.append(cw_ref[0:1, :] * vm1 + cw_ref[1:2, :] * v + cw_ref[2:3, :] * vp1)

        stages.append((lambda: (proj(C_SC + 512, C_SC + 1024), proj(C_SC + 1024, C_SC + 1536)), sc_conv))
        stages.append((lambda: (proj(C_SC, C_SC + 512), proj(C_SC + 1536, C_FN)),
                       lambda ps: put(ysc_ref, ps[0] * conv3[0] * _silu(ps[1]))))
    stages.append((lambda: proj_halo(C_XBC, C_RGG),
                   lambda p: put(xbc_ref, _silu(conv4(p, scw_ref[...], scb_ref[...])))))
    stages.append((lambda: proj_halo(C_RGX, C_XBC), lambda p: put(rg_ref, conv4(p, rcw_ref[...], rcb_ref[...]))))
    stages.append((lambda: proj(C_RGG, C_Z), lambda p: put(rg_ref, p, BRANCH_W)))
    stages.append((lambda: proj(C_Z, C_SC), lambda p: put(z_ref, p)))
    if not scan_only:
        stages.append((lambda: proj(C_FN, C_DT), lambda p: put(fn_ref, p)))
    stages.append((lambda: proj(C_DT, C_END), lambda p: put(dt_ref, p)))
    pending = None
    for matmuls, epilogue in stages:
        result = matmuls()
        if pending is not None:
            pending[1](pending[0])
        pending = (result, epilogue)
    pending[1](pending[0])


def _inproj(x, mod, pw, l, rowlen, tm, mod_row, scan_only=False):
    bsz, t, _ = x.shape
    whole = tm == t and mod_row is not None and tm & (tm - 1) == 0
    halo = 0 if whole else BF16_ROWS
    nb = bsz if whole else 1
    assert t % tm == 0 and tm % rowlen == 0 and rowlen & (rowlen - 1) == 0 and tm % BF16_ROWS == 0
    per_tile = tm // BF16_ROWS
    last_halo = t // BF16_ROWS - 1
    tok = lambda width: pl.BlockSpec((nb, tm, width), lambda b, i: (b, i, 0))
    lay = lambda *shape: _layer_spec(l, shape)
    out = lambda width, dt: jax.ShapeDtypeStruct((bsz, t, width), dt)
    x_specs, xs = [tok(D_MODEL)], [x]
    if halo:
        x_specs += [pl.BlockSpec((1, halo, D_MODEL), lambda b, i: (b, jnp.maximum(i * per_tile - 1, 0), 0)),
                    pl.BlockSpec((1, halo, D_MODEL), lambda b, i: (b, jnp.minimum((i + 1) * per_tile, last_halo), 0))]
        xs += [x, x]
    widths = [(1024, BF16), (SSD_XBC, BF16), (512, BF16), (LANES, F32)]
    if not scan_only:
        widths += [(1024, BF16), (512, BF16)]
    return pl.pallas_call(
        functools.partial(_inproj_kernel, rowlen=rowlen, mod_row=mod_row, halo=halo, scan_only=scan_only),
        grid=(bsz // nb, t // tm),
        in_specs=x_specs + [lay(1, D_MODEL),
                            pl.BlockSpec((None, SUBLANES, D_MODEL), lambda b, i: (l, 0, 0)),
                            pl.BlockSpec((None, SUBLANES, D_MODEL), lambda b, i: (l, 0, 1)),
                            _layer_spec(l, pw["w_in_a"].shape[1:], single_buffer=True),
                            _layer_spec(l, pw["w_in_b"].shape[1:], single_buffer=True),
                            pl.BlockSpec((LANES, LANES), lambda b, i: (0, 0)),
                            lay(3, BRANCH_W), lay(4, BRANCH_W), lay(1, BRANCH_W),
                            lay(4, SSD_XBC), lay(1, SSD_XBC)],
        out_specs=[tok(w) for w, _ in widths],
        out_shape=[out(w, dt) for w, dt in widths],
        scratch_shapes=[pltpu.VMEM((nb * tm + 2 * BF16_ROWS, SSD_XBC), F32)],
        compiler_params=_params("arbitrary", "arbitrary"),
        name="inproj",
    )(*xs, pw["norm_w"], mod, mod, pw["w_in_a"], pw["w_in_b"], pw["dt_rep"], pw["sc_cw"], pw["rg_cw"], pw["rg_cb"],
      pw["ssd_cw"], pw["ssd_cb"])


def _outproj_kernel(yrg_ref, ysc_ref, yfn_ref, yssd_ref, w_ref, x_ref, g_ref, fw_ref, o_ref, *, final, mod_row):
    row = pl.program_id(0) if mod_row is None else mod_row
    nb, tm, _ = x_ref.shape
    flat = lambda ref: ref[...].reshape(nb * tm, ref.shape[2])
    acc = _dot(flat(yrg_ref), w_ref[0:512, :])
    acc += _dot(flat(ysc_ref), w_ref[512:1024, :])
    acc += _dot(flat(yfn_ref), w_ref[1024:1536, :])
    acc += _dot(flat(yssd_ref), w_ref[1536:2048, :])
    xn = flat(x_ref) + g_ref[pl.ds(row, 1), :] * acc
    if final:
        ms = jnp.mean(xn * xn, axis=-1, keepdims=True)
        xn = (xn * lax.rsqrt(ms + NORM_EPS)) * fw_ref[...]
    o_ref[...] = xn.reshape(nb, tm, D_MODEL)


def _outproj(y_rg, y_sc, y_fn, y_ssd, x, mod, pw, l, final, tm, mod_row):
    bsz, t, _ = x.shape
    nb = bsz if (tm == t and mod_row is not None) else 1
    ytok = pl.BlockSpec((nb, tm, BRANCH_W), lambda b, i: (b, i, 0))
    xtok = pl.BlockSpec((nb, tm, D_MODEL), lambda b, i: (b, i, 0))
    return pl.pallas_call(
        functools.partial(_outproj_kernel, final=final, mod_row=mod_row),
        grid=(bsz // nb, t // tm),
        in_specs=[ytok, ytok, ytok, ytok,
                  _layer_spec(l, (D_INNER, D_MODEL), single_buffer=True),
                  xtok,
                  pl.BlockSpec((None, SUBLANES, D_MODEL), lambda b, i: (l, 0, 2)),
                  pl.BlockSpec((1, D_MODEL), lambda b, i: (0, 0))],
        out_specs=xtok,
        out_shape=jax.ShapeDtypeStruct((bsz, t, D_MODEL), F32),
        compiler_params=_params("arbitrary", "arbitrary"),
        name="outproj",
    )(y_rg, y_sc, y_fn, y_ssd, pw["w_out"], x, mod, pw["final_w"])


MIN_NORMAL_F32 = float(np.finfo(np.float32).tiny)
LOG2E = float(np.log2(np.e))


def _rg_coefficients(uh_b, wg, bg_half, lam):
    uh = uh_b.astype(F32)
    th = jnp.tanh(_dot(uh_b, wg) + bg_half)
    k = (-0.5 * RG_C * np.log2(np.e)) * _softplus(-lam)
    a = jnp.exp2(k + k * th[:, :RG_HALF])
    x = 1.0 - a * a
    root = x * lax.rsqrt(jnp.maximum(x, MIN_NORMAL_F32))
    return a, root * (uh + uh * th[:, RG_HALF:])


RG_GATHER = BF16_ROWS


def _rg_kernel(u_ref, g_ref, wg_ref, bg_ref, lam_ref, perm_ref, permt_ref, h0_ref, y_ref, ht_ref,
               s_ref, af_ref, ab_ref, up_ref, *, t, tt):
    nt = t // tt
    ng = tt // SUBLANES
    chunk = t // SUBLANES
    steps = tt // SUBLANES
    nsub = steps // RG_GATHER
    block = SUBLANES * RG_GATHER
    shape = (SUBLANES, RG_HALF)

    def source_rows(k, m, s):
        return pl.ds(pl.multiple_of(s * chunk + k * steps + m * RG_GATHER, RG_GATHER), RG_GATHER)

    def gather(ref, k):
        blocks = []
        for m in range(nsub):
            rows = jnp.concatenate([ref[0, source_rows(k, m, s), :] for s in range(SUBLANES)], axis=0)
            blocks.append(_dot(perm_ref[...], rows).astype(BF16))
        return jnp.concatenate(blocks, axis=0)

    def sweep(d):
        def body(i, carry):
            h, acc = carry
            s = i if d == 0 else nt - 1 - i
            r0 = pl.multiple_of(s * tt, tt)
            if d == 1:
                ub = gather(u_ref, s)
                up_ref[pl.ds(r0, tt), :] = ub
            else:
                ub = up_ref[pl.ds(r0, tt), :]
            a, v = _rg_coefficients(ub, wg_ref[d], bg_ref[d], lam_ref[d:d + 1, :])
            hs, ps = [None] * ng, [None] * ng
            for g in (range(ng) if d == 0 else range(ng - 1, -1, -1)):
                ag = a[g * SUBLANES:(g + 1) * SUBLANES]
                h = ag * h + v[g * SUBLANES:(g + 1) * SUBLANES]
                acc = ag * acc
                hs[g], ps[g] = h, acc
            local = jnp.concatenate(hs, axis=0)
            if d == 1:
                s_ref[pl.ds(r0, tt), :] = local
                ab_ref[pl.ds(r0, tt), :] = jnp.concatenate(ps, axis=0)
            else:
                s_ref[pl.ds(r0, tt), :] += local
                af_ref[pl.ds(r0, tt), :] = jnp.concatenate(ps, axis=0)
            return h, acc

        return lax.fori_loop(0, nt, body, (jnp.zeros(shape, F32), jnp.ones(shape, F32)), unroll=2)

    hb, pb = sweep(1)
    hf, pf = sweep(0)
    cf = [h0_ref[0, 0:1, :]]
    for s in range(SUBLANES - 1):
        cf.append(hf[s:s + 1] + pf[s:s + 1] * cf[s])
    cb = [None] * SUBLANES
    cb[SUBLANES - 1] = h0_ref[0, 1:2, :]
    for s in range(SUBLANES - 1, 0, -1):
        cb[s - 1] = hb[s:s + 1] + pb[s:s + 1] * cb[s]
    last = SUBLANES - 1
    ht_ref[0] = jnp.concatenate([hf[last:] + pf[last:] * cf[last], hb[0:1] + pb[0:1] * cb[0],
                                 jnp.zeros((SUBLANES - 2, RG_HALF), F32)], axis=0)
    cf = jnp.concatenate(cf, axis=0)[None]
    cb = jnp.concatenate(cb, axis=0)[None]

    def fix(s, carry):
        r0 = pl.multiple_of(s * tt, tt)
        corr = (af_ref[pl.ds(r0, tt), :].reshape(ng, SUBLANES, RG_HALF) * cf
                + ab_ref[pl.ds(r0, tt), :].reshape(ng, SUBLANES, RG_HALF) * cb).reshape(tt, RG_HALF)
        g = gather(g_ref, s).astype(F32)
        y = ((s_ref[pl.ds(r0, tt), :] + corr) * _silu(g)).astype(BF16)
        for m in range(nsub):
            back = _dot(permt_ref[...], y[m * block:(m + 1) * block]).astype(BF16)
            for c in range(SUBLANES):
                y_ref[0, source_rows(s, m, c), :] = back[c * RG_GATHER:(c + 1) * RG_GATHER]
        return carry

    lax.fori_loop(0, nt, fix, 0, unroll=4 if nt % 4 == 0 else 1)


def _rg(rg, pw, l, h0, tt):
    bsz, t, _ = rg.shape
    block = SUBLANES * RG_GATHER
    assert t % tt == 0 and tt % block == 0
    nhalf = BRANCH_W // RG_HALF
    pos = np.arange(block)
    perm = np.zeros((block, block))
    perm[pos, RG_GATHER * (pos % SUBLANES) + pos // SUBLANES] = 1.0
    perm = jnp.asarray(perm, dtype=F32).astype(BF16)
    seq = lambda col0: pl.BlockSpec((1, t, RG_HALF), lambda b, h: (b, 0, col0 + h))
    pspec = pl.BlockSpec((block, block), lambda b, h: (0, 0))
    state = pl.BlockSpec((1, SUBLANES, RG_HALF), lambda b, h: (b, 0, h))
    seq_scratch = pltpu.VMEM((t, RG_HALF), F32)
    return pl.pallas_call(
        functools.partial(_rg_kernel, t=t, tt=tt),
        grid=(bsz, nhalf),
        in_specs=[seq(0), seq(nhalf),
                  pl.BlockSpec((None, 2, None, RG_HALF, 2 * RG_HALF), lambda b, h: (l, 0, h, 0, 0)),
                  pl.BlockSpec((None, 2, None, 1, 2 * RG_HALF), lambda b, h: (l, 0, h, 0, 0)),
                  pl.BlockSpec((None, 2, RG_HALF), lambda b, h: (l, 0, h)),
                  pspec, pspec, state],
        out_specs=[seq(0), state],
        out_shape=[jax.ShapeDtypeStruct((bsz, t, BRANCH_W), BF16),
                   jax.ShapeDtypeStruct((bsz, SUBLANES, BRANCH_W), F32)],
        scratch_shapes=[seq_scratch, seq_scratch, seq_scratch, pltpu.VMEM((t, RG_HALF), BF16)],
        compiler_params=_params("arbitrary", "arbitrary"), name="rglru",
    )(rg, rg, pw["rg_wg"], pw["rg_bg"], pw["rg_lam"], perm, perm.T, h0)


def _ssd_kernel(xbc_ref, z_ref, dt_ref, dtb_ref, alog_ref, dsk_ref, nw_ref, ecol_ref, ehead_ref,
                hmask_ref, h0_ref, y_ref, ht_ref, yb_ref, csp_ref, adjt_ref, diagt_ref, wgtt_ref, etot_ref,
                loc_ref, st_ref, *, t):
    q = SSD_CHUNK
    nc = t // q
    npairs = SSD_HEADS // 2
    pairs_per_group = npairs // SSD_GROUPS
    lane1 = lax.broadcasted_iota(jnp.int32, (1, LANES), 1)
    a_row = -LOG2E * jnp.exp(alog_ref[...])
    fwd_lane = (lax.broadcasted_iota(jnp.int32, (q, LANES), 1) & (SSD_DT - 1)) < SSD_HEADS
    ri = lax.broadcasted_iota(jnp.int32, (q, q), 0)
    ci = lax.broadcasted_iota(jnp.int32, (q, q), 1)
    lower_b = jnp.where(ri >= ci, 1.0, 0.0).astype(BF16)
    head0_s = lax.broadcasted_iota(jnp.int32, (SSD_STATE, LANES), 1) < SSD_HD

    def pair_vec(row, l0, l1):
        return jnp.where(lane1 < SSD_HD, row[:, l0:l0 + 1], row[:, l1:l1 + 1])

    def small_rows(c):
        return pl.ds(pl.multiple_of(c * SSD_DT, SSD_DT), SSD_DT)

    def pass0(c, carry):
        r0 = pl.multiple_of(c * q, q)
        dtv = _softplus(dt_ref[0, pl.ds(r0, q), :] + dtb_ref[...])
        da = dtv * a_row
        pre = sum(_dot(lower_b, part.astype(BF16)) for part in _split3(da))
        tot = pre[q - 1:q, :]
        cs = jnp.where(fwd_lane, pre, tot - pre + da)
        csp_ref[pl.ds(r0, q), :] = _pack3(cs)
        adjt_ref[small_rows(c), :] = (cs - LOG2E * jnp.log(dtv)).T[:SSD_DT]
        diagt_ref[small_rows(c), :] = (LOG2E * jnp.log(dtv + pltpu.roll(dtv, SSD_HEADS, 1))).T[:SSD_DT]
        wgtt_ref[small_rows(c), :] = (dtv * jnp.exp2(tot - cs)).T[:SSD_DT]
        etot_ref[small_rows(c), :] = jnp.broadcast_to(jnp.exp2(tot), (SSD_DT, LANES))
        return carry

    def pass1(i, carry):
        c = nc - 1 - i
        r0 = pl.multiple_of(c * q, q)
        xs_b = xbc_ref[0, pl.ds(r0, q), 0:BRANCH_W]
        bm_b = xbc_ref[0, pl.ds(r0, q), BRANCH_W:BRANCH_W + LANES]
        cm_b = xbc_ref[0, pl.ds(r0, q), BRANCH_W + LANES:SSD_XBC]
        cs_p = csp_ref[pl.ds(r0, q), :]
        cs_col = _dot(cs_p, ecol_ref[...])
        ecs_b = jnp.exp2(_dot(cs_p, ehead_ref[:, BRANCH_W:]))
        adjt = adjt_ref[small_rows(c), :]
        diagt = diagt_ref[small_rows(c), :]
        wgtt = wgtt_ref[small_rows(c), :]
        etot = etot_ref[pl.ds(pl.multiple_of(c * SSD_DT, SSD_DT), 1), :]
        bt = bm_b.astype(F32).T
        xs_h0 = xs_b * hmask_ref[0:1, :]
        xs_h1 = xs_b * hmask_ref[1:2, :]
        ys = []
        for g in range(SSD_GROUPS):
            gs = slice(g * SSD_STATE, (g + 1) * SSD_STATE)
            cbm = lax.dot_general(cm_b[:, gs], bm_b[:, gs], (((1,), (1,)), ((), ())), preferred_element_type=F32)
            btg = bt[gs, :]
            for pp in range(pairs_per_group):
                pair = g * pairs_per_group + pp
                ps = slice(pair * LANES, (pair + 1) * LANES)
                state = st_ref[pair]
                lf0 = 2 * pair
                lb0 = SSD_HEADS + lf0
                ms = []
                for lf in (lf0, lf0 + 1):
                    lb = SSD_HEADS + lf
                    arg = jnp.where(ri > ci, cs_col[:, lf * LANES:(lf + 1) * LANES] - adjt[lf:lf + 1, :],
                                    jnp.where(ri < ci, cs_col[:, lb * LANES:(lb + 1) * LANES] - adjt[lb:lb + 1, :],
                                              diagt[lf:lf + 1, :]))
                    ms.append((cbm * jnp.exp2(arg)).astype(BF16))
                yd = _dot(jnp.concatenate(ms, axis=1), jnp.concatenate([xs_h0[:, ps], xs_h1[:, ps]], axis=0))
                yo = _dot(cm_b[:, gs], state.astype(BF16)) * ecs_b[:, ps]
                ys.append(yd + yo)
                stack = jnp.concatenate([(btg * wgtt[l:l + 1, :]).astype(BF16) for l in (lf0, lf0 + 1, lb0, lb0 + 1)],
                                        axis=0)
                res = _dot(stack, xs_b[:, ps])
                n = SSD_STATE
                loc_ref[c, pair] = jnp.where(head0_s, res[0:n], res[n:2 * n])
                st_ref[pair] = state * pair_vec(etot, lb0, lb0 + 1) + jnp.where(head0_s, res[2 * n:3 * n], res[3 * n:])
        yb_ref[pl.ds(r0, q), :] = jnp.concatenate(ys, axis=1) + dsk_ref[...] * xs_b.astype(F32)
        return carry

    def pass2(c, carry):
        r0 = pl.multiple_of(c * q, q)
        cm_b = xbc_ref[0, pl.ds(r0, q), BRANCH_W + LANES:SSD_XBC]
        ecs_f = jnp.exp2(_dot(csp_ref[pl.ds(r0, q), :], ehead_ref[:, :BRANCH_W]))
        etot = etot_ref[pl.ds(pl.multiple_of(c * SSD_DT, SSD_DT), 1), :]
        ys = []
        for pair in range(npairs):
            g = pair // pairs_per_group
            state = st_ref[pair]
            lf0 = 2 * pair
            ys.append(_dot(cm_b[:, g * SSD_STATE:(g + 1) * SSD_STATE], state.astype(BF16))
                      * ecs_f[:, pair * LANES:(pair + 1) * LANES])
            st_ref[pair] = state * pair_vec(etot, lf0, lf0 + 1) + loc_ref[c, pair]
        yt = yb_ref[pl.ds(r0, q), :] + jnp.concatenate(ys, axis=1)
        yt = yt * _silu(z_ref[0, pl.ds(r0, q), :].astype(F32))
        ms = jnp.mean(yt * yt, axis=-1, keepdims=True)
        y_ref[0, pl.ds(r0, q), :] = ((yt * lax.rsqrt(ms + NORM_EPS)) * nw_ref[...]).astype(BF16)
        return carry

    lax.fori_loop(0, nc, pass0, 0, unroll=8 if nc % 8 == 0 else 1)
    for p in range(npairs):
        st_ref[p] = h0_ref[0, 1, p]
    lax.fori_loop(0, nc, pass1, 0, unroll=8 if nc % 8 == 0 else 2)
    for p in range(npairs):
        ht_ref[0, 1, p] = st_ref[p]
        st_ref[p] = h0_ref[0, 0, p]
    lax.fori_loop(0, nc, pass2, 0, unroll=8 if nc % 8 == 0 else 2)
    for p in range(npairs):
        ht_ref[0, 0, p] = st_ref[p]


@functools.lru_cache(maxsize=None)
def _ssd_tables():
    lanes = np.arange(LANES)[:, None]
    src = np.where(lanes < 3 * SSD_DT, lanes % SSD_DT, -1)
    ecol = (src == np.arange(SSD_DT * LANES)[None, :] // LANES)
    ehead = (src == np.arange(2 * BRANCH_W)[None, :] // SSD_HD)
    even_head = (np.arange(BRANCH_W) // SSD_HD) % 2 == 0
    hmask = np.zeros((BF16_ROWS, BRANCH_W))
    hmask[0] = even_head
    hmask[1] = ~even_head
    as_bf16 = lambda a: jnp.asarray(a, dtype=F32).astype(BF16)
    return as_bf16(ecol), as_bf16(ehead), as_bf16(hmask)


def _ssd(xbc, z, dt, pw, l, h0):
    bsz, t, _ = xbc.shape
    assert t % SSD_CHUNK == 0
    npairs = SSD_HEADS // 2
    ecol, ehead, hmask = _ssd_tables()
    full = lambda shape: pl.BlockSpec(shape, lambda b: (0,) * len(shape))
    lay = lambda *shape: _layer_spec(l, shape)
    seq = lambda width: pl.BlockSpec((1, t, width), lambda b: (b, 0, 0))
    st_spec = pl.BlockSpec((1, 2, npairs, SSD_STATE, LANES), lambda b: (b, 0, 0, 0, 0))
    return pl.pallas_call(
        functools.partial(_ssd_kernel, t=t),
        grid=(bsz,),
        in_specs=[seq(SSD_XBC), seq(BRANCH_W), seq(LANES), lay(1, LANES), lay(1, LANES),
                  lay(1, BRANCH_W), lay(1, BRANCH_W), full(ecol.shape), full(ehead.shape), full(hmask.shape),
                  st_spec],
        out_specs=[seq(BRANCH_W), st_spec],
        out_shape=[jax.ShapeDtypeStruct((bsz, t, BRANCH_W), BF16),
                   jax.ShapeDtypeStruct((bsz, 2, npairs, SSD_STATE, LANES), F32)],
        scratch_shapes=[pltpu.VMEM((t, BRANCH_W), F32), pltpu.VMEM((t, LANES), BF16)]
        + [pltpu.VMEM((t // SSD_CHUNK * SSD_DT, LANES), F32)] * 4
        + [pltpu.VMEM((t // SSD_CHUNK, npairs, SSD_STATE, LANES), F32),
                        pltpu.VMEM((npairs, SSD_STATE, LANES), F32)],
        compiler_params=_params("arbitrary"),
        name="ssd",
    )(xbc, z, dt, pw["ssd_dtb"], pw["ssd_alog"], pw["ssd_dskip"], pw["ssd_nw"], ecol, ehead, hmask, h0)


@functools.lru_cache(maxsize=None)
def _fourier_tables(n, blk):
    half = n // 2
    nh = half // blk
    assert nh * blk == half and nh <= SUBLANES
    p = np.arange(FN_GD)
    ang_c = 2.0 * np.pi * ((p[:, None] * p[None, :]) % FN_GD) / FN_GD
    eye = np.eye(BRANCH_W // FN_GD)
    cc = np.kron(eye, np.cos(ang_c))
    sc = np.kron(eye, np.sin(ang_c))
    k = np.arange(half)
    ang_t = 2.0 * np.pi * ((k[:, None] * k[None, :]) % n) / n
    ct = np.cos(ang_t)
    stn = -np.sin(ang_t)
    assert blk % FN_FLIP == 0
    jp = np.zeros((FN_FLIP, FN_FLIP))
    r = np.arange(1, FN_FLIP)
    jp[r, FN_FLIP - r] = 1.0
    tt = np.arange(n)
    ks = blk * (np.arange(nh) + 1)
    ang_k = 2.0 * np.pi * ((ks[:, None] * tt[None, :]) % n) / n
    tc = np.zeros((2 * SUBLANES, n))
    ts = np.zeros((2 * SUBLANES, n))
    tc[:nh] = np.cos(ang_k)
    ts[:nh] = np.sin(ang_k)
    tc[SUBLANES, half] = 1.0
    as_bf16 = lambda a: jnp.asarray(a, dtype=F32).astype(BF16)
    return dict(cc=as_bf16(cc), sc=as_bf16(sc), ct=as_bf16(ct), stn=as_bf16(stn), jp=as_bf16(jp),
                tc=as_bf16(tc), ts=as_bf16(ts))


def _mirror(jp_ref, src, row0):
    n = src.shape[0]
    nsub = n // FN_FLIP
    first = lax.broadcasted_iota(jnp.int32, (FN_FLIP, src.shape[1]), 0) == 0
    out = []
    for a in range(nsub):
        s = nsub - 1 - a
        head = row0 if a == 0 else src[(s + 1) * FN_FLIP:(s + 1) * FN_FLIP + 1].astype(F32)
        out.append(jnp.where(first, head, _dot(jp_ref[...], src[s * FN_FLIP:(s + 1) * FN_FLIP])))
    return jnp.concatenate(out, axis=0)


def _fn_fold_kernel(xj_ref, xm_ref, xr_ref, jp_ref, cc_ref, sc_ref, eo_ref):
    j = pl.program_id(1)
    xj = xj_ref[0]
    row0 = xr_ref[0, 0:1, :].astype(F32) * jnp.where(j == 0, 0.0, 1.0)
    mir = _mirror(jp_ref, xm_ref[0], row0).astype(BF16)
    e = _dot(xj, cc_ref[...]) + _dot(mir, cc_ref[...])
    o = _dot(xj, sc_ref[...]) - _dot(mir, sc_ref[...])
    eo_ref[0] = jnp.concatenate([e, o], axis=1).astype(BF16)


def _fn_main_kernel(ct_ref, stn_ref, eo_ref, x_ref, tc_ref, ts_ref, cc_ref, sc_ref, g_ref, jp_ref, y_ref, aux_ref,
                    *, scale):
    m = pl.program_id(1)
    blk = ct_ref.shape[0]
    nblk = y_ref.shape[1] // blk
    rows = lax.broadcasted_iota(jnp.int32, (blk, BRANCH_W), 0)

    @pl.when(m == 0)
    def _():
        x = x_ref[0]
        xc = _dot(tc_ref[...], x).astype(BF16)
        xs = _dot(ts_ref[...], x).astype(BF16)
        aux_ref[0] = _dot(xc, cc_ref[...]) + _dot(xs, sc_ref[...])

    p = _dot(ct_ref[...], eo_ref[0, :, 0:BRANCH_W])
    qn = _dot(stn_ref[...], eo_ref[0, :, BRANCH_W:2 * BRANCH_W])
    sgn = (1 - 2 * ((m * blk + rows) & 1)).astype(F32)
    p = p + sgn * aux_ref[0, SUBLANES:SUBLANES + 1, :]
    lo = pl.multiple_of(m * blk, blk)
    hi = pl.multiple_of((nblk - 1 - m) * blk, blk)
    y_ref[0, pl.ds(lo, blk), :] = ((p + qn) * (_silu(g_ref[0, pl.ds(lo, blk), :].astype(F32)) * scale)).astype(BF16)
    sel = lax.broadcasted_iota(jnp.int32, (2 * SUBLANES, BRANCH_W), 0) == m
    row0 = jnp.sum(jnp.where(sel, aux_ref[0], 0.0), axis=0, keepdims=True)
    y_ref[0, pl.ds(hi, blk), :] = (_mirror(jp_ref, (p - qn).astype(BF16), row0)
                                   * (_silu(g_ref[0, pl.ds(hi, blk), :].astype(F32)) * scale)).astype(BF16)


def _fourier(fn, blk):
    bsz, n, _ = fn.shape
    tb = _fourier_tables(n, blk)
    half = n // 2
    nh = half // blk
    nblk = 2 * nh
    rows16 = n // BF16_ROWS
    full2 = lambda shape: pl.BlockSpec(shape, lambda b, j: (0,) * len(shape))
    eo = pl.pallas_call(
        _fn_fold_kernel,
        grid=(bsz, nh),
        in_specs=[pl.BlockSpec((1, blk, BRANCH_W), lambda b, j: (b, j, 0)),
                  pl.BlockSpec((1, blk, BRANCH_W), lambda b, j: (b, nblk - 1 - j, 0)),
                  pl.BlockSpec((1, BF16_ROWS, BRANCH_W),
                               lambda b, j: (b, jnp.minimum((blk // BF16_ROWS) * (nblk - j), rows16 - 1), 0)),
                  full2((FN_FLIP, FN_FLIP)), full2((BRANCH_W, BRANCH_W)), full2((BRANCH_W, BRANCH_W))],
        out_specs=pl.BlockSpec((1, blk, 2 * BRANCH_W), lambda b, j: (b, j, 0)),
        out_shape=jax.ShapeDtypeStruct((bsz, half, 2 * BRANCH_W), BF16),
        compiler_params=_params("arbitrary", "arbitrary"),
        name="fn_fold",
    )(fn, fn, fn, tb["jp"], tb["cc"], tb["sc"])
    aux_rows = 2 * SUBLANES
    return pl.pallas_call(
        functools.partial(_fn_main_kernel, scale=float(1.0 / np.sqrt(n * FN_GD))),
        grid=(bsz, nh),
        in_specs=[pl.BlockSpec((blk, half), lambda b, m: (m, 0)),
                  pl.BlockSpec((blk, half), lambda b, m: (m, 0)),
                  pl.BlockSpec((1, half, 2 * BRANCH_W), lambda b, m: (b, 0, 0)),
                  pl.BlockSpec((1, n, BRANCH_W), lambda b, m: (b, 0, 0)),
                  full2((aux_rows, n)), full2((aux_rows, n)),
                  full2((BRANCH_W, BRANCH_W)), full2((BRANCH_W, BRANCH_W)),
                  pl.BlockSpec((1, n, BRANCH_W), lambda b, m: (b, 0, 1)),
                  full2((FN_FLIP, FN_FLIP))],
        out_specs=pl.BlockSpec((1, n, BRANCH_W), lambda b, m: (b, 0, 0)),
        out_shape=jax.ShapeDtypeStruct((bsz, n, BRANCH_W), BF16),
        scratch_shapes=[pltpu.VMEM((1, aux_rows, BRANCH_W), F32)],
        compiler_params=_params("arbitrary", "arbitrary"),
        name="fn_main",
    )(tb["ct"], tb["stn"], eo, fn, tb["tc"], tb["ts"], tb["cc"], tb["sc"], fn, tb["jp"])


def _prepare_weights(norm_w, w_in, w_out, rg_conv_w, rg_conv_b, rg_gate_a_w, rg_gate_a_b, rg_gate_x_w, rg_gate_x_b,
                     rg_lambda, sc_conv_w, ssd_conv_w, ssd_conv_b, ssd_dt_bias, ssd_a_log, ssd_d, ssd_norm_w,
                     final_norm_w):
    depth = w_in.shape[0]
    assert REF_DT_HI - REF_DT_LO == SSD_DT and w_in.shape[2] - SSD_DT + LANES == C_END
    lane = np.arange(LANES)
    dt_rep = jnp.asarray(lane[:, None] == lane[None, :] % SSD_DT, dtype=F32).astype(BF16)
    w_a = w_in[:, :, :REF_DT_LO + LANES].astype(BF16)
    w_b = w_in[:, :, REF_DT_HI:].astype(BF16)
    w_a, w_b = lax.optimization_barrier((w_a, w_b))
    heads_per_half = RG_HALF // RG_HD
    eye = jnp.eye(heads_per_half, dtype=F32)

    def blockdiag(wg):
        wg = wg.reshape(depth, 2, -1, heads_per_half, RG_HD, RG_HD)
        return jnp.einsum("ldhjio,jk->ldhjiko", wg, eye).reshape(depth, 2, -1, RG_HALF, RG_HALF)

    halves = lambda b: b.reshape(depth, 2, -1, 1, RG_HALF)
    rep = LANES // SSD_DT
    return dict(
        norm_w=norm_w.reshape(depth, 1, D_MODEL), final_w=final_norm_w.reshape(1, D_MODEL),
        w_in_a=w_a, w_in_b=w_b, dt_rep=dt_rep, w_out=w_out.astype(BF16), sc_cw=sc_conv_w,
        rg_cw=0.5 * rg_conv_w, rg_cb=0.5 * rg_conv_b.reshape(depth, 1, BRANCH_W),
        rg_wg=jnp.concatenate([blockdiag(rg_gate_a_w), blockdiag(rg_gate_x_w)], axis=-1).astype(BF16),
        rg_bg=0.5 * jnp.concatenate([halves(rg_gate_a_b), halves(rg_gate_x_b)], axis=-1),
        rg_lam=rg_lambda,
        ssd_cw=ssd_conv_w, ssd_cb=ssd_conv_b.reshape(depth, 1, SSD_XBC),
        ssd_dtb=jnp.tile(ssd_dt_bias.reshape(depth, 1, SSD_DT), (1, 1, rep)),
        ssd_alog=jnp.tile(ssd_a_log.reshape(depth, 1, SSD_DT), (1, 1, rep)),
        ssd_dskip=jnp.repeat(ssd_d, SSD_HD, axis=1).reshape(depth, 1, BRANCH_W),
        ssd_nw=ssd_norm_w.reshape(depth, 1, BRANCH_W),
    )


def _mix(x, mod, pw, l, mod_row, rowlen, tm, tt, blk, rg_h0, ssd_h0, with_output, final):
    rg, xbc, z, dt, *rest = _inproj(x, mod, pw, l, rowlen, tm, mod_row, scan_only=not with_output)
    y_rg, rg_st = _rg(rg, pw, l, rg_h0, tt)
    y_ssd, ssd_st = _ssd(xbc, z, dt, pw, l, ssd_h0)
    if not with_output:
        return None, rg_st, ssd_st
    fn, y_sc = rest
    y_fn = _fourier(fn, blk)
    return _outproj(y_rg, y_sc, y_fn, y_ssd, x, mod, pw, l, final, tm, mod_row), rg_st, ssd_st


def kernel(x, c, ctx, c_ctx, ada_w, ada_b, norm_w, w_in, w_out, rg_conv_w, rg_conv_b, rg_gate_a_w, rg_gate_a_b,
           rg_gate_x_w, rg_gate_x_b, rg_lambda, sc_conv_w, ssd_conv_w, ssd_conv_b, ssd_dt_bias, ssd_a_log, ssd_d,
           ssd_norm_w, final_norm_w):
    bsz, seq, _ = x.shape
    ctx_len = ctx.shape[1]
    depth = w_in.shape[0]
    assert bsz + 1 <= SUBLANES and seq % SUBLANES == 0
    pw = _prepare_weights(norm_w, w_in, w_out, rg_conv_w, rg_conv_b, rg_gate_a_w, rg_gate_a_b, rg_gate_x_w,
                          rg_gate_x_b, rg_lambda, sc_conv_w, ssd_conv_w, ssd_conv_b, ssd_dt_bias, ssd_a_log, ssd_d,
                          ssd_norm_w, final_norm_w)
    c8 = jnp.concatenate([c, c_ctx[None, :], jnp.zeros((SUBLANES - bsz - 1, D_MODEL), F32)], axis=0)
    mod = _ada(c8, ada_w, ada_b.reshape(depth, 1, 3 * D_MODEL))
    zeros_rg = jnp.zeros((bsz, SUBLANES, BRANCH_W), F32)
    zeros_ssd = jnp.zeros((bsz, 2, SSD_HEADS // 2, SSD_STATE, LANES), F32)
    tm_lat = min(1024, seq)
    tt_lat = min(2048, seq)
    tt_ctx = min(1024, ctx_len)
    blk_lat = min(512, seq // 2)
    blk_ctx = min(512, ctx_len // 2)
    for l in range(depth):
        last = l == depth - 1
        new_ctx, rg_st, ssd_st = _mix(ctx, mod, pw, l, bsz, ctx_len, ctx_len, tt_ctx, blk_ctx,
                                      zeros_rg, zeros_ssd, not last, False)
        x, _, _ = _mix(x, mod, pw, l, None, GRID_W, tm_lat, tt_lat, blk_lat, rg_st, ssd_st, True, last)
        if not last:
            ctx = new_ctx
    return x
```

```python
import functools

import numpy as np
import jax
import jax.numpy as jnp
from jax import lax
from jax.experimental import pallas as pl
from jax.experimental.pallas import tpu as pltpu

F32 = jnp.float32
BF16 = jnp.bfloat16

D_MODEL = 1024
D_INNER = 2048
BRANCH_W = 512
GRID_W = 64
RG_HD = 64
RG_C = 8.0
RG_HALF = 256
SSD_HEADS = 8
SSD_HD = 64
SSD_GROUPS = 2
SSD_STATE = 64
SSD_XBC = BRANCH_W + 2 * SSD_GROUPS * SSD_STATE
SSD_CHUNK = 128
SSD_DT = 2 * SSD_HEADS
FN_GD = 128
FN_FLIP = 128
NORM_EPS = 1e-6

LANES = 128
SUBLANES = 8
BF16_ROWS = 16
VMEM_LIMIT_BYTES = 60000 * 1024

C_RGX, C_XBC, C_RGG, C_Z, C_SC, C_FN, C_DT, C_END = 0, 512, 1280, 1792, 2304, 4352, 5376, 5504
REF_DT_LO, REF_DT_HI = 1280, 1296


def _silu(v):
    h = 0.5 * v
    return h + h * jnp.tanh(h)


def _softplus(v):
    return jnp.maximum(v, 0.0) + jnp.log1p(jnp.exp(-jnp.abs(v)))


def _dot(a, b):
    return jnp.dot(a, b, preferred_element_type=F32)


def _split3(v):
    hi = v.astype(BF16).astype(F32)
    rest = v - hi
    mid = rest.astype(BF16).astype(F32)
    return hi, mid, rest - mid


def _pack3(v):
    hi, mid, lo = _split3(v)
    period = lax.broadcasted_iota(jnp.int32, v.shape, 1) // SSD_DT
    return jnp.where(period == 0, hi, jnp.where(period == 1, mid, jnp.where(period == 2, lo, 0.0))).astype(BF16)


def _params(*semantics):
    return pltpu.CompilerParams(dimension_semantics=semantics, vmem_limit_bytes=VMEM_LIMIT_BYTES)


def _layer_spec(l, shape, single_buffer=False):
    mode = dict(pipeline_mode=pl.Buffered(1)) if single_buffer else {}
    return pl.BlockSpec((None,) + tuple(shape), lambda *_: (l,) + (0,) * len(shape), **mode)


def _ada_kernel(c_ref, w_ref, b_ref, o_ref):
    s = _silu(c_ref[...])
    hi = s.astype(BF16)
    lo = (s - hi.astype(F32)).astype(BF16)
    w = w_ref[...].astype(BF16)
    o_ref[...] = _dot(hi, w) + _dot(lo, w) + b_ref[...]


def _ada(c8, w, b):
    depth = w.shape[0]
    tn = 1024
    return pl.pallas_call(
        _ada_kernel,
        grid=(depth, 3 * D_MODEL // tn),
        in_specs=[pl.BlockSpec((SUBLANES, D_MODEL), lambda l, j: (0, 0)),
                  pl.BlockSpec((None, D_MODEL, tn), lambda l, j: (l, 0, j)),
                  pl.BlockSpec((None, 1, tn), lambda l, j: (l, 0, j))],
        out_specs=pl.BlockSpec((None, SUBLANES, tn), lambda l, j: (l, 0, j)),
        out_shape=jax.ShapeDtypeStruct((depth, SUBLANES, 3 * D_MODEL), F32),
        compiler_params=_params("arbitrary", "arbitrary"),
        name="ada",
    )(c8, w, b)


def _inproj_kernel(*refs, rowlen, mod_row, halo, scan_only):
    if halo:
        x_ref, xp_ref, xn_ref, *refs = refs
    else:
        x_ref, *refs = refs
    nw_ref, sh_ref, sc_ref, wa_ref, wb_ref, rep_ref, cw_ref, rcw_ref, rcb_ref, scw_ref, scb_ref, *outs, tap_ref = refs
    nb, tm, _ = x_ref.shape
    rows = nb * tm
    i = pl.program_id(1)
    row = pl.program_id(0) if mod_row is None else mod_row
    if halo:
        x = jnp.concatenate([xp_ref[0], x_ref[0], xn_ref[0]], axis=0)
    else:
        x = x_ref[...].reshape(rows, D_MODEL)
    ms = jnp.mean(x * x, axis=-1, keepdims=True)
    gain = nw_ref[...] * (1.0 + sc_ref[pl.ds(row, 1), :])
    hx = ((x * lax.rsqrt(ms + NORM_EPS)) * gain + sh_ref[pl.ds(row, 1), :]).astype(BF16)
    hb = hx[halo:halo + rows]

    def wcols(lo, hi):
        if hi <= REF_DT_LO:
            return wa_ref[:, lo:hi]
        if lo >= C_DT:
            assert (lo, hi) == (C_DT, C_END)
            return _dot(wa_ref[:, REF_DT_LO:REF_DT_LO + LANES], rep_ref[...]).astype(BF16)
        assert lo >= REF_DT_LO and hi <= C_DT
        return wb_ref[:, lo - REF_DT_LO:hi - REF_DT_LO]

    def proj(lo, hi):
        return _dot(hb, wcols(lo, hi))

    def proj_halo(lo, hi):
        return _dot(hx, wcols(lo, hi))

    def conv4(p, cw, cb):
        n = p.shape[0]
        if halo:
            keep_prev = jnp.where(i == 0, 0.0, 1.0)
            keep_next = jnp.where(i == pl.num_programs(1) - 1, 0.0, 1.0)
            w = p.shape[1]
            tap_ref[0:halo, 0:w] = p[:halo] * keep_prev
            tap_ref[halo:halo + rows, 0:w] = p[halo:halo + rows]
            tap_ref[halo + rows:n, 0:w] = p[halo + rows:] * keep_next
            taps = [tap_ref[halo + off:halo + off + rows, 0:w] for off in (-2, -1)]
            taps += [p[halo:halo + rows], tap_ref[halo + 1:halo + 1 + rows, 0:w]]
        else:
            pos = lax.broadcasted_iota(jnp.int32, p.shape, 0) & (tm - 1)
            taps = [jnp.where(pos >= 2, pltpu.roll(p, 2, 0), 0.0), jnp.where(pos >= 1, pltpu.roll(p, 1, 0), 0.0), p,
                    jnp.where(pos <= tm - 2, pltpu.roll(p, n - 1, 0), 0.0)]
        return sum(cw[k:k + 1, :] * taps[k] for k in range(4)) + cb

    def put(ref, value, lo=0):
        ref[:, :, lo:lo + value.shape[1]] = value.reshape(nb, tm, value.shape[1]).astype(ref.dtype)

    if scan_only:
        rg_ref, xbc_ref, z_ref, dt_ref = outs
    else:
        rg_ref, xbc_ref, z_ref, dt_ref, fn_ref, ysc_ref = outs
        v = proj(C_SC + 512, C_SC + 1024) * proj(C_SC + 1024, C_SC + 1536)
        pos = lax.broadcasted_iota(jnp.int32, (rows, BRANCH_W), 0) & (rowlen - 1)
        pad = SUBLANES
        edge = jnp.zeros((pad, BRANCH_W), F32)
        tap_ref[0:pad, 0:BRANCH_W] = edge
        tap_ref[pad:pad + rows, 0:BRANCH_W] = v
        tap_ref[pad + rows:2 * pad + rows, 0:BRANCH_W] = edge
        vm1 = jnp.where(pos == 0, 0.0, tap_ref[pad - 1:pad - 1 + rows, 0:BRANCH_W])
        vp1 = jnp.where(pos == rowlen - 1, 0.0, tap_ref[pad + 1:pad + 1 + rows, 0:BRANCH_W])
        vc = cw_ref[0:1, :] * vm1 + cw_ref[1:2, :] * v + cw_ref[2:3, :] * vp1
        put(ysc_ref, proj(C_SC, C_SC + 512) * vc * _silu(proj(C_SC + 1536, C_FN)))
        put(fn_ref, proj(C_FN, C_DT))
    put(xbc_ref, _silu(conv4(proj_halo(C_XBC, C_RGG), scw_ref[...], scb_ref[...])))
    put(rg_ref, conv4(proj_halo(C_RGX, C_XBC), rcw_ref[...], rcb_ref[...]))
    put(rg_ref, proj(C_RGG, C_Z), BRANCH_W)
    put(z_ref, proj(C_Z, C_SC))
    put(dt_ref, proj(C_DT, C_END))


def _inproj(x, mod, pw, l, rowlen, tm, mod_row, scan_only=False):
    bsz, t, _ = x.shape
    whole = tm == t and mod_row is not None and tm & (tm - 1) == 0
    halo = 0 if whole else BF16_ROWS
    nb = bsz if whole else 1
    assert t % tm == 0 and tm % rowlen == 0 and rowlen & (rowlen - 1) == 0 and tm % BF16_ROWS == 0
    per_tile = tm // BF16_ROWS
    last_halo = t // BF16_ROWS - 1
    tok = lambda width: pl.BlockSpec((nb, tm, width), lambda b, i: (b, i, 0))
    lay = lambda *shape: _layer_spec(l, shape)
    out = lambda width, dt: jax.ShapeDtypeStruct((bsz, t, width), dt)
    x_specs, xs = [tok(D_MODEL)], [x]
    if halo:
        x_specs += [pl.BlockSpec((1, halo, D_MODEL), lambda b, i: (b, jnp.maximum(i * per_tile - 1, 0), 0)),
                    pl.BlockSpec((1, halo, D_MODEL), lambda b, i: (b, jnp.minimum((i + 1) * per_tile, last_halo), 0))]
        xs += [x, x]
    widths = [(1024, BF16), (SSD_XBC, BF16), (512, BF16), (LANES, F32)]
    if not scan_only:
        widths += [(1024, BF16), (512, BF16)]
    return pl.pallas_call(
        functools.partial(_inproj_kernel, rowlen=rowlen, mod_row=mod_row, halo=halo, scan_only=scan_only),
        grid=(bsz // nb, t // tm),
        in_specs=x_specs + [lay(1, D_MODEL),
                            pl.BlockSpec((None, SUBLANES, D_MODEL), lambda b, i: (l, 0, 0)),
                            pl.BlockSpec((None, SUBLANES, D_MODEL), lambda b, i: (l, 0, 1)),
                            _layer_spec(l, pw["w_in_a"].shape[1:], single_buffer=True),
                            _layer_spec(l, pw["w_in_b"].shape[1:], single_buffer=True),
                            pl.BlockSpec((LANES, LANES), lambda b, i: (0, 0)),
                            lay(3, BRANCH_W), lay(4, BRANCH_W), lay(1, BRANCH_W),
                            lay(4, SSD_XBC), lay(1, SSD_XBC)],
        out_specs=[tok(w) for w, _ in widths],
        out_shape=[out(w, dt) for w, dt in widths],
        scratch_shapes=[pltpu.VMEM((nb * tm + 2 * BF16_ROWS, SSD_XBC), F32)],
        compiler_params=_params("arbitrary", "arbitrary"),
        name="inproj",
    )(*xs, pw["norm_w"], mod, mod, pw["w_in_a"], pw["w_in_b"], pw["dt_rep"], pw["sc_cw"], pw["rg_cw"], pw["rg_cb"],
      pw["ssd_cw"], pw["ssd_cb"])


def _outproj_kernel(yrg_ref, ysc_ref, yfn_ref, yssd_ref, w_ref, x_ref, g_ref, fw_ref, o_ref, *, final, mod_row):
    row = pl.program_id(0) if mod_row is None else mod_row
    nb, tm, _ = x_ref.shape
    flat = lambda ref: ref[...].reshape(nb * tm, ref.shape[2])
    acc = _dot(flat(yrg_ref), w_ref[0:512, :])
    acc += _dot(flat(ysc_ref), w_ref[512:1024, :])
    acc += _dot(flat(yfn_ref), w_ref[1024:1536, :])
    acc += _dot(flat(yssd_ref), w_ref[1536:2048, :])
    xn = flat(x_ref) + g_ref[pl.ds(row, 1), :] * acc
    if final:
        ms = jnp.mean(xn * xn, axis=-1, keepdims=True)
        xn = (xn * lax.rsqrt(ms + NORM_EPS)) * fw_ref[...]
    o_ref[...] = xn.reshape(nb, tm, D_MODEL)


def _outproj(y_rg, y_sc, y_fn, y_ssd, x, mod, pw, l, final, tm, mod_row):
    bsz, t, _ = x.shape
    nb = bsz if (tm == t and mod_row is not None) else 1
    ytok = pl.BlockSpec((nb, tm, BRANCH_W), lambda b, i: (b, i, 0))
    xtok = pl.BlockSpec((nb, tm, D_MODEL), lambda b, i: (b, i, 0))
    return pl.pallas_call(
        functools.partial(_outproj_kernel, final=final, mod_row=mod_row),
        grid=(bsz // nb, t // tm),
        in_specs=[ytok, ytok, ytok, ytok,
                  _layer_spec(l, (D_INNER, D_MODEL), single_buffer=True),
                  xtok,
                  pl.BlockSpec((None, SUBLANES, D_MODEL), lambda b, i: (l, 0, 2)),
                  pl.BlockSpec((1, D_MODEL), lambda b, i: (0, 0))],
        out_specs=xtok,
        out_shape=jax.ShapeDtypeStruct((bsz, t, D_MODEL), F32),
        compiler_params=_params("arbitrary", "arbitrary"),
        name="outproj",
    )(y_rg, y_sc, y_fn, y_ssd, pw["w_out"], x, mod, pw["final_w"])


MIN_NORMAL_F32 = float(np.finfo(np.float32).tiny)
LOG2E = float(np.log2(np.e))


def _rg_coefficients(uh_b, wg, bg_half, lam):
    uh = uh_b.astype(F32)
    th = jnp.tanh(_dot(uh_b, wg) + bg_half)
    k = (-0.5 * RG_C * np.log2(np.e)) * _softplus(-lam)
    a = jnp.exp2(k + k * th[:, :RG_HALF])
    x = 1.0 - a * a
    root = x * lax.rsqrt(jnp.maximum(x, MIN_NORMAL_F32))
    return a, root * (uh + uh * th[:, RG_HALF:])


RG_GATHER = BF16_ROWS


def _rg_kernel(u_ref, g_ref, wg_ref, bg_ref, lam_ref, perm_ref, permt_ref, h0_ref, y_ref, ht_ref,
               s_ref, af_ref, ab_ref, up_ref, *, t, tt):
    nt = t // tt
    ng = tt // SUBLANES
    chunk = t // SUBLANES
    steps = tt // SUBLANES
    nsub = steps // RG_GATHER
    block = SUBLANES * RG_GATHER
    shape = (SUBLANES, RG_HALF)

    def source_rows(k, m, s):
        return pl.ds(pl.multiple_of(s * chunk + k * steps + m * RG_GATHER, RG_GATHER), RG_GATHER)

    def gather(ref, k):
        blocks = []
        for m in range(nsub):
            rows = jnp.concatenate([ref[0, source_rows(k, m, s), :] for s in range(SUBLANES)], axis=0)
            blocks.append(_dot(perm_ref[...], rows).astype(BF16))
        return jnp.concatenate(blocks, axis=0)

    def sweep(d):
        def body(i, carry):
            h, acc = carry
            s = i if d == 0 else nt - 1 - i
            r0 = pl.multiple_of(s * tt, tt)
            if d == 1:
                ub = gather(u_ref, s)
                up_ref[pl.ds(r0, tt), :] = ub
            else:
                ub = up_ref[pl.ds(r0, tt), :]
            a, v = _rg_coefficients(ub, wg_ref[d], bg_ref[d], lam_ref[d:d + 1, :])
            hs, ps = [None] * ng, [None] * ng
            for g in (range(ng) if d == 0 else range(ng - 1, -1, -1)):
                ag = a[g * SUBLANES:(g + 1) * SUBLANES]
                h = ag * h + v[g * SUBLANES:(g + 1) * SUBLANES]
                acc = ag * acc
                hs[g], ps[g] = h, acc
            local = jnp.concatenate(hs, axis=0)
            if d == 1:
                s_ref[pl.ds(r0, tt), :] = local
                ab_ref[pl.ds(r0, tt), :] = jnp.concatenate(ps, axis=0)
            else:
                s_ref[pl.ds(r0, tt), :] += local
                af_ref[pl.ds(r0, tt), :] = jnp.concatenate(ps, axis=0)
            return h, acc

        return lax.fori_loop(0, nt, body, (jnp.zeros(shape, F32), jnp.ones(shape, F32)), unroll=2)

    hb, pb = sweep(1)
    hf, pf = sweep(0)
    cf = [h0_ref[0, 0:1, :]]
    for s in range(SUBLANES - 1):
        cf.append(hf[s:s + 1] + pf[s:s + 1] * cf[s])
    cb = [None] * SUBLANES
    cb[SUBLANES - 1] = h0_ref[0, 1:2, :]
    for s in range(SUBLANES - 1, 0, -1):
        cb[s - 1] = hb[s:s + 1] + pb[s:s + 1] * cb[s]
    last = SUBLANES - 1
    ht_ref[0] = jnp.concatenate([hf[last:] + pf[last:] * cf[last], hb[0:1] + pb[0:1] * cb[0],
                                 jnp.zeros((SUBLANES - 2, RG_HALF), F32)], axis=0)
    cf = jnp.concatenate(cf, axis=0)[None]
    cb = jnp.concatenate(cb, axis=0)[None]

    def fix(s, carry):
        r0 = pl.multiple_of(s * tt, tt)
        corr = (af_ref[pl.ds(r0, tt), :].reshape(ng, SUBLANES, RG_HALF) * cf
                + ab_ref[pl.ds(r0, tt), :].reshape(ng, SUBLANES, RG_HALF) * cb).reshape(tt, RG_HALF)
        g = gather(g_ref, s).astype(F32)
        y = ((s_ref[pl.ds(r0, tt), :] + corr) * _silu(g)).astype(BF16)
        for m in range(nsub):
            back = _dot(permt_ref[...], y[m * block:(m + 1) * block]).astype(BF16)
            for c in range(SUBLANES):
                y_ref[0, source_rows(s, m, c), :] = back[c * RG_GATHER:(c + 1) * RG_GATHER]
        return carry

    lax.fori_loop(0, nt, fix, 0, unroll=4 if nt % 4 == 0 else 1)


def _rg(rg, pw, l, h0, tt):
    bsz, t, _ = rg.shape
    block = SUBLANES * RG_GATHER
    assert t % tt == 0 and tt % block == 0
    nhalf = BRANCH_W // RG_HALF
    pos = np.arange(block)
    perm = np.zeros((block, block))
    perm[pos, RG_GATHER * (pos % SUBLANES) + pos // SUBLANES] = 1.0
    perm = jnp.asarray(perm, dtype=F32).astype(BF16)
    seq = lambda col0: pl.BlockSpec((1, t, RG_HALF), lambda b, h: (b, 0, col0 + h))
    pspec = pl.BlockSpec((block, block), lambda b, h: (0, 0))
    state = pl.BlockSpec((1, SUBLANES, RG_HALF), lambda b, h: (b, 0, h))
    seq_scratch = pltpu.VMEM((t, RG_HALF), F32)
    return pl.pallas_call(
        functools.partial(_rg_kernel, t=t, tt=tt),
        grid=(bsz, nhalf),
        in_specs=[seq(0), seq(nhalf),
                  pl.BlockSpec((None, 2, None, RG_HALF, 2 * RG_HALF), lambda b, h: (l, 0, h, 0, 0)),
                  pl.BlockSpec((None, 2, None, 1, 2 * RG_HALF), lambda b, h: (l, 0, h, 0, 0)),
                  pl.BlockSpec((None, 2, RG_HALF), lambda b, h: (l, 0, h)),
                  pspec, pspec, state],
        out_specs=[seq(0), state],
        out_shape=[jax.ShapeDtypeStruct((bsz, t, BRANCH_W), BF16),
                   jax.ShapeDtypeStruct((bsz, SUBLANES, BRANCH_W), F32)],
        scratch_shapes=[seq_scratch, seq_scratch, seq_scratch, pltpu.VMEM((t, RG_HALF), BF16)],
        compiler_params=_params("arbitrary", "arbitrary"), name="rglru",
    )(rg, rg, pw["rg_wg"], pw["rg_bg"], pw["rg_lam"], perm, perm.T, h0)


def _ssd_kernel(xbc_ref, z_ref, dt_ref, dtb_ref, alog_ref, dsk_ref, nw_ref, ecol_ref, ehead_ref,
                hmask_ref, h0_ref, y_ref, ht_ref, yb_ref, csp_ref, adjt_ref, diagt_ref, wgtt_ref, etot_ref,
                loc_ref, st_ref, *, t):
    q = SSD_CHUNK
    nc = t // q
    npairs = SSD_HEADS // 2
    pairs_per_group = npairs // SSD_GROUPS
    lane1 = lax.broadcasted_iota(jnp.int32, (1, LANES), 1)
    a_row = -LOG2E * jnp.exp(alog_ref[...])
    fwd_lane = (lax.broadcasted_iota(jnp.int32, (q, LANES), 1) & (SSD_DT - 1)) < SSD_HEADS
    ri = lax.broadcasted_iota(jnp.int32, (q, q), 0)
    ci = lax.broadcasted_iota(jnp.int32, (q, q), 1)
    lower_b = jnp.where(ri >= ci, 1.0, 0.0).astype(BF16)
    head0_s = lax.broadcasted_iota(jnp.int32, (SSD_STATE, LANES), 1) < SSD_HD

    def pair_vec(row, l0, l1):
        return jnp.where(lane1 < SSD_HD, row[:, l0:l0 + 1], row[:, l1:l1 + 1])

    def small_rows(c):
        return pl.ds(pl.multiple_of(c * SSD_DT, SSD_DT), SSD_DT)

    def pass0(c, carry):
        r0 = pl.multiple_of(c * q, q)
        dtv = _softplus(dt_ref[0, pl.ds(r0, q), :] + dtb_ref[...])
        da = dtv * a_row
        pre = sum(_dot(lower_b, part.astype(BF16)) for part in _split3(da))
        tot = pre[q - 1:q, :]
        cs = jnp.where(fwd_lane, pre, tot - pre + da)
        csp_ref[pl.ds(r0, q), :] = _pack3(cs)
        adjt_ref[small_rows(c), :] = (cs - LOG2E * jnp.log(dtv)).T[:SSD_DT]
        diagt_ref[small_rows(c), :] = (LOG2E * jnp.log(dtv + pltpu.roll(dtv, SSD_HEADS, 1))).T[:SSD_DT]
        wgtt_ref[small_rows(c), :] = (dtv * jnp.exp2(tot - cs)).T[:SSD_DT]
        etot_ref[small_rows(c), :] = jnp.broadcast_to(jnp.exp2(tot), (SSD_DT, LANES))
        return carry

    def pass1(i, carry):
        c = nc - 1 - i
        r0 = pl.multiple_of(c * q, q)
        xs_b = xbc_ref[0, pl.ds(r0, q), 0:BRANCH_W]
        bm_b = xbc_ref[0, pl.ds(r0, q), BRANCH_W:BRANCH_W + LANES]
        cm_b = xbc_ref[0, pl.ds(r0, q), BRANCH_W + LANES:SSD_XBC]
        cs_p = csp_ref[pl.ds(r0, q), :]
        cs_col = _dot(cs_p, ecol_ref[...])
        ecs_b = jnp.exp2(_dot(cs_p, ehead_ref[:, BRANCH_W:]))
        adjt = adjt_ref[small_rows(c), :]
        diagt = diagt_ref[small_rows(c), :]
        wgtt = wgtt_ref[small_rows(c), :]
        etot = etot_ref[pl.ds(pl.multiple_of(c * SSD_DT, SSD_DT), 1), :]
        bt = bm_b.astype(F32).T
        xs_h0 = xs_b * hmask_ref[0:1, :]
        xs_h1 = xs_b * hmask_ref[1:2, :]
        ys = []
        for g in range(SSD_GROUPS):
            gs = slice(g * SSD_STATE, (g + 1) * SSD_STATE)
            cbm = lax.dot_general(cm_b[:, gs], bm_b[:, gs], (((1,), (1,)), ((), ())), preferred_element_type=F32)
            btg = bt[gs, :]
            for pp in range(pairs_per_group):
                pair = g * pairs_per_group + pp
                ps = slice(pair * LANES, (pair + 1) * LANES)
                state = st_ref[pair]
                lf0 = 2 * pair
                lb0 = SSD_HEADS + lf0
                ms = []
                for lf in (lf0, lf0 + 1):
                    lb = SSD_HEADS + lf
                    arg = jnp.where(ri > ci, cs_col[:, lf * LANES:(lf + 1) * LANES] - adjt[lf:lf + 1, :],
                                    jnp.where(ri < ci, cs_col[:, lb * LANES:(lb + 1) * LANES] - adjt[lb:lb + 1, :],
                                              diagt[lf:lf + 1, :]))
                    ms.append((cbm * jnp.exp2(arg)).astype(BF16))
                yd = _dot(jnp.concatenate(ms, axis=1), jnp.concatenate([xs_h0[:, ps], xs_h1[:, ps]], axis=0))
                yo = _dot(cm_b[:, gs], state.astype(BF16)) * ecs_b[:, ps]
                ys.append(yd + yo)
                stack = jnp.concatenate([(btg * wgtt[l:l + 1, :]).astype(BF16) for l in (lf0, lf0 + 1, lb0, lb0 + 1)],
                                        axis=0)
                res = _dot(stack, xs_b[:, ps])
                n = SSD_STATE
                loc_ref[c, pair] = jnp.where(head0_s, res[0:n], res[n:2 * n])
                st_ref[pair] = state * pair_vec(etot, lb0, lb0 + 1) + jnp.where(head0_s, res[2 * n:3 * n], res[3 * n:])
        yb_ref[pl.ds(r0, q), :] = jnp.concatenate(ys, axis=1) + dsk_ref[...] * xs_b.astype(F32)
        return carry

    def pass2(c, carry):
        r0 = pl.multiple_of(c * q, q)
        cm_b = xbc_ref[0, pl.ds(r0, q), BRANCH_W + LANES:SSD_XBC]
        ecs_f = jnp.exp2(_dot(csp_ref[pl.ds(r0, q), :], ehead_ref[:, :BRANCH_W]))
        etot = etot_ref[pl.ds(pl.multiple_of(c * SSD_DT, SSD_DT), 1), :]
        ys = []
        for pair in range(npairs):
            g = pair // pairs_per_group
            state = st_ref[pair]
            lf0 = 2 * pair
            ys.append(_dot(cm_b[:, g * SSD_STATE:(g + 1) * SSD_STATE], state.astype(BF16))
                      * ecs_f[:, pair * LANES:(pair + 1) * LANES])
            st_ref[pair] = state * pair_vec(etot, lf0, lf0 + 1) + loc_ref[c, pair]
        yt = yb_ref[pl.ds(r0, q), :] + jnp.concatenate(ys, axis=1)
        yt = yt * _silu(z_ref[0, pl.ds(r0, q), :].astype(F32))
        ms = jnp.mean(yt * yt, axis=-1, keepdims=True)
        y_ref[0, pl.ds(r0, q), :] = ((yt * lax.rsqrt(ms + NORM_EPS)) * nw_ref[...]).astype(BF16)
        return carry

    lax.fori_loop(0, nc, pass0, 0, unroll=8 if nc % 8 == 0 else 1)
    for p in range(npairs):
        st_ref[p] = h0_ref[0, 1, p]
    lax.fori_loop(0, nc, pass1, 0, unroll=8 if nc % 8 == 0 else 2)
    for p in range(npairs):
        ht_ref[0, 1, p] = st_ref[p]
        st_ref[p] = h0_ref[0, 0, p]
    lax.fori_loop(0, nc, pass2, 0, unroll=8 if nc % 8 == 0 else 2)
    for p in range(npairs):
        ht_ref[0, 0, p] = st_ref[p]


@functools.lru_cache(maxsize=None)
def _ssd_tables():
    lanes = np.arange(LANES)[:, None]
    src = np.where(lanes < 3 * SSD_DT, lanes % SSD_DT, -1)
    ecol = (src == np.arange(SSD_DT * LANES)[None, :] // LANES)
    ehead = (src == np.arange(2 * BRANCH_W)[None, :] // SSD_HD)
    even_head = (np.arange(BRANCH_W) // SSD_HD) % 2 == 0
    hmask = np.zeros((BF16_ROWS, BRANCH_W))
    hmask[0] = even_head
    hmask[1] = ~even_head
    as_bf16 = lambda a: jnp.asarray(a, dtype=F32).astype(BF16)
    return as_bf16(ecol), as_bf16(ehead), as_bf16(hmask)


def _ssd(xbc, z, dt, pw, l, h0):
    bsz, t, _ = xbc.shape
    assert t % SSD_CHUNK == 0
    npairs = SSD_HEADS // 2
    ecol, ehead, hmask = _ssd_tables()
    full = lambda shape: pl.BlockSpec(shape, lambda b: (0,) * len(shape))
    lay = lambda *shape: _layer_spec(l, shape)
    seq = lambda width: pl.BlockSpec((1, t, width), lambda b: (b, 0, 0))
    st_spec = pl.BlockSpec((1, 2, npairs, SSD_STATE, LANES), lambda b: (b, 0, 0, 0, 0))
    return pl.pallas_call(
        functools.partial(_ssd_kernel, t=t),
        grid=(bsz,),
        in_specs=[seq(SSD_XBC), seq(BRANCH_W), seq(LANES), lay(1, LANES), lay(1, LANES),
                  lay(1, BRANCH_W), lay(1, BRANCH_W), full(ecol.shape), full(ehead.shape), full(hmask.shape),
                  st_spec],
        out_specs=[seq(BRANCH_W), st_spec],
        out_shape=[jax.ShapeDtypeStruct((bsz, t, BRANCH_W), BF16),
                   jax.ShapeDtypeStruct((bsz, 2, npairs, SSD_STATE, LANES), F32)],
        scratch_shapes=[pltpu.VMEM((t, BRANCH_W), F32), pltpu.VMEM((t, LANES), BF16)]
        + [pltpu.VMEM((t // SSD_CHUNK * SSD_DT, LANES), F32)] * 4
        + [pltpu.VMEM((t // SSD_CHUNK, npairs, SSD_STATE, LANES), F32),
                        pltpu.VMEM((npairs, SSD_STATE, LANES), F32)],
        compiler_params=_params("arbitrary"),
        name="ssd",
    )(xbc, z, dt, pw["ssd_dtb"], pw["ssd_alog"], pw["ssd_dskip"], pw["ssd_nw"], ecol, ehead, hmask, h0)


@functools.lru_cache(maxsize=None)
def _fourier_tables(n, blk):
    half = n // 2
    nh = half // blk
    assert nh * blk == half and nh <= SUBLANES
    p = np.arange(FN_GD)
    ang_c = 2.0 * np.pi * ((p[:, None] * p[None, :]) % FN_GD) / FN_GD
    eye = np.eye(BRANCH_W // FN_GD)
    cc = np.kron(eye, np.cos(ang_c))
    sc = np.kron(eye, np.sin(ang_c))
    k = np.arange(half)
    ang_t = 2.0 * np.pi * ((k[:, None] * k[None, :]) % n) / n
    ct = np.cos(ang_t)
    stn = -np.sin(ang_t)
    assert blk % FN_FLIP == 0
    jp = np.zeros((FN_FLIP, FN_FLIP))
    r = np.arange(1, FN_FLIP)
    jp[r, FN_FLIP - r] = 1.0
    tt = np.arange(n)
    ks = blk * (np.arange(nh) + 1)
    ang_k = 2.0 * np.pi * ((ks[:, None] * tt[None, :]) % n) / n
    tc = np.zeros((2 * SUBLANES, n))
    ts = np.zeros((2 * SUBLANES, n))
    tc[:nh] = np.cos(ang_k)
    ts[:nh] = np.sin(ang_k)
    tc[SUBLANES, half] = 1.0
    as_bf16 = lambda a: jnp.asarray(a, dtype=F32).astype(BF16)
    return dict(cc=as_bf16(cc), sc=as_bf16(sc), ct=as_bf16(ct), stn=as_bf16(stn), jp=as_bf16(jp),
                tc=as_bf16(tc), ts=as_bf16(ts))


def _mirror(jp_ref, src, row0):
    n = src.shape[0]
    nsub = n // FN_FLIP
    first = lax.broadcasted_iota(jnp.int32, (FN_FLIP, src.shape[1]), 0) == 0
    out = []
    for a in range(nsub):
        s = nsub - 1 - a
        head = row0 if a == 0 else src[(s + 1) * FN_FLIP:(s + 1) * FN_FLIP + 1].astype(F32)
        out.append(jnp.where(first, head, _dot(jp_ref[...], src[s * FN_FLIP:(s + 1) * FN_FLIP])))
    return jnp.concatenate(out, axis=0)


def _fn_fold_kernel(xj_ref, xm_ref, xr_ref, jp_ref, cc_ref, sc_ref, eo_ref):
    j = pl.program_id(1)
    xj = xj_ref[0]
    row0 = xr_ref[0, 0:1, :].astype(F32) * jnp.where(j == 0, 0.0, 1.0)
    mir = _mirror(jp_ref, xm_ref[0], row0).astype(BF16)
    e = _dot(xj, cc_ref[...]) + _dot(mir, cc_ref[...])
    o = _dot(xj, sc_ref[...]) - _dot(mir, sc_ref[...])
    eo_ref[0] = jnp.concatenate([e, o], axis=1).astype(BF16)


def _fn_aux_kernel(x_ref, tc_ref, ts_ref, cc_ref, sc_ref, o_ref):
    x = x_ref[0]
    xc = _dot(tc_ref[...], x).astype(BF16)
    xs = _dot(ts_ref[...], x).astype(BF16)
    o_ref[0] = _dot(xc, cc_ref[...]) + _dot(xs, sc_ref[...])


def _fn_main_kernel(ct_ref, stn_ref, eo_ref, aux_ref, g_ref, jp_ref, y_ref, *, scale):
    m = pl.program_id(1)
    blk = ct_ref.shape[0]
    nblk = y_ref.shape[1] // blk
    rows = lax.broadcasted_iota(jnp.int32, (blk, BRANCH_W), 0)
    p = _dot(ct_ref[...], eo_ref[0, :, 0:BRANCH_W])
    qn = _dot(stn_ref[...], eo_ref[0, :, BRANCH_W:2 * BRANCH_W])
    sgn = (1 - 2 * ((m * blk + rows) & 1)).astype(F32)
    p = p + sgn * aux_ref[0, SUBLANES:SUBLANES + 1, :]
    lo = pl.multiple_of(m * blk, blk)
    hi = pl.multiple_of((nblk - 1 - m) * blk, blk)
    y_ref[0, pl.ds(lo, blk), :] = ((p + qn) * (_silu(g_ref[0, pl.ds(lo, blk), :].astype(F32)) * scale)).astype(BF16)
    sel = lax.broadcasted_iota(jnp.int32, (2 * SUBLANES, BRANCH_W), 0) == m
    row0 = jnp.sum(jnp.where(sel, aux_ref[0], 0.0), axis=0, keepdims=True)
    y_ref[0, pl.ds(hi, blk), :] = (_mirror(jp_ref, (p - qn).astype(BF16), row0)
                                   * (_silu(g_ref[0, pl.ds(hi, blk), :].astype(F32)) * scale)).astype(BF16)


def _fourier(fn, blk):
    bsz, n, _ = fn.shape
    tb = _fourier_tables(n, blk)
    half = n // 2
    nh = half // blk
    nblk = 2 * nh
    rows16 = n // BF16_ROWS
    full2 = lambda shape: pl.BlockSpec(shape, lambda b, j: (0,) * len(shape))
    eo = pl.pallas_call(
        _fn_fold_kernel,
        grid=(bsz, nh),
        in_specs=[pl.BlockSpec((1, blk, BRANCH_W), lambda b, j: (b, j, 0)),
                  pl.BlockSpec((1, blk, BRANCH_W), lambda b, j: (b, nblk - 1 - j, 0)),
                  pl.BlockSpec((1, BF16_ROWS, BRANCH_W),
                               lambda b, j: (b, jnp.minimum((blk // BF16_ROWS) * (nblk - j), rows16 - 1), 0)),
                  full2((FN_FLIP, FN_FLIP)), full2((BRANCH_W, BRANCH_W)), full2((BRANCH_W, BRANCH_W))],
        out_specs=pl.BlockSpec((1, blk, 2 * BRANCH_W), lambda b, j: (b, j, 0)),
        out_shape=jax.ShapeDtypeStruct((bsz, half, 2 * BRANCH_W), BF16),
        compiler_params=_params("arbitrary", "arbitrary"),
        name="fn_fold",
    )(fn, fn, fn, tb["jp"], tb["cc"], tb["sc"])
    full1 = lambda shape: pl.BlockSpec(shape, lambda b: (0,) * len(shape))
    aux = pl.pallas_call(
        _fn_aux_kernel,
        grid=(bsz,),
        in_specs=[pl.BlockSpec((1, n, BRANCH_W), lambda b: (b, 0, 0)),
                  full1((2 * SUBLANES, n)), full1((2 * SUBLANES, n)),
                  full1((BRANCH_W, BRANCH_W)), full1((BRANCH_W, BRANCH_W))],
        out_specs=pl.BlockSpec((1, 2 * SUBLANES, BRANCH_W), lambda b: (b, 0, 0)),
        out_shape=jax.ShapeDtypeStruct((bsz, 2 * SUBLANES, BRANCH_W), F32),
        compiler_params=_params("arbitrary"),
        name="fn_aux",
    )(fn, tb["tc"], tb["ts"], tb["cc"], tb["sc"])
    return pl.pallas_call(
        functools.partial(_fn_main_kernel, scale=float(1.0 / np.sqrt(n * FN_GD))),
        grid=(bsz, nh),
        in_specs=[pl.BlockSpec((blk, half), lambda b, m: (m, 0)),
                  pl.BlockSpec((blk, half), lambda b, m: (m, 0)),
                  pl.BlockSpec((1, half, 2 * BRANCH_W), lambda b, m: (b, 0, 0)),
                  pl.BlockSpec((1, 2 * SUBLANES, BRANCH_W), lambda b, m: (b, 0, 0)),
                  pl.BlockSpec((1, n, BRANCH_W), lambda b, m: (b, 0, 1)),
                  full2((FN_FLIP, FN_FLIP))],
        out_specs=pl.BlockSpec((1, n, BRANCH_W), lambda b, m: (b, 0, 0)),
        out_shape=jax.ShapeDtypeStruct((bsz, n, BRANCH_W), BF16),
        compiler_params=_params("arbitrary", "arbitrary"),
        name="fn_main",
    )(tb["ct"], tb["stn"], eo, aux, fn, tb["jp"])


def _prepare_weights(norm_w, w_in, w_out, rg_conv_w, rg_conv_b, rg_gate_a_w, rg_gate_a_b, rg_gate_x_w, rg_gate_x_b,
                     rg_lambda, sc_conv_w, ssd_conv_w, ssd_conv_b, ssd_dt_bias, ssd_a_log, ssd_d, ssd_norm_w,
                     final_norm_w):
    depth = w_in.shape[0]
    assert REF_DT_HI - REF_DT_LO == SSD_DT and w_in.shape[2] - SSD_DT + LANES == C_END
    lane = np.arange(LANES)
    dt_rep = jnp.asarray(lane[:, None] == lane[None, :] % SSD_DT, dtype=F32).astype(BF16)
    w_a = w_in[:, :, :REF_DT_LO + LANES].astype(BF16)
    w_b = w_in[:, :, REF_DT_HI:].astype(BF16)
    w_a, w_b = lax.optimization_barrier((w_a, w_b))
    heads_per_half = RG_HALF // RG_HD
    eye = jnp.eye(heads_per_half, dtype=F32)

    def blockdiag(wg):
        wg = wg.reshape(depth, 2, -1, heads_per_half, RG_HD, RG_HD)
        return jnp.einsum("ldhjio,jk->ldhjiko", wg, eye).reshape(depth, 2, -1, RG_HALF, RG_HALF)

    halves = lambda b: b.reshape(depth, 2, -1, 1, RG_HALF)
    rep = LANES // SSD_DT
    return dict(
        norm_w=norm_w.reshape(depth, 1, D_MODEL), final_w=final_norm_w.reshape(1, D_MODEL),
        w_in_a=w_a, w_in_b=w_b, dt_rep=dt_rep, w_out=w_out.astype(BF16), sc_cw=sc_conv_w,
        rg_cw=0.5 * rg_conv_w, rg_cb=0.5 * rg_conv_b.reshape(depth, 1, BRANCH_W),
        rg_wg=jnp.concatenate([blockdiag(rg_gate_a_w), blockdiag(rg_gate_x_w)], axis=-1).astype(BF16),
        rg_bg=0.5 * jnp.concatenate([halves(rg_gate_a_b), halves(rg_gate_x_b)], axis=-1),
        rg_lam=rg_lambda,
        ssd_cw=ssd_conv_w, ssd_cb=ssd_conv_b.reshape(depth, 1, SSD_XBC),
        ssd_dtb=jnp.tile(ssd_dt_bias.reshape(depth, 1, SSD_DT), (1, 1, rep)),
        ssd_alog=jnp.tile(ssd_a_log.reshape(depth, 1, SSD_DT), (1, 1, rep)),
        ssd_dskip=jnp.repeat(ssd_d, SSD_HD, axis=1).reshape(depth, 1, BRANCH_W),
        ssd_nw=ssd_norm_w.reshape(depth, 1, BRANCH_W),
    )


def _mix(x, mod, pw, l, mod_row, rowlen, tm, tt, blk, rg_h0, ssd_h0, with_output, final):
    rg, xbc, z, dt, *rest = _inproj(x, mod, pw, l, rowlen, tm, mod_row, scan_only=not with_output)
    y_rg, rg_st = _rg(rg, pw, l, rg_h0, tt)
    y_ssd, ssd_st = _ssd(xbc, z, dt, pw, l, ssd_h0)
    if not with_output:
        return None, rg_st, ssd_st
    fn, y_sc = rest
    y_fn = _fourier(fn, blk)
    return _outproj(y_rg, y_sc, y_fn, y_ssd, x, mod, pw, l, final, tm, mod_row), rg_st, ssd_st


def kernel(x, c, ctx, c_ctx, ada_w, ada_b, norm_w, w_in, w_out, rg_conv_w, rg_conv_b, rg_gate_a_w, rg_gate_a_b,
           rg_gate_x_w, rg_gate_x_b, rg_lambda, sc_conv_w, ssd_conv_w, ssd_conv_b, ssd_dt_bias, ssd_a_log, ssd_d,
           ssd_norm_w, final_norm_w):
    bsz, seq, _ = x.shape
    ctx_len = ctx.shape[1]
    depth = w_in.shape[0]
    assert bsz + 1 <= SUBLANES and seq % SUBLANES == 0
    pw = _prepare_weights(norm_w, w_in, w_out, rg_conv_w, rg_conv_b, rg_gate_a_w, rg_gate_a_b, rg_gate_x_w,
                          rg_gate_x_b, rg_lambda, sc_conv_w, ssd_conv_w, ssd_conv_b, ssd_dt_bias, ssd_a_log, ssd_d,
                          ssd_norm_w, final_norm_w)
    c8 = jnp.concatenate([c, c_ctx[None, :], jnp.zeros((SUBLANES - bsz - 1, D_MODEL), F32)], axis=0)
    mod = _ada(c8, ada_w, ada_b.reshape(depth, 1, 3 * D_MODEL))
    zeros_rg = jnp.zeros((bsz, SUBLANES, BRANCH_W), F32)
    zeros_ssd = jnp.zeros((bsz, 2, SSD_HEADS // 2, SSD_STATE, LANES), F32)
    tm_lat = min(1024, seq)
    tt_lat = min(2048, seq)
    tt_ctx = min(1024, ctx_len)
    blk_lat = min(512, seq // 2)
    blk_ctx = min(512, ctx_len // 2)
    for l in range(depth):
        last = l == depth - 1
        new_ctx, rg_st, ssd_st = _mix(ctx, mod, pw, l, bsz, ctx_len, ctx_len, tt_ctx, blk_ctx,
                                      zeros_rg, zeros_ssd, not last, False)
        x, _, _ = _mix(x, mod, pw, l, None, GRID_W, tm_lat, tt_lat, blk_lat, rg_st, ssd_st, True, last)
        if not last:
            ctx = new_ctx
    return x
```
